```python
import math
import jax
import jax.numpy as jnp
from jax import lax
import numpy as np

D_MODEL = 1024
BATCH = 4
SEQ = 4096
DEPTH = 2

N_MIXERS = 2
CHUNK = 64
NORM_EPS = 1e-6

ML_HEADS = 4
ML_QK_DIM = D_MODEL // (2 * ML_HEADS)
ML_V_DIM = D_MODEL // ML_HEADS
ML_QK_WIDTH = ML_HEADS * ML_QK_DIM
ML_V_WIDTH = ML_HEADS * ML_V_DIM
ML_PROJ = 2 * ML_QK_WIDTH + 2 * ML_V_WIDTH + 2 * ML_HEADS
ML_GATE_CAP = 15.0

GDN_K_HEADS = 8
GDN_V_HEADS = 16
GDN_HEAD_DIM = 128
GDN_KEY_WIDTH = GDN_K_HEADS * GDN_HEAD_DIM
GDN_VAL_WIDTH = GDN_V_HEADS * GDN_HEAD_DIM
GDN_CONV_DIM = 2 * GDN_KEY_WIDTH + GDN_VAL_WIDTH
GDN_CONV_WIDTH = 4
GDN_PROJ = GDN_CONV_DIM + GDN_VAL_WIDTH + 2 * GDN_V_HEADS

D_FF = ((8 * D_MODEL + 3 * 256 - 1) // (3 * 256)) * 256

N_ML_LAYERS = (DEPTH + N_MIXERS - 1) // N_MIXERS
N_GDN_LAYERS = DEPTH // N_MIXERS

kernel_name = 'hybrid_mlstm_gdn_adaln_block'


def rms_norm(x, eps=NORM_EPS):
    xf = x.astype(jnp.float32)
    y = xf * lax.rsqrt(jnp.mean(xf * xf, axis=-1, keepdims=True) + eps)
    return y.astype(x.dtype)


def l2_normalize(x, eps=NORM_EPS):
    return x * lax.rsqrt(jnp.sum(x * x, axis=-1, keepdims=True) + eps)


def to_chunks(t):
    b, s, h, d = t.shape
    return t.reshape(b, s // CHUNK, CHUNK, h, d).transpose(0, 3, 1, 2, 4)


def from_chunks(t):
    b, h, nc, l, d = t.shape
    return t.transpose(0, 2, 3, 1, 4).reshape(b, nc * l, h, d)


def causal_depthwise_conv(x, w):
    k = w.shape[0]
    return lax.conv_general_dilated(
        x, w[:, None, :], window_strides=(1,), padding=[(k - 1, 0)],
        dimension_numbers=('NWC', 'WIO', 'NWC'), feature_group_count=x.shape[-1])


def mlstm_chunkwise(q, k, v, log_i, log_f):
    L = q.shape[-2]
    idx = jnp.arange(L)
    causal = idx[:, None] >= idx[None, :]
    b = jnp.cumsum(log_f, axis=-1)
    b_last = b[..., -1]
    d_mat = jnp.where(causal, b[..., :, None] - b[..., None, :] + log_i[..., None, :], -jnp.inf)
    m_intra = jnp.max(d_mat, axis=-1)
    g_end = b_last[..., None] - b + log_i
    m_loc = jnp.max(g_end, axis=-1)
    k_w = k * jnp.exp(g_end - m_loc[..., None])[..., None]
    d_c = jnp.einsum('bhclk,bhclv->bhckv', k_w, v)
    d_n = jnp.sum(k_w, axis=-2)

    def step(carry, xs):
        c_st, n_st, m_st = carry
        dc_c, dn_c, mloc_c, bl_c = xs
        m_new = jnp.maximum(bl_c + m_st, mloc_c)
        a_old = jnp.exp(bl_c + m_st - m_new)
        a_new = jnp.exp(mloc_c - m_new)
        c_new = a_old[..., None, None] * c_st + a_new[..., None, None] * dc_c
        n_new = a_old[..., None] * n_st + a_new[..., None] * dn_c
        return (c_new, n_new, m_new), (c_st, n_st, m_st)

    bsz, nh, _, _, dk = q.shape
    dv = v.shape[-1]
    init = (jnp.zeros((bsz, nh, dk, dv), q.dtype), jnp.zeros((bsz, nh, dk), q.dtype),
            jnp.zeros((bsz, nh), q.dtype))
    xs = tuple(jnp.moveaxis(t, 2, 0) for t in (d_c, d_n, m_loc, b_last))
    _, (c_prev, n_prev, m_prev) = lax.scan(step, init, xs)
    c_prev = jnp.moveaxis(c_prev, 0, 2)
    n_prev = jnp.moveaxis(n_prev, 0, 2)
    m_prev = jnp.moveaxis(m_prev, 0, 2)

    m_inter = b + m_prev[..., None]
    m_t = jnp.maximum(m_inter, m_intra)
    s_inter = jnp.exp(m_inter - m_t)
    p = jnp.exp(d_mat - m_t[..., None]) * jnp.einsum('bhcld,bhcsd->bhcls', q, k)
    num = (s_inter[..., None] * jnp.einsum('bhcld,bhcdv->bhclv', q, c_prev)
           + jnp.einsum('bhcls,bhcsv->bhclv', p, v))
    den = s_inter * jnp.einsum('bhcld,bhcd->bhcl', q, n_prev) + jnp.sum(p, axis=-1)
    return num / jnp.maximum(jnp.abs(den), jnp.exp(-m_t))[..., None]


def gated_delta_chunkwise(q, k, v, g, beta):
    L = q.shape[-2]
    dv = v.shape[-1]
    idx = jnp.arange(L)
    causal = idx[:, None] >= idx[None, :]
    strict = idx[:, None] > idx[None, :]
    G = jnp.cumsum(g, axis=-1)
    decay = jnp.exp(jnp.where(causal, G[..., :, None] - G[..., None, :], -jnp.inf))
    k_beta = k * beta[..., None]
    a_mat = jnp.where(strict, jnp.einsum('bhcld,bhcsd->bhcls', k_beta, k) * decay, 0.0)
    t_mat = a_mat + jnp.eye(L, dtype=a_mat.dtype)
    rhs = jnp.concatenate([v * beta[..., None], k_beta * jnp.exp(G)[..., None]], axis=-1)
    sol = lax.linalg.triangular_solve(t_mat, rhs, left_side=True, lower=True, unit_diagonal=True)
    u, w = sol[..., :dv], sol[..., dv:]
    attn = jnp.einsum('bhcld,bhcsd->bhcls', q, k) * decay
    q_dec = q * jnp.exp(G)[..., None]
    g_last = G[..., -1]
    k_dec = k * jnp.exp(g_last[..., None] - G)[..., None]

    def step(state, xs):
        q_c, k_c, u_c, w_c, attn_c, gl_c = xs
        v_new = u_c - jnp.einsum('bhlk,bhkv->bhlv', w_c, state)
        out = (jnp.einsum('bhlk,bhkv->bhlv', q_c, state)
               + jnp.einsum('bhls,bhsv->bhlv', attn_c, v_new))
        state = state * jnp.exp(gl_c)[..., None, None] + jnp.einsum('bhlk,bhlv->bhkv', k_c, v_new)
        return state, out

    bsz, nh, _, _, dk = q.shape
    init = jnp.zeros((bsz, nh, dk, dv), q.dtype)
    xs = tuple(jnp.moveaxis(t, 2, 0) for t in (q_dec, k_dec, u, w, attn, g_last))
    _, out = lax.scan(step, init, xs)
    return jnp.moveaxis(out, 0, 2)


def mlstm_mixer(h, w_in, b_if, norm_w, w_out):
    bsz, seq, _ = h.shape
    f32 = jnp.float32
    proj = h @ w_in
    q, k, v, o, gates = jnp.split(
        proj, [ML_QK_WIDTH, 2 * ML_QK_WIDTH, 2 * ML_QK_WIDTH + ML_V_WIDTH,
               2 * ML_QK_WIDTH + 2 * ML_V_WIDTH], axis=-1)
    q = q.astype(f32).reshape(bsz, seq, ML_HEADS, ML_QK_DIM)
    k = k.astype(f32).reshape(bsz, seq, ML_HEADS, ML_QK_DIM) * (ML_QK_DIM ** -0.5)
    v = v.astype(f32).reshape(bsz, seq, ML_HEADS, ML_V_DIM)
    gates = gates.astype(f32) + b_if.astype(f32)
    gates = ML_GATE_CAP * jnp.tanh(gates / ML_GATE_CAP)
    log_i, f_pre = jnp.split(gates, 2, axis=-1)
    log_f = jax.nn.log_sigmoid(f_pre)
    hc = mlstm_chunkwise(to_chunks(q), to_chunks(k), to_chunks(v),
                         to_chunks(log_i[..., None])[..., 0], to_chunks(log_f[..., None])[..., 0])
    hc = rms_norm(from_chunks(hc)).reshape(bsz, seq, ML_V_WIDTH) * norm_w.astype(f32)
    out = hc * jax.nn.sigmoid(o.astype(f32))
    return out.astype(h.dtype) @ w_out


def gdn_mixer(h, w_in, conv_w, a_log, dt_bias, norm_w, w_out):
    bsz, seq, _ = h.shape
    f32 = jnp.float32
    proj = h @ w_in
    qkv, z, b_raw, a_raw = jnp.split(
        proj, [GDN_CONV_DIM, GDN_CONV_DIM + GDN_VAL_WIDTH,
               GDN_CONV_DIM + GDN_VAL_WIDTH + GDN_V_HEADS], axis=-1)
    qkv = jax.nn.silu(causal_depthwise_conv(qkv, conv_w)).astype(f32)
    q, k, v = jnp.split(qkv, [GDN_KEY_WIDTH, 2 * GDN_KEY_WIDTH], axis=-1)
    rep = GDN_V_HEADS // GDN_K_HEADS
    q = l2_normalize(q.reshape(bsz, seq, GDN_K_HEADS, GDN_HEAD_DIM)) * (GDN_HEAD_DIM ** -0.5)
    k = l2_normalize(k.reshape(bsz, seq, GDN_K_HEADS, GDN_HEAD_DIM))
    q = jnp.repeat(q, rep, axis=2)
    k = jnp.repeat(k, rep, axis=2)
    v = v.reshape(bsz, seq, GDN_V_HEADS, GDN_HEAD_DIM)
    beta = jax.nn.sigmoid(b_raw.astype(f32))
    g = -jnp.exp(a_log.astype(f32)) * jax.nn.softplus(a_raw.astype(f32) + dt_bias.astype(f32))
    o = gated_delta_chunkwise(to_chunks(q), to_chunks(k), to_chunks(v),
                              to_chunks(g[..., None])[..., 0], to_chunks(beta[..., None])[..., 0])
    o = from_chunks(o)
    z = z.astype(f32).reshape(bsz, seq, GDN_V_HEADS, GDN_HEAD_DIM)
    o = rms_norm(o) * norm_w.astype(f32) * jax.nn.silu(z)
    return o.reshape(bsz, seq, GDN_VAL_WIDTH).astype(h.dtype) @ w_out


def swiglu(h, w_gate_up, w_down):
    gate, up = jnp.split(h @ w_gate_up, 2, axis=-1)
    return (jax.nn.silu(gate) * up) @ w_down


def setup_inputs(seed: int = 0) -> dict:
    key = jax.random.key(seed)
    ks = jax.random.split(key, 20)
    f32 = jnp.float32

    def nrm(k, shape, scale):
        return jax.random.normal(k, shape, f32) * scale

    x = nrm(ks[0], (BATCH, SEQ, D_MODEL), 1.0)
    c = nrm(ks[1], (BATCH, D_MODEL), 1.0)
    ada_w = nrm(ks[2], (DEPTH, D_MODEL, 6 * D_MODEL), 0.5 * D_MODEL ** -0.5)
    ada_b = nrm(ks[3], (DEPTH, 6 * D_MODEL), 0.02)
    ml_w_in = nrm(ks[4], (N_ML_LAYERS, D_MODEL, ML_PROJ), D_MODEL ** -0.5)
    i_bias = nrm(ks[5], (N_ML_LAYERS, ML_HEADS), 0.1)
    f_bias = jnp.linspace(3.0, 6.0, ML_HEADS, dtype=f32)[None, :] + nrm(ks[6], (N_ML_LAYERS, ML_HEADS), 0.1)
    ml_b_if = jnp.concatenate([i_bias, f_bias], axis=-1)
    ml_norm_w = 1.0 + nrm(ks[7], (N_ML_LAYERS, ML_V_WIDTH), 0.02)
    ml_w_out = nrm(ks[8], (N_ML_LAYERS, ML_V_WIDTH, D_MODEL), ML_V_WIDTH ** -0.5)
    gdn_w_in = nrm(ks[9], (N_GDN_LAYERS, D_MODEL, GDN_PROJ), D_MODEL ** -0.5)
    gdn_conv_w = nrm(ks[10], (N_GDN_LAYERS, GDN_CONV_WIDTH, GDN_CONV_DIM), GDN_CONV_WIDTH ** -0.5)
    gdn_a_log = jnp.log(jax.random.uniform(ks[11], (N_GDN_LAYERS, GDN_V_HEADS), f32, 1.0, 16.0))
    dt = jnp.exp(jax.random.uniform(ks[12], (N_GDN_LAYERS, GDN_V_HEADS), f32,
                                    math.log(1e-3), math.log(1e-1)))
    gdn_dt_bias = dt + jnp.log(-jnp.expm1(-dt))
    gdn_norm_w = 1.0 + nrm(ks[13], (N_GDN_LAYERS, GDN_HEAD_DIM), 0.02)
    gdn_w_out = nrm(ks[14], (N_GDN_LAYERS, GDN_VAL_WIDTH, D_MODEL), GDN_VAL_WIDTH ** -0.5)
    ffn_w_gate_up = nrm(ks[15], (DEPTH, D_MODEL, 2 * D_FF), D_MODEL ** -0.5)
    ffn_w_down = nrm(ks[16], (DEPTH, D_FF, D_MODEL), D_FF ** -0.5)
    final_norm_w = 1.0 + nrm(ks[17], (D_MODEL,), 0.02)
    return {'x': x, 'c': c, 'ada_w': ada_w, 'ada_b': ada_b,
            'ml_w_in': ml_w_in, 'ml_b_if': ml_b_if, 'ml_norm_w': ml_norm_w, 'ml_w_out': ml_w_out,
            'gdn_w_in': gdn_w_in, 'gdn_conv_w': gdn_conv_w, 'gdn_a_log': gdn_a_log,
            'gdn_dt_bias': gdn_dt_bias, 'gdn_norm_w': gdn_norm_w, 'gdn_w_out': gdn_w_out,
            'ffn_w_gate_up': ffn_w_gate_up, 'ffn_w_down': ffn_w_down, 'final_norm_w': final_norm_w}


def reference(x, c, ada_w, ada_b, ml_w_in, ml_b_if, ml_norm_w, ml_w_out,
              gdn_w_in, gdn_conv_w, gdn_a_log, gdn_dt_bias, gdn_norm_w, gdn_w_out,
              ffn_w_gate_up, ffn_w_down, final_norm_w):
    c_act = jax.nn.silu(c)
    for layer in range(DEPTH):
        mod = c_act @ ada_w[layer] + ada_b[layer]
        sh1, sc1, g1, sh2, sc2, g2 = [m[:, None, :] for m in jnp.split(mod, 6, axis=-1)]
        h = rms_norm(x) * (1.0 + sc1) + sh1
        j = layer // N_MIXERS
        if layer % N_MIXERS == 0:
            y = mlstm_mixer(h, ml_w_in[j], ml_b_if[j], ml_norm_w[j], ml_w_out[j])
        else:
            y = gdn_mixer(h, gdn_w_in[j], gdn_conv_w[j], gdn_a_log[j], gdn_dt_bias[j],
                          gdn_norm_w[j], gdn_w_out[j])
        x = x + g1 * y
        h = rms_norm(x) * (1.0 + sc2) + sh2
        x = x + g2 * swiglu(h, ffn_w_gate_up[layer], ffn_w_down[layer])
    return rms_norm(x) * final_norm_w
```

```python
import functools

import jax
import jax.numpy as jnp
from jax import lax
from jax.experimental import pallas as pl
from jax.experimental.pallas import tpu as pltpu

F32 = jnp.float32
BF16 = jnp.bfloat16
HIGHEST = lax.Precision.HIGHEST

LANES = 128
VMEM_LIMIT = 56 * 1024 * 1024

D_MODEL = 1024
CHUNK = 64
NORM_EPS = 1e-6

ML_HEADS = 4
ML_QK_DIM = 128
ML_V_DIM = 256
ML_QK_WIDTH = ML_HEADS * ML_QK_DIM
ML_V_WIDTH = ML_HEADS * ML_V_DIM
ML_MAIN = 2 * ML_QK_WIDTH + 2 * ML_V_WIDTH
ML_GATE_ROWS = 16
ML_GATE_CAP = 15.0

GDN_K_HEADS = 8
GDN_V_HEADS = 16
GDN_HEAD_DIM = 128
GDN_KEY_WIDTH = GDN_K_HEADS * GDN_HEAD_DIM
GDN_VAL_WIDTH = GDN_V_HEADS * GDN_HEAD_DIM
GDN_CONV_DIM = 2 * GDN_KEY_WIDTH + GDN_VAL_WIDTH
GDN_CONV_WIDTH = 4
GDN_MAIN = GDN_CONV_DIM + GDN_VAL_WIDTH
GDN_GATE_ROWS = 2 * GDN_V_HEADS

INPROJ_TM = 512
INPROJ_TN = 1024
REC_ROWS = 256
CONV_ROWS = 512
CONV_COLS = 512
OUT_TM = 512
FFN_TM = 512
FFN_TH = 256


def _params(*sem):
    return pltpu.CompilerParams(dimension_semantics=sem, vmem_limit_bytes=VMEM_LIMIT)


def _sigmoid(x):
    return 1.0 / (1.0 + jnp.exp(-x))


def _softplus(x):
    return jnp.maximum(x, 0.0) + jnp.log1p(jnp.exp(-jnp.abs(x)))


def _dot(a, b):
    return jnp.dot(a, b, preferred_element_type=F32)


def _dot_nt(a, b):
    return lax.dot_general(a, b, (((1,), (1,)), ((), ())), preferred_element_type=F32)


def _dot_tn(a, b):
    return lax.dot_general(a, b, (((0,), (0,)), ((), ())), preferred_element_type=F32)


def _iota(shape, dim):
    return lax.broadcasted_iota(jnp.int32, shape, dim)


def _adaln_kernel(c_ref, w_ref, b_ref, o_ref):
    c = c_ref[...]
    act = c * _sigmoid(c)
    o_ref[0] = jnp.dot(act, w_ref[0], precision=HIGHEST, preferred_element_type=F32) + b_ref[0]


def _adaln(c_pad, ada_w, ada_b):
    depth, d, n = ada_w.shape
    rows = c_pad.shape[0]
    tn = D_MODEL
    return pl.pallas_call(
        _adaln_kernel,
        grid=(depth, n // tn),
        in_specs=[
            pl.BlockSpec((rows, d), lambda l, j: (0, 0)),
            pl.BlockSpec((1, d, tn), lambda l, j: (l, 0, j)),
            pl.BlockSpec((1, 1, tn), lambda l, j: (l, 0, j)),
        ],
        out_specs=pl.BlockSpec((1, rows, tn), lambda l, j: (l, 0, j)),
        out_shape=jax.ShapeDtypeStruct((depth, rows, n), F32),
        compiler_params=_params("parallel", "parallel"),
        name="adaln",
    )(c_pad, ada_w, ada_b.reshape(depth, 1, n))


def _modulated_norm(x, scale, shift):
    ms = jnp.mean(x * x, axis=-1, keepdims=True)
    return x * lax.rsqrt(ms + NORM_EPS) * (1.0 + scale) + shift


def _ml_gates(pre, idx):
    capped = ML_GATE_CAP * jnp.tanh(pre / ML_GATE_CAP)
    log_f = jnp.minimum(capped, 0.0) - jnp.log1p(jnp.exp(-jnp.abs(capped)))
    is_f = idx >= ML_HEADS
    return jnp.where(is_f, log_f, capped), is_f


def _gdn_gates(pre, a_log, idx):
    is_g = idx >= GDN_V_HEADS
    return jnp.where(is_g, -jnp.exp(a_log) * _softplus(pre), _sigmoid(pre)), is_g


def _inproj_kernel(x_ref, sc_ref, sh_ref, w_ref, wg_ref, wgt_ref, bcol_ref, brow_ref, acol_ref, arow_ref,
                   proj_ref, gcol_ref, grow_ref, h_scr, *, mode):
    tm = x_ref.shape[0]
    n_rows = wgt_ref.shape[0]

    @pl.when(pl.program_id(1) == 0)
    def _():
        hb = _modulated_norm(x_ref[...], sc_ref[0], sh_ref[0]).astype(BF16)
        h_scr[...] = hb

        pre = _dot(hb, wg_ref[...]) + bcol_ref[...]
        lane = _iota((tm, LANES), 1)
        if mode == "ml":
            vals, cum = _ml_gates(pre, lane)
        else:
            vals, cum = _gdn_gates(pre, acol_ref[...], lane)
        r = _iota((tm, tm), 0)
        c = _iota((tm, tm), 1)
        tri = jnp.where((r // CHUNK == c // CHUNK) & (c <= r), 1.0, 0.0).astype(F32)
        csum = jnp.dot(tri, vals, precision=HIGHEST, preferred_element_type=F32)
        gcol_ref[...] = jnp.where(cum, csum, vals)

        rr = _iota((CHUNK, CHUNK), 0)
        cc = _iota((CHUNK, CHUNK), 1)
        upper = jnp.where(rr <= cc, 1.0, 0.0).astype(F32)
        row = _iota((n_rows, CHUNK), 0)
        for ch in range(tm // CHUNK):
            pre_t = _dot_nt(wgt_ref[...], h_scr[ch * CHUNK:(ch + 1) * CHUNK, :]) + brow_ref[...]
            if mode == "ml":
                vals_t, cum_t = _ml_gates(pre_t, row)
            else:
                vals_t, cum_t = _gdn_gates(pre_t, arow_ref[...], row)
            csum_t = jnp.dot(vals_t, upper, precision=HIGHEST, preferred_element_type=F32)
            grow_ref[ch] = jnp.where(cum_t, csum_t, vals_t)

    proj_ref[...] = _dot(h_scr[...], w_ref[...])


def _inproj(x2, scale, shift, w_main, w_gate, w_gate_t, bias_col, bias_row, alog_col, alog_row, *, mode, seq):
    tokens, d = x2.shape
    n = w_main.shape[1]
    n_rows = w_gate_t.shape[0]
    tm, tn = INPROJ_TM, INPROJ_TN
    per_batch = seq // tm
    const = lambda i, j: (0, 0)
    return pl.pallas_call(
        functools.partial(_inproj_kernel, mode=mode),
        grid=(tokens // tm, n // tn),
        in_specs=[
            pl.BlockSpec((tm, d), lambda i, j: (i, 0)),
            pl.BlockSpec((1, 1, d), lambda i, j: (i // per_batch, 0, 0)),
            pl.BlockSpec((1, 1, d), lambda i, j: (i // per_batch, 0, 0)),
            pl.BlockSpec((d, tn), lambda i, j: (0, j)),
            pl.BlockSpec((d, LANES), const),
            pl.BlockSpec((n_rows, d), const),
            pl.BlockSpec((1, LANES), const),
            pl.BlockSpec((n_rows, 1), const),
            pl.BlockSpec((1, LANES), const),
            pl.BlockSpec((n_rows, 1), const),
        ],
        out_specs=[
            pl.BlockSpec((tm, tn), lambda i, j: (i, j)),
            pl.BlockSpec((tm, LANES), lambda i, j: (i, 0)),
            pl.BlockSpec((tm // CHUNK, n_rows, CHUNK), lambda i, j: (i, 0, 0)),
        ],
        out_shape=[
            jax.ShapeDtypeStruct((tokens, n), F32),
            jax.ShapeDtypeStruct((tokens, LANES), F32),
            jax.ShapeDtypeStruct((tokens // CHUNK, n_rows, CHUNK), F32),
        ],
        scratch_shapes=[pltpu.VMEM((tm, d), BF16)],
        compiler_params=_params("parallel", "arbitrary"),
        name="inproj_" + mode,
    )(x2, scale, shift, w_main, w_gate, w_gate_t, bias_col, bias_row, alog_col, alog_row)


def _mlstm_kernel(q_ref, k_ref, v_ref, o_ref, gcol_ref, grow_ref, nw_ref, out_ref, c_scr, n_scr, m_scr):
    rows = q_ref.shape[0]

    @pl.when(pl.program_id(1) == 0)
    def _():
        c_scr[...] = jnp.zeros_like(c_scr)
        n_scr[...] = jnp.zeros_like(n_scr)
        m_scr[...] = jnp.zeros_like(m_scr)

    causal = _iota((CHUNK, CHUNK), 0) >= _iota((CHUNK, CHUNK), 1)
    k_scale = ML_QK_DIM ** -0.5

    for ch in range(rows // CHUNK):
        rs = slice(ch * CHUNK, (ch + 1) * CHUNK)
        gc = gcol_ref[rs, :]
        gr = grow_ref[ch]
        for h in range(ML_HEADS):
            qs = slice(h * ML_QK_DIM, (h + 1) * ML_QK_DIM)
            vs = slice(h * ML_V_DIM, (h + 1) * ML_V_DIM)
            li_c = gc[:, h:h + 1]
            b_c = gc[:, ML_HEADS + h:ML_HEADS + h + 1]
            li_r = gr[h:h + 1, :]
            b_r = gr[ML_HEADS + h:ML_HEADS + h + 1, :]
            b_last = b_r[:, CHUNK - 1:CHUNK]

            q = q_ref[rs, qs]
            k = k_ref[rs, qs]
            v = v_ref[rs, vs]
            qb = q.astype(BF16)
            vb = v.astype(BF16)
            c_prev = c_scr[h]
            n_prev = n_scr[h]
            m_prev = m_scr[h][:, 0:1]

            d_mat = jnp.where(causal, b_c - b_r + li_r, -jnp.inf)
            m_intra = jnp.max(d_mat, axis=1, keepdims=True)
            m_inter = b_c + m_prev
            m_t = jnp.maximum(m_inter, m_intra)
            s_inter = jnp.exp(m_inter - m_t)
            p = jnp.exp(d_mat - m_t) * (_dot_nt(qb, k.astype(BF16)) * k_scale)
            num = s_inter * _dot(qb, c_prev.astype(BF16)) + _dot(p.astype(BF16), vb)
            den = (s_inter * jnp.sum(q * n_prev, axis=1, keepdims=True)
                   + jnp.sum(p, axis=1, keepdims=True))
            hid = num / jnp.maximum(jnp.abs(den), jnp.exp(-m_t))

            ms = jnp.mean(hid * hid, axis=1, keepdims=True)
            o = o_ref[rs, vs]
            out = hid * lax.rsqrt(ms + NORM_EPS) * nw_ref[:, vs] * _sigmoid(o)
            out_ref[rs, vs] = out.astype(out_ref.dtype)

            g_end = b_last - b_c + li_c
            m_loc = jnp.max(g_end, axis=0, keepdims=True)
            k_w = k * (k_scale * jnp.exp(g_end - m_loc))
            d_c = _dot_tn(k_w.astype(BF16), vb)
            d_n = jnp.sum(k_w, axis=0, keepdims=True)
            m_new = jnp.maximum(b_last + m_prev, m_loc)
            a_old = jnp.exp(b_last + m_prev - m_new)
            a_new = jnp.exp(m_loc - m_new)
            c_scr[h] = a_old * c_prev + a_new * d_c
            n_scr[h] = a_old * n_prev + a_new * d_n
            m_scr[h] = jnp.broadcast_to(m_new, (1, LANES))


def _mlstm(proj, gcol, grow, norm_w, *, batch, seq):
    tokens = proj.shape[0]
    rows = REC_ROWS
    steps = seq // rows
    row = lambda b, t: b * steps + t
    return pl.pallas_call(
        _mlstm_kernel,
        grid=(batch, steps),
        in_specs=[
            pl.BlockSpec((rows, ML_QK_WIDTH), lambda b, t: (row(b, t), 0)),
            pl.BlockSpec((rows, ML_QK_WIDTH), lambda b, t: (row(b, t), 1)),
            pl.BlockSpec((rows, ML_V_WIDTH), lambda b, t: (row(b, t), 1)),
            pl.BlockSpec((rows, ML_V_WIDTH), lambda b, t: (row(b, t), 2)),
            pl.BlockSpec((rows, LANES), lambda b, t: (row(b, t), 0)),
            pl.BlockSpec((rows // CHUNK, ML_GATE_ROWS, CHUNK), lambda b, t: (row(b, t), 0, 0)),
            pl.BlockSpec((1, ML_V_WIDTH), lambda b, t: (0, 0)),
        ],
        out_specs=pl.BlockSpec((rows, ML_V_WIDTH), lambda b, t: (row(b, t), 0)),
        out_shape=jax.ShapeDtypeStruct((tokens, ML_V_WIDTH), BF16),
        scratch_shapes=[
            pltpu.VMEM((ML_HEADS, ML_QK_DIM, ML_V_DIM), F32),
            pltpu.VMEM((ML_HEADS, 1, ML_QK_DIM), F32),
            pltpu.VMEM((ML_HEADS, 1, LANES), F32),
        ],
        compiler_params=_params("parallel", "arbitrary"),
        name="mlstm",
    )(proj, proj, proj, proj, gcol, grow, norm_w)


def _gdn_conv_kernel(x_ref, w_ref, o_ref, buf):
    rows = x_ref.shape[0]
    halo = 8
    t = pl.program_id(2)

    @pl.when(t == 0)
    def _():
        buf[0:halo, :] = jnp.zeros((halo, buf.shape[1]), F32)

    @pl.when(t > 0)
    def _():
        buf[0:halo, :] = buf[rows:rows + halo, :]

    x = x_ref[...]
    buf[halo:halo + rows, :] = x
    w = w_ref[...]
    y = w[GDN_CONV_WIDTH - 1:GDN_CONV_WIDTH, :] * x
    for j in range(GDN_CONV_WIDTH - 1):
        shift = GDN_CONV_WIDTH - 1 - j
        y = y + w[j:j + 1, :] * buf[halo - shift:halo - shift + rows, :]
    y = y * _sigmoid(y)

    col_block = pl.program_id(1)
    qk_blocks = 2 * GDN_KEY_WIDTH // x_ref.shape[1]
    q_blocks = GDN_KEY_WIDTH // x_ref.shape[1]

    @pl.when(col_block >= qk_blocks)
    def _():
        o_ref[...] = y

    @pl.when(col_block < qk_blocks)
    def _():
        q_scale = jnp.where(col_block < q_blocks, GDN_HEAD_DIM ** -0.5, 1.0).astype(F32)
        for hh in range(x_ref.shape[1] // GDN_HEAD_DIM):
            hs = slice(hh * GDN_HEAD_DIM, (hh + 1) * GDN_HEAD_DIM)
            blk = y[:, hs]
            ss = jnp.sum(blk * blk, axis=1, keepdims=True)
            o_ref[:, hs] = blk * lax.rsqrt(ss + NORM_EPS) * q_scale


def _gdn_conv(proj, conv_w, *, batch, seq):
    tokens = proj.shape[0]
    rows, cols = CONV_ROWS, CONV_COLS
    steps = seq // rows
    return pl.pallas_call(
        _gdn_conv_kernel,
        grid=(batch, GDN_CONV_DIM // cols, steps),
        in_specs=[
            pl.BlockSpec((rows, cols), lambda b, c, t: (b * steps + t, c)),
            pl.BlockSpec((GDN_CONV_WIDTH, cols), lambda b, c, t: (0, c)),
        ],
        out_specs=pl.BlockSpec((rows, cols), lambda b, c, t: (b * steps + t, c)),
        out_shape=jax.ShapeDtypeStruct((tokens, GDN_CONV_DIM), F32),
        scratch_shapes=[pltpu.VMEM((rows + 8, cols), F32)],
        compiler_params=_params("parallel", "parallel", "arbitrary"),
        name="gdn_conv",
    )(proj, conv_w)


def _unit_lower_inverse(a, eye):
    x = eye - a
    p = a
    steps = CHUNK.bit_length() - 2
    for _ in range(steps):
        pb = p.astype(BF16)
        p = _dot(pb, pb)
        x = x + _dot(x.astype(BF16), p.astype(BF16))
    return x


def _gdn_kernel(q_ref, k_ref, v_ref, z_ref, gcol_ref, grow_ref, nw_ref, out_ref, s_scr):
    rows = q_ref.shape[0]
    rep = GDN_V_HEADS // GDN_K_HEADS
    dh = GDN_HEAD_DIM
    kh = pl.program_id(1)

    @pl.when(pl.program_id(2) == 0)
    def _():
        s_scr[...] = jnp.zeros_like(s_scr)

    ri = _iota((CHUNK, CHUNK), 0)
    ci = _iota((CHUNK, CHUNK), 1)
    causal = ri >= ci
    strict = ri > ci
    eye = jnp.where(ri == ci, 1.0, 0.0).astype(F32)
    lane = _iota((CHUNK, LANES), 1)

    for ch in range(rows // CHUNK):
        rs = slice(ch * CHUNK, (ch + 1) * CHUNK)
        q = q_ref[rs, :]
        k = k_ref[rs, :]
        qb = q.astype(BF16)
        kb = k.astype(BF16)
        kk = _dot_nt(kb, kb)
        qk = _dot_nt(qb, kb)
        gc = gcol_ref[rs, :]
        for j in range(rep):
            vh = kh * rep + j
            vs = slice(j * dh, (j + 1) * dh)
            beta_c = jnp.sum(jnp.where(lane == vh, gc, 0.0), axis=1, keepdims=True)
            g_c = jnp.sum(jnp.where(lane == GDN_V_HEADS + vh, gc, 0.0), axis=1, keepdims=True)
            g_r = grow_ref[ch, pl.ds(GDN_V_HEADS + vh, 1), :]
            g_last = g_r[:, CHUNK - 1:CHUNK]

            decay = jnp.exp(jnp.where(causal, g_c - g_r, -jnp.inf))
            a_mat = jnp.where(strict, kk * beta_c * decay, 0.0)
            t_inv = _unit_lower_inverse(a_mat, eye)

            v = v_ref[rs, vs]
            e_g = jnp.exp(g_c)
            rhs = jnp.concatenate([v * beta_c, k * (beta_c * e_g)], axis=1).astype(BF16)
            sol = _dot(t_inv.astype(BF16), rhs)
            u = sol[:, :dh]
            w = sol[:, dh:]

            attn = qk * decay
            q_dec = q * e_g
            k_dec = k * jnp.exp(g_last - g_c)
            state = s_scr[j]
            sb = state.astype(BF16)
            v_new = u - _dot(w.astype(BF16), sb)
            vnb = v_new.astype(BF16)
            out = _dot(q_dec.astype(BF16), sb) + _dot(attn.astype(BF16), vnb)
            s_scr[j] = state * jnp.exp(g_last) + _dot_tn(k_dec.astype(BF16), vnb)

            ms = jnp.mean(out * out, axis=1, keepdims=True)
            z = z_ref[rs, vs]
            gated = out * lax.rsqrt(ms + NORM_EPS) * nw_ref[...] * (z * _sigmoid(z))
            out_ref[rs, vs] = gated.astype(out_ref.dtype)


def _gdn(qkv, proj, gcol, grow, norm_w, *, batch, seq):
    tokens = qkv.shape[0]
    rows = REC_ROWS
    steps = seq // rows
    rep = GDN_V_HEADS // GDN_K_HEADS
    vw = rep * GDN_HEAD_DIM
    row = lambda b, t: b * steps + t
    return pl.pallas_call(
        _gdn_kernel,
        grid=(batch, GDN_K_HEADS, steps),
        in_specs=[
            pl.BlockSpec((rows, GDN_HEAD_DIM), lambda b, h, t: (row(b, t), h)),
            pl.BlockSpec((rows, GDN_HEAD_DIM), lambda b, h, t: (row(b, t), GDN_K_HEADS + h)),
            pl.BlockSpec((rows, vw), lambda b, h, t: (row(b, t), 2 * GDN_KEY_WIDTH // vw + h)),
            pl.BlockSpec((rows, vw), lambda b, h, t: (row(b, t), GDN_CONV_DIM // vw + h)),
            pl.BlockSpec((rows, LANES), lambda b, h, t: (row(b, t), 0)),
            pl.BlockSpec((rows // CHUNK, GDN_GATE_ROWS, CHUNK), lambda b, h, t: (row(b, t), 0, 0)),
            pl.BlockSpec((1, GDN_HEAD_DIM), lambda b, h, t: (0, 0)),
        ],
        out_specs=pl.BlockSpec((rows, vw), lambda b, h, t: (row(b, t), h)),
        out_shape=jax.ShapeDtypeStruct((tokens, GDN_VAL_WIDTH), BF16),
        scratch_shapes=[pltpu.VMEM((rep, GDN_HEAD_DIM, GDN_HEAD_DIM), F32)],
        compiler_params=_params("parallel", "parallel", "arbitrary"),
        name="gdn",
    )(qkv, qkv, qkv, proj, gcol, grow, norm_w)


def _outproj_kernel(a_ref, w_ref, x_ref, g_ref, o_ref):
    o_ref[...] = x_ref[...] + g_ref[0] * _dot(a_ref[...], w_ref[...])


def _outproj(a, w, x2, gate, *, seq):
    tokens, kdim = a.shape
    d = x2.shape[1]
    tm = OUT_TM
    per_batch = seq // tm
    return pl.pallas_call(
        _outproj_kernel,
        grid=(tokens // tm,),
        in_specs=[
            pl.BlockSpec((tm, kdim), lambda i: (i, 0)),
            pl.BlockSpec((kdim, d), lambda i: (0, 0)),
            pl.BlockSpec((tm, d), lambda i: (i, 0)),
            pl.BlockSpec((1, 1, d), lambda i: (i // per_batch, 0, 0)),
        ],
        out_specs=pl.BlockSpec((tm, d), lambda i: (i, 0)),
        out_shape=jax.ShapeDtypeStruct((tokens, d), F32),
        compiler_params=_params("parallel"),
        name="outproj",
    )(a, w, x2, gate)


def _ffn_kernel(x_ref, sc_ref, sh_ref, g_ref, wg_ref, wu_ref, wd_ref, fw_ref, o_ref, h_scr, acc_scr, *, final_norm):
    j = pl.program_id(1)

    @pl.when(j == 0)
    def _():
        h_scr[...] = _modulated_norm(x_ref[...], sc_ref[0], sh_ref[0]).astype(BF16)
        acc_scr[...] = jnp.zeros_like(acc_scr)

    hb = h_scr[...]
    gate = _dot(hb, wg_ref[...])
    up = _dot(hb, wu_ref[...])
    act = gate * _sigmoid(gate) * up
    acc_scr[...] += _dot(act.astype(BF16), wd_ref[...])

    @pl.when(j == pl.num_programs(1) - 1)
    def _():
        y = x_ref[...] + g_ref[0] * acc_scr[...]
        if final_norm:
            ms = jnp.mean(y * y, axis=-1, keepdims=True)
            y = y * lax.rsqrt(ms + NORM_EPS) * fw_ref[...]
        o_ref[...] = y


def _ffn(x2, scale, shift, gate, w_gate_up, w_down, final_w, *, seq, final_norm):
    tokens, d = x2.shape
    d_ff = w_down.shape[0]
    tm, th = FFN_TM, FFN_TH
    per_batch = seq // tm
    n_h = d_ff // th
    mod_spec = pl.BlockSpec((1, 1, d), lambda i, j: (i // per_batch, 0, 0))
    return pl.pallas_call(
        functools.partial(_ffn_kernel, final_norm=final_norm),
        grid=(tokens // tm, n_h),
        in_specs=[
            pl.BlockSpec((tm, d), lambda i, j: (i, 0)),
            mod_spec, mod_spec, mod_spec,
            pl.BlockSpec((d, th), lambda i, j: (0, j)),
            pl.BlockSpec((d, th), lambda i, j: (0, n_h + j)),
            pl.BlockSpec((th, d), lambda i, j: (j, 0)),
            pl.BlockSpec((1, d), lambda i, j: (0, 0)),
        ],
        out_specs=pl.BlockSpec((tm, d), lambda i, j: (i, 0)),
        out_shape=jax.ShapeDtypeStruct((tokens, d), F32),
        scratch_shapes=[pltpu.VMEM((tm, d), BF16), pltpu.VMEM((tm, d), F32)],
        compiler_params=_params("parallel", "arbitrary"),
        name="ffn",
    )(x2, scale, shift, gate, w_gate_up, w_gate_up, w_down, final_w)


def _gate_weights(w_in, start, n_rows):
    wg = w_in[:, start:]
    count = wg.shape[1]
    col = jnp.pad(wg, ((0, 0), (0, LANES - count))).astype(BF16)
    row = jnp.pad(wg.T, ((0, n_rows - count), (0, 0))).astype(BF16)
    return col, row


def _gate_vec(vec, offset, n_rows):
    count = vec.shape[0]
    col = jnp.zeros((1, LANES), F32).at[0, offset:offset + count].set(vec.astype(F32))
    row = jnp.zeros((n_rows, 1), F32).at[offset:offset + count, 0].set(vec.astype(F32))
    return col, row


def kernel(x, c, ada_w, ada_b, ml_w_in, ml_b_if, ml_norm_w, ml_w_out, gdn_w_in, gdn_conv_w, gdn_a_log,
           gdn_dt_bias, gdn_norm_w, gdn_w_out, ffn_w_gate_up, ffn_w_down, final_norm_w):
    batch, seq, d = x.shape
    depth = ada_w.shape[0]
    n_mixers = 2
    tokens = batch * seq
    x2 = x.reshape(tokens, d)

    c_rows = 8
    c_pad = jnp.pad(c, ((0, c_rows - batch), (0, 0)))
    mod = _adaln(c_pad, ada_w, ada_b)[:, :batch, :]

    zeros_col, zeros_row_ml = _gate_vec(jnp.zeros((1,), F32), 0, ML_GATE_ROWS)

    for layer in range(depth):
        sh1, sc1, g1, sh2, sc2, g2 = [m.reshape(batch, 1, d) for m in jnp.split(mod[layer], 6, axis=-1)]
        jdx = layer // n_mixers
        if layer % n_mixers == 0:
            w_in = ml_w_in[jdx]
            wg_col, wg_row = _gate_weights(w_in, ML_MAIN, ML_GATE_ROWS)
            b_col, b_row = _gate_vec(ml_b_if[jdx], 0, ML_GATE_ROWS)
            proj, gcol, grow = _inproj(x2, sc1, sh1, w_in[:, :ML_MAIN].astype(BF16), wg_col, wg_row,
                                       b_col, b_row, zeros_col, zeros_row_ml, mode="ml", seq=seq)
            mixed = _mlstm(proj, gcol, grow, ml_norm_w[jdx].reshape(1, ML_V_WIDTH).astype(F32),
                           batch=batch, seq=seq)
            w_out = ml_w_out[jdx]
        else:
            w_in = gdn_w_in[jdx]
            wg_col, wg_row = _gate_weights(w_in, GDN_MAIN, GDN_GATE_ROWS)
            b_col, b_row = _gate_vec(gdn_dt_bias[jdx], GDN_V_HEADS, GDN_GATE_ROWS)
            a_col, a_row = _gate_vec(gdn_a_log[jdx], GDN_V_HEADS, GDN_GATE_ROWS)
            proj, gcol, grow = _inproj(x2, sc1, sh1, w_in[:, :GDN_MAIN].astype(BF16), wg_col, wg_row,
                                       b_col, b_row, a_col, a_row, mode="gdn", seq=seq)
            qkv = _gdn_conv(proj, gdn_conv_w[jdx].astype(F32), batch=batch, seq=seq)
            mixed = _gdn(qkv, proj, gcol, grow, gdn_norm_w[jdx].reshape(1, GDN_HEAD_DIM).astype(F32),
                         batch=batch, seq=seq)
            w_out = gdn_w_out[jdx]
        x2 = _outproj(mixed, w_out.astype(BF16), x2, g1, seq=seq)
        x2 = _ffn(x2, sc2, sh2, g2, ffn_w_gate_up[layer].astype(BF16), ffn_w_down[layer].astype(BF16),
                  final_norm_w.reshape(1, d).astype(F32), seq=seq, final_norm=(layer == depth - 1))
    return x2.reshape(batch, seq, d)
```

```python
import functools

import jax
import jax.numpy as jnp
from jax import lax
from jax.experimental import pallas as pl
from jax.experimental.pallas import tpu as pltpu

F32 = jnp.float32
BF16 = jnp.bfloat16
HIGHEST = lax.Precision.HIGHEST

LANES = 128
VMEM_LIMIT = 56 * 1024 * 1024

D_MODEL = 1024
CHUNK = 64
NORM_EPS = 1e-6

ML_HEADS = 4
ML_QK_DIM = 128
ML_V_DIM = 256
ML_QK_WIDTH = ML_HEADS * ML_QK_DIM
ML_V_WIDTH = ML_HEADS * ML_V_DIM
ML_MAIN = 2 * ML_QK_WIDTH + 2 * ML_V_WIDTH
ML_GATE_ROWS = 16
ML_GATE_CAP = 15.0

GDN_K_HEADS = 8
GDN_V_HEADS = 16
GDN_HEAD_DIM = 128
GDN_KEY_WIDTH = GDN_K_HEADS * GDN_HEAD_DIM
GDN_VAL_WIDTH = GDN_V_HEADS * GDN_HEAD_DIM
GDN_CONV_DIM = 2 * GDN_KEY_WIDTH + GDN_VAL_WIDTH
GDN_CONV_WIDTH = 4
GDN_MAIN = GDN_CONV_DIM + GDN_VAL_WIDTH
GDN_GATE_ROWS = 2 * GDN_V_HEADS

INPROJ_TM = 512
INPROJ_TN = 1024
REC_ROWS = 256
GDN_STEP_K_HEADS = 2
CONV_ROWS = 512
CONV_COLS = 512
OUT_TM = 512
FFN_TM = 512
FFN_TH = 256


def _params(*sem):
    return pltpu.CompilerParams(dimension_semantics=sem, vmem_limit_bytes=VMEM_LIMIT)


def _sigmoid(x):
    return 1.0 / (1.0 + jnp.exp(-x))


def _softplus(x):
    return jnp.maximum(x, 0.0) + jnp.log1p(jnp.exp(-jnp.abs(x)))


def _dot(a, b):
    return jnp.dot(a, b, preferred_element_type=F32)


def _dot_nt(a, b):
    return lax.dot_general(a, b, (((1,), (1,)), ((), ())), preferred_element_type=F32)


def _dot_tn(a, b):
    return lax.dot_general(a, b, (((0,), (0,)), ((), ())), preferred_element_type=F32)


def _iota(shape, dim):
    return lax.broadcasted_iota(jnp.int32, shape, dim)


def _adaln_kernel(c_ref, w_ref, b_ref, o_ref):
    c = c_ref[...]
    act = c * _sigmoid(c)
    o_ref[0] = jnp.dot(act, w_ref[0], precision=HIGHEST, preferred_element_type=F32) + b_ref[0]


def _adaln(c_pad, ada_w, ada_b):
    depth, d, n = ada_w.shape
    rows = c_pad.shape[0]
    tn = D_MODEL
    return pl.pallas_call(
        _adaln_kernel,
        grid=(depth, n // tn),
        in_specs=[
            pl.BlockSpec((rows, d), lambda l, j: (0, 0)),
            pl.BlockSpec((1, d, tn), lambda l, j: (l, 0, j)),
            pl.BlockSpec((1, 1, tn), lambda l, j: (l, 0, j)),
        ],
        out_specs=pl.BlockSpec((1, rows, tn), lambda l, j: (l, 0, j)),
        out_shape=jax.ShapeDtypeStruct((depth, rows, n), F32),
        compiler_params=_params("parallel", "parallel"),
        name="adaln",
    )(c_pad, ada_w, ada_b.reshape(depth, 1, n))


def _modulated_norm(x, scale, shift):
    ms = jnp.mean(x * x, axis=-1, keepdims=True)
    return x * lax.rsqrt(ms + NORM_EPS) * (1.0 + scale) + shift


def _ml_gates(pre, idx):
    capped = ML_GATE_CAP * jnp.tanh(pre / ML_GATE_CAP)
    log_f = jnp.minimum(capped, 0.0) - jnp.log1p(jnp.exp(-jnp.abs(capped)))
    is_f = idx >= ML_HEADS
    return jnp.where(is_f, log_f, capped), is_f


def _gdn_gates(pre, a_log, idx):
    is_g = idx >= GDN_V_HEADS
    return jnp.where(is_g, -jnp.exp(a_log) * _softplus(pre), _sigmoid(pre)), is_g


def _inproj_kernel(x_ref, sc_ref, sh_ref, w_ref, wg_ref, wgt_ref, bcol_ref, brow_ref, acol_ref, arow_ref,
                   proj_ref, gcol_ref, grow_ref, h_scr, *, mode):
    tm = x_ref.shape[0]
    n_rows = wgt_ref.shape[0]

    @pl.when(pl.program_id(1) == 0)
    def _():
        hb = _modulated_norm(x_ref[...], sc_ref[0], sh_ref[0]).astype(BF16)
        h_scr[...] = hb

        pre = _dot(hb, wg_ref[...]) + bcol_ref[...]
        lane = _iota((tm, LANES), 1)
        if mode == "ml":
            vals, cum = _ml_gates(pre, lane)
        else:
            vals, cum = _gdn_gates(pre, acol_ref[...], lane)
        r = _iota((tm, tm), 0)
        c = _iota((tm, tm), 1)
        tri = jnp.where((r // CHUNK == c // CHUNK) & (c <= r), 1.0, 0.0).astype(F32)
        csum = jnp.dot(tri, vals, precision=HIGHEST, preferred_element_type=F32)
        gcol_ref[...] = jnp.where(cum, csum, vals)

        rr = _iota((CHUNK, CHUNK), 0)
        cc = _iota((CHUNK, CHUNK), 1)
        upper = jnp.where(rr <= cc, 1.0, 0.0).astype(F32)
        row = _iota((n_rows, CHUNK), 0)
        for ch in range(tm // CHUNK):
            pre_t = _dot_nt(wgt_ref[...], h_scr[ch * CHUNK:(ch + 1) * CHUNK, :]) + brow_ref[...]
            if mode == "ml":
                vals_t, cum_t = _ml_gates(pre_t, row)
            else:
                vals_t, cum_t = _gdn_gates(pre_t, arow_ref[...], row)
            csum_t = jnp.dot(vals_t, upper, precision=HIGHEST, preferred_element_type=F32)
            grow_ref[ch] = jnp.where(cum_t, csum_t, vals_t)

    proj_ref[...] = _dot(h_scr[...], w_ref[...])


def _inproj(x2, scale, shift, w_main, w_gate, w_gate_t, bias_col, bias_row, alog_col, alog_row, *, mode, seq):
    tokens, d = x2.shape
    n = w_main.shape[1]
    n_rows = w_gate_t.shape[0]
    tm, tn = INPROJ_TM, INPROJ_TN
    per_batch = seq // tm
    const = lambda i, j: (0, 0)
    return pl.pallas_call(
        functools.partial(_inproj_kernel, mode=mode),
        grid=(tokens // tm, n // tn),
        in_specs=[
            pl.BlockSpec((tm, d), lambda i, j: (i, 0)),
            pl.BlockSpec((1, 1, d), lambda i, j: (i // per_batch, 0, 0)),
            pl.BlockSpec((1, 1, d), lambda i, j: (i // per_batch, 0, 0)),
            pl.BlockSpec((d, tn), lambda i, j: (0, j)),
            pl.BlockSpec((d, LANES), const),
            pl.BlockSpec((n_rows, d), const),
            pl.BlockSpec((1, LANES), const),
            pl.BlockSpec((n_rows, 1), const),
            pl.BlockSpec((1, LANES), const),
            pl.BlockSpec((n_rows, 1), const),
        ],
        out_specs=[
            pl.BlockSpec((tm, tn), lambda i, j: (i, j)),
            pl.BlockSpec((tm, LANES), lambda i, j: (i, 0)),
            pl.BlockSpec((tm // CHUNK, n_rows, CHUNK), lambda i, j: (i, 0, 0)),
        ],
        out_shape=[
            jax.ShapeDtypeStruct((tokens, n), F32),
            jax.ShapeDtypeStruct((tokens, LANES), F32),
            jax.ShapeDtypeStruct((tokens // CHUNK, n_rows, CHUNK), F32),
        ],
        scratch_shapes=[pltpu.VMEM((tm, d), BF16)],
        compiler_params=_params("parallel", "arbitrary"),
        name="inproj_" + mode,
    )(x2, scale, shift, w_main, w_gate, w_gate_t, bias_col, bias_row, alog_col, alog_row)


def _mlstm_kernel(q_ref, k_ref, v_ref, o_ref, gcol_ref, grow_ref, nw_ref, out_ref,
                  c_scr, n_scr, m_scr, cprev_scr):
    rows = q_ref.shape[0]

    @pl.when(pl.program_id(1) == 0)
    def _():
        c_scr[...] = jnp.zeros_like(c_scr)
        n_scr[...] = jnp.zeros_like(n_scr)
        m_scr[...] = jnp.zeros_like(m_scr)

    causal = _iota((CHUNK, CHUNK), 0) >= _iota((CHUNK, CHUNK), 1)
    k_scale = ML_QK_DIM ** -0.5
    n_ch = rows // CHUNK
    items = [(ch, h) for ch in range(n_ch) for h in range(ML_HEADS)]

    def row_slice(ch):
        return slice(ch * CHUNK, (ch + 1) * CHUNK)

    def qk_slice(h):
        return slice(h * ML_QK_DIM, (h + 1) * ML_QK_DIM)

    def v_slice(h):
        return slice(h * ML_V_DIM, (h + 1) * ML_V_DIM)

    n_items = len(items)

    b_col, b_last, d_mat, g_end, m_intra, m_loc, k_w, d_c, d_n, scores = ({} for _ in range(10))
    for it, (ch, h) in enumerate(items):
        rs = row_slice(ch)
        gc = gcol_ref[rs, :]
        gr = grow_ref[ch]
        li_c = gc[:, h:h + 1]
        b_c = gc[:, ML_HEADS + h:ML_HEADS + h + 1]
        li_r = gr[h:h + 1, :]
        b_r = gr[ML_HEADS + h:ML_HEADS + h + 1, :]
        b_l = b_r[:, CHUNK - 1:CHUNK]
        b_col[it], b_last[it] = b_c, b_l
        d_mat[it] = jnp.where(causal, b_c - b_r + li_r, -jnp.inf)
        g_end[it] = b_l - b_c + li_c
    for it in range(n_items):
        m_intra[it] = jnp.max(d_mat[it], axis=1, keepdims=True)
        m_loc[it] = jnp.max(g_end[it], axis=0, keepdims=True)
    for it, (ch, h) in enumerate(items):
        k_w[it] = k_ref[row_slice(ch), qk_slice(h)] * (k_scale * jnp.exp(g_end[it] - m_loc[it]))
    for it, (ch, h) in enumerate(items):
        rs = row_slice(ch)
        d_c[it] = _dot_tn(k_w[it].astype(BF16), v_ref[rs, v_slice(h)].astype(BF16))
        d_n[it] = jnp.sum(k_w[it], axis=0, keepdims=True)
        scores[it] = _dot_nt(q_ref[rs, qk_slice(h)].astype(BF16),
                             k_ref[rs, qk_slice(h)].astype(BF16)) * k_scale

    m_prev, n_prev = {}, {}
    m_run = [m_scr[h][:, 0:1] for h in range(ML_HEADS)]
    n_run = [n_scr[h] for h in range(ML_HEADS)]
    for it, (ch, h) in enumerate(items):
        c_run = c_scr[h]
        cprev_scr[it] = c_run.astype(BF16)
        m_prev[it], n_prev[it] = m_run[h], n_run[h]
        m_new = jnp.maximum(b_last[it] + m_run[h], m_loc[it])
        a_old = jnp.exp(b_last[it] + m_run[h] - m_new)
        a_new = jnp.exp(m_loc[it] - m_new)
        c_scr[h] = a_old * c_run + a_new * d_c[it]
        n_run[h] = a_old * n_run[h] + a_new * d_n[it]
        m_run[h] = m_new
    for h in range(ML_HEADS):
        n_scr[h] = n_run[h]
        m_scr[h] = jnp.broadcast_to(m_run[h], (1, LANES))

    m_t, s_inter, p, num, den, hid, ms = ({} for _ in range(7))
    for it in range(n_items):
        m_inter = b_col[it] + m_prev[it]
        m_t[it] = jnp.maximum(m_inter, m_intra[it])
        s_inter[it] = jnp.exp(m_inter - m_t[it])
        p[it] = jnp.exp(d_mat[it] - m_t[it]) * scores[it]
    for it, (ch, h) in enumerate(items):
        rs = row_slice(ch)
        q = q_ref[rs, qk_slice(h)]
        num[it] = (s_inter[it] * _dot(q.astype(BF16), cprev_scr[it])
                   + _dot(p[it].astype(BF16), v_ref[rs, v_slice(h)].astype(BF16)))
        den[it] = (s_inter[it] * jnp.sum(q * n_prev[it], axis=1, keepdims=True)
                   + jnp.sum(p[it], axis=1, keepdims=True))
    for it in range(n_items):
        hid[it] = num[it] / jnp.maximum(jnp.abs(den[it]), jnp.exp(-m_t[it]))
        ms[it] = jnp.mean(hid[it] * hid[it], axis=1, keepdims=True)
    for it, (ch, h) in enumerate(items):
        rs, vs = row_slice(ch), v_slice(h)
        out = hid[it] * lax.rsqrt(ms[it] + NORM_EPS) * nw_ref[:, vs] * _sigmoid(o_ref[rs, vs])
        out_ref[rs, vs] = out.astype(out_ref.dtype)


def _mlstm(proj, gcol, grow, norm_w, *, batch, seq):
    tokens = proj.shape[0]
    rows = REC_ROWS
    steps = seq // rows
    row = lambda b, t: b * steps + t
    return pl.pallas_call(
        _mlstm_kernel,
        grid=(batch, steps),
        in_specs=[
            pl.BlockSpec((rows, ML_QK_WIDTH), lambda b, t: (row(b, t), 0)),
            pl.BlockSpec((rows, ML_QK_WIDTH), lambda b, t: (row(b, t), 1)),
            pl.BlockSpec((rows, ML_V_WIDTH), lambda b, t: (row(b, t), 1)),
            pl.BlockSpec((rows, ML_V_WIDTH), lambda b, t: (row(b, t), 2)),
            pl.BlockSpec((rows, LANES), lambda b, t: (row(b, t), 0)),
            pl.BlockSpec((rows // CHUNK, ML_GATE_ROWS, CHUNK), lambda b, t: (row(b, t), 0, 0)),
            pl.BlockSpec((1, ML_V_WIDTH), lambda b, t: (0, 0)),
        ],
        out_specs=pl.BlockSpec((rows, ML_V_WIDTH), lambda b, t: (row(b, t), 0)),
        out_shape=jax.ShapeDtypeStruct((tokens, ML_V_WIDTH), BF16),
        scratch_shapes=[
            pltpu.VMEM((ML_HEADS, ML_QK_DIM, ML_V_DIM), F32),
            pltpu.VMEM((ML_HEADS, 1, ML_QK_DIM), F32),
            pltpu.VMEM((ML_HEADS, 1, LANES), F32),
            pltpu.VMEM((ML_HEADS * rows // CHUNK, ML_QK_DIM, ML_V_DIM), BF16),
        ],
        compiler_params=_params("parallel", "arbitrary"),
        name="mlstm",
    )(proj, proj, proj, proj, gcol, grow, norm_w)


def _gdn_conv_kernel(x_ref, w_ref, o_ref, buf):
    rows = x_ref.shape[0]
    halo = 8
    t = pl.program_id(2)

    @pl.when(t == 0)
    def _():
        buf[0:halo, :] = jnp.zeros((halo, buf.shape[1]), F32)

    @pl.when(t > 0)
    def _():
        buf[0:halo, :] = buf[rows:rows + halo, :]

    x = x_ref[...]
    buf[halo:halo + rows, :] = x
    w = w_ref[...]
    y = w[GDN_CONV_WIDTH - 1:GDN_CONV_WIDTH, :] * x
    for j in range(GDN_CONV_WIDTH - 1):
        shift = GDN_CONV_WIDTH - 1 - j
        y = y + w[j:j + 1, :] * buf[halo - shift:halo - shift + rows, :]
    y = y * _sigmoid(y)

    col_block = pl.program_id(1)
    qk_blocks = 2 * GDN_KEY_WIDTH // x_ref.shape[1]
    q_blocks = GDN_KEY_WIDTH // x_ref.shape[1]

    @pl.when(col_block >= qk_blocks)
    def _():
        o_ref[...] = y

    @pl.when(col_block < qk_blocks)
    def _():
        q_scale = jnp.where(col_block < q_blocks, GDN_HEAD_DIM ** -0.5, 1.0).astype(F32)
        for hh in range(x_ref.shape[1] // GDN_HEAD_DIM):
            hs = slice(hh * GDN_HEAD_DIM, (hh + 1) * GDN_HEAD_DIM)
            blk = y[:, hs]
            ss = jnp.sum(blk * blk, axis=1, keepdims=True)
            o_ref[:, hs] = blk * lax.rsqrt(ss + NORM_EPS) * q_scale


def _gdn_conv(proj, conv_w, *, batch, seq):
    tokens = proj.shape[0]
    rows, cols = CONV_ROWS, CONV_COLS
    steps = seq // rows
    return pl.pallas_call(
        _gdn_conv_kernel,
        grid=(batch, GDN_CONV_DIM // cols, steps),
        in_specs=[
            pl.BlockSpec((rows, cols), lambda b, c, t: (b * steps + t, c)),
            pl.BlockSpec((GDN_CONV_WIDTH, cols), lambda b, c, t: (0, c)),
        ],
        out_specs=pl.BlockSpec((rows, cols), lambda b, c, t: (b * steps + t, c)),
        out_shape=jax.ShapeDtypeStruct((tokens, GDN_CONV_DIM), F32),
        scratch_shapes=[pltpu.VMEM((rows + 8, cols), F32)],
        compiler_params=_params("parallel", "parallel", "arbitrary"),
        name="gdn_conv",
    )(proj, conv_w)


def _gdn_kernel(q_ref, k_ref, v_ref, z_ref, gcol_ref, grow_ref, nw_ref, out_ref,
                s_scr, wq_scr, attn_scr, kdt_scr, u_scr, eg_scr):
    rows = q_ref.shape[0]
    rep = GDN_V_HEADS // GDN_K_HEADS
    dh = GDN_HEAD_DIM
    n_kh = q_ref.shape[1] // dh
    n_vh = n_kh * rep
    n_ch = rows // CHUNK
    group = pl.program_id(1)

    @pl.when(pl.program_id(2) == 0)
    def _():
        s_scr[...] = jnp.zeros_like(s_scr)

    ri = _iota((CHUNK, CHUNK), 0)
    ci = _iota((CHUNK, CHUNK), 1)
    causal = ri >= ci
    strict = ri > ci
    eye = jnp.where(ri == ci, 1.0, 0.0).astype(F32)
    lane = _iota((CHUNK, LANES), 1)

    kk, qk, qs, ks = {}, {}, {}, {}
    for ch in range(n_ch):
        rs = slice(ch * CHUNK, (ch + 1) * CHUNK)
        for kh in range(n_kh):
            hs = slice(kh * dh, (kh + 1) * dh)
            q = q_ref[rs, hs]
            k = k_ref[rs, hs]
            kb = k.astype(BF16)
            qs[ch, kh], ks[ch, kh] = q, k
            kk[ch, kh] = _dot_nt(kb, kb)
            qk[ch, kh] = _dot_nt(q.astype(BF16), kb)

    items = [(ch, vh) for ch in range(n_ch) for vh in range(n_vh)]
    x, p, rhs = {}, {}, {}
    for it, (ch, vh) in enumerate(items):
        rs = slice(ch * CHUNK, (ch + 1) * CHUNK)
        kh = vh // rep
        head = group * n_vh + vh
        gc = gcol_ref[rs, :]
        beta_c = jnp.sum(jnp.where(lane == head, gc, 0.0), axis=1, keepdims=True)
        g_c = jnp.sum(jnp.where(lane == GDN_V_HEADS + head, gc, 0.0), axis=1, keepdims=True)
        g_r = grow_ref[ch, pl.ds(GDN_V_HEADS + head, 1), :]
        g_last = g_r[:, CHUNK - 1:CHUNK]
        decay = jnp.exp(jnp.where(causal, g_c - g_r, -jnp.inf))
        a_mat = jnp.where(strict, kk[ch, kh] * beta_c * decay, 0.0)
        x[it] = eye - a_mat
        ab = a_mat.astype(BF16)
        p[it] = _dot(ab, ab)

        q, k = qs[ch, kh], ks[ch, kh]
        v = v_ref[rs, vh * dh:(vh + 1) * dh]
        e_g = jnp.exp(g_c)
        rhs[it] = jnp.concatenate([v * beta_c, k * (beta_c * e_g)], axis=1).astype(BF16)
        attn_scr[it] = (qk[ch, kh] * decay).astype(BF16)
        wq_scr[it, CHUNK:2 * CHUNK, :] = (q * e_g).astype(BF16)
        kdt_scr[it] = (k * jnp.exp(g_last - g_c)).T.astype(BF16)
        eg_scr[it] = jnp.broadcast_to(jnp.exp(g_last), (1, LANES))

    n_levels = CHUNK.bit_length() - 2
    for level in range(n_levels):
        for it in range(len(items)):
            pb = p[it].astype(BF16)
            x[it] = x[it] + _dot(x[it].astype(BF16), pb)
            if level + 1 < n_levels:
                p[it] = _dot(pb, pb)

    for it in range(len(items)):
        sol = _dot(x[it].astype(BF16), rhs[it])
        u_scr[it] = sol[:, :dh]
        wq_scr[it, 0:CHUNK, :] = sol[:, dh:].astype(BF16)

    state = [s_scr[vh] for vh in range(n_vh)]
    for ch in range(n_ch):
        rs = slice(ch * CHUNK, (ch + 1) * CHUNK)
        base = ch * n_vh
        sb = [state[vh].astype(BF16) for vh in range(n_vh)]
        r = [_dot(wq_scr[base + vh], sb[vh]) for vh in range(n_vh)]
        vnb = [(u_scr[base + vh] - r[vh][:CHUNK]).astype(BF16) for vh in range(n_vh)]
        state = [state[vh] * eg_scr[base + vh][:, 0:1] + _dot(kdt_scr[base + vh], vnb[vh]) for vh in range(n_vh)]
        for vh in range(n_vh):
            out = r[vh][CHUNK:] + _dot(attn_scr[base + vh], vnb[vh])
            ms = jnp.mean(out * out, axis=1, keepdims=True)
            z = z_ref[rs, vh * dh:(vh + 1) * dh]
            gated = out * lax.rsqrt(ms + NORM_EPS) * nw_ref[...] * (z * _sigmoid(z))
            out_ref[rs, vh * dh:(vh + 1) * dh] = gated.astype(out_ref.dtype)
    for vh in range(n_vh):
        s_scr[vh] = state[vh]


def _gdn(qkv, proj, gcol, grow, norm_w, *, batch, seq):
    tokens = qkv.shape[0]
    rows = REC_ROWS
    steps = seq // rows
    rep = GDN_V_HEADS // GDN_K_HEADS
    n_kh = GDN_STEP_K_HEADS
    n_vh = n_kh * rep
    kw = n_kh * GDN_HEAD_DIM
    vw = n_vh * GDN_HEAD_DIM
    n_items = n_vh * rows // CHUNK
    row = lambda b, t: b * steps + t
    return pl.pallas_call(
        _gdn_kernel,
        grid=(batch, GDN_K_HEADS // n_kh, steps),
        in_specs=[
            pl.BlockSpec((rows, kw), lambda b, g, t: (row(b, t), g)),
            pl.BlockSpec((rows, kw), lambda b, g, t: (row(b, t), GDN_KEY_WIDTH // kw + g)),
            pl.BlockSpec((rows, vw), lambda b, g, t: (row(b, t), 2 * GDN_KEY_WIDTH // vw + g)),
            pl.BlockSpec((rows, vw), lambda b, g, t: (row(b, t), GDN_CONV_DIM // vw + g)),
            pl.BlockSpec((rows, LANES), lambda b, g, t: (row(b, t), 0)),
            pl.BlockSpec((rows // CHUNK, GDN_GATE_ROWS, CHUNK), lambda b, g, t: (row(b, t), 0, 0)),
            pl.BlockSpec((1, GDN_HEAD_DIM), lambda b, g, t: (0, 0)),
        ],
        out_specs=pl.BlockSpec((rows, vw), lambda b, g, t: (row(b, t), g)),
        out_shape=jax.ShapeDtypeStruct((tokens, GDN_VAL_WIDTH), BF16),
        scratch_shapes=[
            pltpu.VMEM((n_vh, GDN_HEAD_DIM, GDN_HEAD_DIM), F32),
            pltpu.VMEM((n_items, 2 * CHUNK, GDN_HEAD_DIM), BF16),
            pltpu.VMEM((n_items, CHUNK, CHUNK), BF16),
            pltpu.VMEM((n_items, GDN_HEAD_DIM, CHUNK), BF16),
            pltpu.VMEM((n_items, CHUNK, GDN_HEAD_DIM), F32),
            pltpu.VMEM((n_items, 1, LANES), F32),
        ],
        compiler_params=_params("parallel", "parallel", "arbitrary"),
        name="gdn",
    )(qkv, qkv, qkv, proj, gcol, grow, norm_w)


def _outproj_kernel(a_ref, w_ref, x_ref, g_ref, o_ref):
    o_ref[...] = x_ref[...] + g_ref[0] * _dot(a_ref[...], w_ref[...])


def _outproj(a, w, x2, gate, *, seq):
    tokens, kdim = a.shape
    d = x2.shape[1]
    tm = OUT_TM
    per_batch = seq // tm
    return pl.pallas_call(
        _outproj_kernel,
        grid=(tokens // tm,),
        in_specs=[
            pl.BlockSpec((tm, kdim), lambda i: (i, 0)),
            pl.BlockSpec((kdim, d), lambda i: (0, 0)),
            pl.BlockSpec((tm, d), lambda i: (i, 0)),
            pl.BlockSpec((1, 1, d), lambda i: (i // per_batch, 0, 0)),
        ],
        out_specs=pl.BlockSpec((tm, d), lambda i: (i, 0)),
        out_shape=jax.ShapeDtypeStruct((tokens, d), F32),
        compiler_params=_params("parallel"),
        name="outproj",
    )(a, w, x2, gate)


def _ffn_kernel(x_ref, sc_ref, sh_ref, g_ref, wg_ref, wu_ref, wd_ref, fw_ref, o_ref, h_scr, acc_scr, *, final_norm):
    j = pl.program_id(1)

    @pl.when(j == 0)
    def _():
        h_scr[...] = _modulated_norm(x_ref[...], sc_ref[0], sh_ref[0]).astype(BF16)
        acc_scr[...] = jnp.zeros_like(acc_scr)

    hb = h_scr[...]
    gate = _dot(hb, wg_ref[...])
    up = _dot(hb, wu_ref[...])
    act = gate * _sigmoid(gate) * up
    acc_scr[...] += _dot(act.astype(BF16), wd_ref[...])

    @pl.when(j == pl.num_programs(1) - 1)
    def _():
        y = x_ref[...] + g_ref[0] * acc_scr[...]
        if final_norm:
            ms = jnp.mean(y * y, axis=-1, keepdims=True)
            y = y * lax.rsqrt(ms + NORM_EPS) * fw_ref[...]
        o_ref[...] = y


def _ffn(x2, scale, shift, gate, w_gate_up, w_down, final_w, *, seq, final_norm):
    tokens, d = x2.shape
    d_ff = w_down.shape[0]
    tm, th = FFN_TM, FFN_TH
    per_batch = seq // tm
    n_h = d_ff // th
    mod_spec = pl.BlockSpec((1, 1, d), lambda i, j: (i // per_batch, 0, 0))
    return pl.pallas_call(
        functools.partial(_ffn_kernel, final_norm=final_norm),
        grid=(tokens // tm, n_h),
        in_specs=[
            pl.BlockSpec((tm, d), lambda i, j: (i, 0)),
            mod_spec, mod_spec, mod_spec,
            pl.BlockSpec((d, th), lambda i, j: (0, j)),
            pl.BlockSpec((d, th), lambda i, j: (0, n_h + j)),
            pl.BlockSpec((th, d), lambda i, j: (j, 0)),
            pl.BlockSpec((1, d), lambda i, j: (0, 0)),
        ],
        out_specs=pl.BlockSpec((tm, d), lambda i, j: (i, 0)),
        out_shape=jax.ShapeDtypeStruct((tokens, d), F32),
        scratch_shapes=[pltpu.VMEM((tm, d), BF16), pltpu.VMEM((tm, d), F32)],
        compiler_params=_params("parallel", "arbitrary"),
        name="ffn",
    )(x2, scale, shift, gate, w_gate_up, w_gate_up, w_down, final_w)


def _gate_weights(w_in, start, n_rows):
    wg = w_in[:, start:]
    count = wg.shape[1]
    col = jnp.pad(wg, ((0, 0), (0, LANES - count))).astype(BF16)
    row = jnp.pad(wg.T, ((0, n_rows - count), (0, 0))).astype(BF16)
    return col, row


def _gate_vec(vec, offset, n_rows):
    count = vec.shape[0]
    col = jnp.zeros((1, LANES), F32).at[0, offset:offset + count].set(vec.astype(F32))
    row = jnp.zeros((n_rows, 1), F32).at[offset:offset + count, 0].set(vec.astype(F32))
    return col, row


def kernel(x, c, ada_w, ada_b, ml_w_in, ml_b_if, ml_norm_w, ml_w_out, gdn_w_in, gdn_conv_w, gdn_a_log,
           gdn_dt_bias, gdn_norm_w, gdn_w_out, ffn_w_gate_up, ffn_w_down, final_norm_w):
    batch, seq, d = x.shape
    depth = ada_w.shape[0]
    n_mixers = 2
    tokens = batch * seq
    x2 = x.reshape(tokens, d)

    c_rows = 8
    c_pad = jnp.pad(c, ((0, c_rows - batch), (0, 0)))
    mod = _adaln(c_pad, ada_w, ada_b)[:, :batch, :]

    zeros_col, zeros_row_ml = _gate_vec(jnp.zeros((1,), F32), 0, ML_GATE_ROWS)

    for layer in range(depth):
        sh1, sc1, g1, sh2, sc2, g2 = [m.reshape(batch, 1, d) for m in jnp.split(mod[layer], 6, axis=-1)]
        jdx = layer // n_mixers
        if layer % n_mixers == 0:
            w_in = ml_w_in[jdx]
            wg_col, wg_row = _gate_weights(w_in, ML_MAIN, ML_GATE_ROWS)
            b_col, b_row = _gate_vec(ml_b_if[jdx], 0, ML_GATE_ROWS)
            proj, gcol, grow = _inproj(x2, sc1, sh1, w_in[:, :ML_MAIN].astype(BF16), wg_col, wg_row,
                                       b_col, b_row, zeros_col, zeros_row_ml, mode="ml", seq=seq)
            mixed = _mlstm(proj, gcol, grow, ml_norm_w[jdx].reshape(1, ML_V_WIDTH).astype(F32),
                           batch=batch, seq=seq)
            w_out = ml_w_out[jdx]
        else:
            w_in = gdn_w_in[jdx]
            wg_col, wg_row = _gate_weights(w_in, GDN_MAIN, GDN_GATE_ROWS)
            b_col, b_row = _gate_vec(gdn_dt_bias[jdx], GDN_V_HEADS, GDN_GATE_ROWS)
            a_col, a_row = _gate_vec(gdn_a_log[jdx], GDN_V_HEADS, GDN_GATE_ROWS)
            proj, gcol, grow = _inproj(x2, sc1, sh1, w_in[:, :GDN_MAIN].astype(BF16), wg_col, wg_row,
                                       b_col, b_row, a_col, a_row, mode="gdn", seq=seq)
            qkv = _gdn_conv(proj, gdn_conv_w[jdx].astype(F32), batch=batch, seq=seq)
            mixed = _gdn(qkv, proj, gcol, grow, gdn_norm_w[jdx].reshape(1, GDN_HEAD_DIM).astype(F32),
                         batch=batch, seq=seq)
            w_out = gdn_w_out[jdx]
        x2 = _outproj(mixed, w_out.astype(BF16), x2, g1, seq=seq)
        x2 = _ffn(x2, sc2, sh2, g2, ffn_w_gate_up[layer].astype(BF16), ffn_w_down[layer].astype(BF16),
                  final_norm_w.reshape(1, d).astype(F32), seq=seq, final_norm=(layer == depth - 1))
    return x2.reshape(batch, seq, d)
```

```python
import functools

import jax
import jax.numpy as jnp
from jax import lax
from jax.experimental import pallas as pl
from jax.experimental.pallas import tpu as pltpu

F32 = jnp.float32
BF16 = jnp.bfloat16
HIGHEST = lax.Precision.HIGHEST

LANES = 128
VMEM_LIMIT = 56 * 1024 * 1024

D_MODEL = 1024
CHUNK = 64
NORM_EPS = 1e-6

ML_HEADS = 4
ML_QK_DIM = 128
ML_V_DIM = 256
ML_QK_WIDTH = ML_HEADS * ML_QK_DIM
ML_V_WIDTH = ML_HEADS * ML_V_DIM
ML_MAIN = 2 * ML_QK_WIDTH + 2 * ML_V_WIDTH
ML_GATE_ROWS = 16
ML_GATE_CAP = 15.0

GDN_K_HEADS = 8
GDN_V_HEADS = 16
GDN_HEAD_DIM = 128
GDN_KEY_WIDTH = GDN_K_HEADS * GDN_HEAD_DIM
GDN_VAL_WIDTH = GDN_V_HEADS * GDN_HEAD_DIM
GDN_CONV_DIM = 2 * GDN_KEY_WIDTH + GDN_VAL_WIDTH
GDN_CONV_WIDTH = 4
GDN_MAIN = GDN_CONV_DIM + GDN_VAL_WIDTH
GDN_GATE_ROWS = 2 * GDN_V_HEADS

INPROJ_TM = 512
INPROJ_TN = 512
CONV_HALO = 8
REC_ROWS = 256
GDN_STEP_K_HEADS = 2
FFN_TM = 512
FFN_TH = 256


def _params(*sem):
    return pltpu.CompilerParams(dimension_semantics=sem, vmem_limit_bytes=VMEM_LIMIT)


def _sigmoid(x):
    return 1.0 / (1.0 + jnp.exp(-x))


def _softplus(x):
    return jnp.maximum(x, 0.0) + jnp.log1p(jnp.exp(-jnp.abs(x)))


def _dot(a, b):
    return jnp.dot(a, b, preferred_element_type=F32)


def _dot_nt(a, b):
    return lax.dot_general(a, b, (((1,), (1,)), ((), ())), preferred_element_type=F32)


def _dot_tn(a, b):
    return lax.dot_general(a, b, (((0,), (0,)), ((), ())), preferred_element_type=F32)


def _iota(shape, dim):
    return lax.broadcasted_iota(jnp.int32, shape, dim)


def _adaln_kernel(c_ref, w_ref, b_ref, o_ref):
    c = c_ref[...]
    act = c * _sigmoid(c)
    o_ref[0] = jnp.dot(act, w_ref[0], precision=HIGHEST, preferred_element_type=F32) + b_ref[0]


def _adaln(c_pad, ada_w, ada_b):
    depth, d, n = ada_w.shape
    rows = c_pad.shape[0]
    tn = D_MODEL
    return pl.pallas_call(
        _adaln_kernel,
        grid=(depth, n // tn),
        in_specs=[
            pl.BlockSpec((rows, d), lambda l, j: (0, 0)),
            pl.BlockSpec((1, d, tn), lambda l, j: (l, 0, j)),
            pl.BlockSpec((1, 1, tn), lambda l, j: (l, 0, j)),
        ],
        out_specs=pl.BlockSpec((1, rows, tn), lambda l, j: (l, 0, j)),
        out_shape=jax.ShapeDtypeStruct((depth, rows, n), F32),
        compiler_params=_params("parallel", "parallel"),
        name="adaln",
    )(c_pad, ada_w, ada_b.reshape(depth, 1, n))


def _modulated_norm(x, scale, shift):
    ms = jnp.mean(x * x, axis=-1, keepdims=True)
    return x * lax.rsqrt(ms + NORM_EPS) * (1.0 + scale) + shift


def _ml_gates(pre, idx):
    capped = ML_GATE_CAP * jnp.tanh(pre / ML_GATE_CAP)
    log_f = jnp.minimum(capped, 0.0) - jnp.log1p(jnp.exp(-jnp.abs(capped)))
    is_f = idx >= ML_HEADS
    return jnp.where(is_f, log_f, capped), is_f


def _gdn_gates(pre, a_log, idx):
    is_g = idx >= GDN_V_HEADS
    return jnp.where(is_g, -jnp.exp(a_log) * _softplus(pre), _sigmoid(pre)), is_g


def _conv_silu(y, conv_w, first_of_seq, halo_ref, buf):
    tm = y.shape[0]
    buf[0:CONV_HALO, :] = jnp.where(first_of_seq, 0.0, halo_ref[...])
    buf[CONV_HALO:CONV_HALO + tm, :] = y
    halo_ref[...] = y[tm - CONV_HALO:tm, :]
    acc = conv_w[GDN_CONV_WIDTH - 1:GDN_CONV_WIDTH, :] * y
    for j in range(GDN_CONV_WIDTH - 1):
        start = CONV_HALO - (GDN_CONV_WIDTH - 1 - j)
        acc = acc + conv_w[j:j + 1, :] * buf[start:start + tm, :]
    return acc * _sigmoid(acc)


def _inproj_kernel(*refs, mode, per_batch):
    if mode == "ml":
        (x_ref, sc_ref, sh_ref, w_ref, wg_ref, wgt_ref, bcol_ref, brow_ref,
         proj_ref, gcol_ref, grow_ref, h_scr) = refs
    else:
        (x_ref, sc_ref, sh_ref, w_ref, wg_ref, wgt_ref, bcol_ref, brow_ref, acol_ref, arow_ref, cw_ref,
         proj_ref, gcol_ref, grow_ref, h_scr, halo_scr, conv_buf) = refs
    tm = x_ref.shape[0]
    n = w_ref.shape[1]
    n_rows = wgt_ref.shape[0]
    tn = INPROJ_TN

    hb = _modulated_norm(x_ref[...], sc_ref[0], sh_ref[0]).astype(BF16)
    h_scr[...] = hb

    pre = _dot(hb, wg_ref[...]) + bcol_ref[...]
    lane = _iota((CHUNK, LANES), 1)
    rr = _iota((CHUNK, CHUNK), 0)
    cc = _iota((CHUNK, CHUNK), 1)
    lower = jnp.where(cc <= rr, 1.0, 0.0).astype(F32)
    upper = jnp.where(rr <= cc, 1.0, 0.0).astype(F32)
    for ch in range(tm // CHUNK):
        rs = slice(ch * CHUNK, (ch + 1) * CHUNK)
        if mode == "ml":
            vals, cum = _ml_gates(pre[rs, :], lane)
        else:
            vals, cum = _gdn_gates(pre[rs, :], acol_ref[...], lane)
        csum = jnp.dot(lower, vals, precision=HIGHEST, preferred_element_type=F32)
        gcol_ref[rs, :] = jnp.where(cum, csum, vals)

    row = _iota((n_rows, CHUNK), 0)
    for ch in range(tm // CHUNK):
        pre_t = _dot_nt(wgt_ref[...], h_scr[ch * CHUNK:(ch + 1) * CHUNK, :]) + brow_ref[...]
        if mode == "ml":
            vals_t, cum_t = _ml_gates(pre_t, row)
        else:
            vals_t, cum_t = _gdn_gates(pre_t, arow_ref[...], row)
        csum_t = jnp.dot(vals_t, upper, precision=HIGHEST, preferred_element_type=F32)
        grow_ref[ch] = jnp.where(cum_t, csum_t, vals_t)

    if mode == "gdn":
        first_of_seq = pl.program_id(0) % per_batch == 0
    for j in range(n // tn):
        cs = slice(j * tn, (j + 1) * tn)
        y = _dot(h_scr[...], w_ref[:, cs])
        if mode == "gdn" and j * tn < GDN_CONV_DIM:
            y = _conv_silu(y, cw_ref[:, cs], first_of_seq, halo_scr.at[:, cs], conv_buf)
            if j * tn < 2 * GDN_KEY_WIDTH:
                q_scale = GDN_HEAD_DIM ** -0.5 if j * tn < GDN_KEY_WIDTH else 1.0
                for hh in range(tn // GDN_HEAD_DIM):
                    hs = slice(hh * GDN_HEAD_DIM, (hh + 1) * GDN_HEAD_DIM)
                    blk = y[:, hs]
                    ss = jnp.sum(blk * blk, axis=1, keepdims=True)
                    proj_ref[:, j * tn + hh * GDN_HEAD_DIM:j * tn + (hh + 1) * GDN_HEAD_DIM] = (
                        blk * (lax.rsqrt(ss + NORM_EPS) * q_scale)).astype(proj_ref.dtype)
                continue
        proj_ref[:, cs] = y.astype(proj_ref.dtype)


def _inproj(x2, scale, shift, w_main, w_gate, w_gate_t, bias_col, bias_row, gdn_extra=None, *, mode, seq):
    tokens, d = x2.shape
    n = w_main.shape[1]
    n_rows = w_gate_t.shape[0]
    tm = INPROJ_TM
    per_batch = seq // tm
    const = lambda i: (0, 0)
    resident = functools.partial(pl.BlockSpec, index_map=const, pipeline_mode=pl.Buffered(1))
    in_specs = [
        pl.BlockSpec((tm, d), lambda i: (i, 0)),
        pl.BlockSpec((1, 1, d), lambda i: (i // per_batch, 0, 0)),
        pl.BlockSpec((1, 1, d), lambda i: (i // per_batch, 0, 0)),
        resident((d, n)),
        resident((d, LANES)),
        resident((n_rows, d)),
        resident((1, LANES)),
        resident((n_rows, 1)),
    ]
    operands = [x2, scale, shift, w_main, w_gate, w_gate_t, bias_col, bias_row]
    scratch = [pltpu.VMEM((tm, d), BF16)]
    if mode == "gdn":
        alog_col, alog_row, conv_w = gdn_extra
        in_specs += [resident((1, LANES)), resident((n_rows, 1)), resident((GDN_CONV_WIDTH, GDN_CONV_DIM))]
        operands += [alog_col, alog_row, conv_w]
        scratch += [pltpu.VMEM((CONV_HALO, GDN_CONV_DIM), F32),
                    pltpu.VMEM((tm + CONV_HALO, INPROJ_TN), F32)]
    return pl.pallas_call(
        functools.partial(_inproj_kernel, mode=mode, per_batch=per_batch),
        grid=(tokens // tm,),
        in_specs=in_specs,
        out_specs=[
            pl.BlockSpec((tm, n), lambda i: (i, 0)),
            pl.BlockSpec((tm, LANES), lambda i: (i, 0)),
            pl.BlockSpec((tm // CHUNK, n_rows, CHUNK), lambda i: (i, 0, 0)),
        ],
        out_shape=[
            jax.ShapeDtypeStruct((tokens, n), BF16),
            jax.ShapeDtypeStruct((tokens, LANES), F32),
            jax.ShapeDtypeStruct((tokens // CHUNK, n_rows, CHUNK), F32),
        ],
        scratch_shapes=scratch,
        compiler_params=_params("arbitrary"),
        name="inproj_" + mode,
    )(*operands)


def _mlstm_kernel(q_ref, k_ref, v_ref, o_ref, gcol_ref, grow_ref, nw_ref, out_ref,
                  c_scr, n_scr, m_scr, cprev_scr):
    rows = q_ref.shape[0]

    @pl.when(pl.program_id(1) == 0)
    def _():
        c_scr[...] = jnp.zeros_like(c_scr)
        n_scr[...] = jnp.zeros_like(n_scr)
        m_scr[...] = jnp.zeros_like(m_scr)

    causal = _iota((CHUNK, CHUNK), 0) >= _iota((CHUNK, CHUNK), 1)
    k_scale = ML_QK_DIM ** -0.5
    n_ch = rows // CHUNK
    items = [(ch, h) for ch in range(n_ch) for h in range(ML_HEADS)]

    def row_slice(ch):
        return slice(ch * CHUNK, (ch + 1) * CHUNK)

    def qk_slice(h):
        return slice(h * ML_QK_DIM, (h + 1) * ML_QK_DIM)

    def v_slice(h):
        return slice(h * ML_V_DIM, (h + 1) * ML_V_DIM)

    n_items = len(items)

    b_col, b_last, d_mat, g_end, m_intra, m_loc, k_w, d_c, d_n, scores = ({} for _ in range(10))
    for it, (ch, h) in enumerate(items):
        rs = row_slice(ch)
        gc = gcol_ref[rs, :]
        gr = grow_ref[ch]
        li_c = gc[:, h:h + 1]
        b_c = gc[:, ML_HEADS + h:ML_HEADS + h + 1]
        li_r = gr[h:h + 1, :]
        b_r = gr[ML_HEADS + h:ML_HEADS + h + 1, :]
        b_l = b_r[:, CHUNK - 1:CHUNK]
        b_col[it], b_last[it] = b_c, b_l
        d_mat[it] = jnp.where(causal, b_c - b_r + li_r, -jnp.inf)
        g_end[it] = b_l - b_c + li_c
    for it in range(n_items):
        m_intra[it] = jnp.max(d_mat[it], axis=1, keepdims=True)
        m_loc[it] = jnp.max(g_end[it], axis=0, keepdims=True)
    for it, (ch, h) in enumerate(items):
        k_w[it] = (k_ref[row_slice(ch), qk_slice(h)].astype(F32)
                   * (k_scale * jnp.exp(g_end[it] - m_loc[it])))
    for it, (ch, h) in enumerate(items):
        rs = row_slice(ch)
        d_c[it] = _dot_tn(k_w[it].astype(BF16), v_ref[rs, v_slice(h)])
        d_n[it] = jnp.sum(k_w[it], axis=0, keepdims=True)
        scores[it] = _dot_nt(q_ref[rs, qk_slice(h)], k_ref[rs, qk_slice(h)]) * k_scale

    m_prev, n_prev = {}, {}
    m_run = [m_scr[h][:, 0:1] for h in range(ML_HEADS)]
    n_run = [n_scr[h] for h in range(ML_HEADS)]
    for it, (ch, h) in enumerate(items):
        c_run = c_scr[h]
        cprev_scr[it] = c_run.astype(BF16)
        m_prev[it], n_prev[it] = m_run[h], n_run[h]
        m_new = jnp.maximum(b_last[it] + m_run[h], m_loc[it])
        a_old = jnp.exp(b_last[it] + m_run[h] - m_new)
        a_new = jnp.exp(m_loc[it] - m_new)
        c_scr[h] = a_old * c_run + a_new * d_c[it]
        n_run[h] = a_old * n_run[h] + a_new * d_n[it]
        m_run[h] = m_new
    for h in range(ML_HEADS):
        n_scr[h] = n_run[h]
        m_scr[h] = jnp.broadcast_to(m_run[h], (1, LANES))

    m_t, s_inter, p, num, den, hid, ms = ({} for _ in range(7))
    for it in range(n_items):
        m_inter = b_col[it] + m_prev[it]
        m_t[it] = jnp.maximum(m_inter, m_intra[it])
        s_inter[it] = jnp.exp(m_inter - m_t[it])
        p[it] = jnp.exp(d_mat[it] - m_t[it]) * scores[it]
    for it, (ch, h) in enumerate(items):
        rs = row_slice(ch)
        q = q_ref[rs, qk_slice(h)]
        num[it] = (s_inter[it] * _dot(q, cprev_scr[it])
                   + _dot(p[it].astype(BF16), v_ref[rs, v_slice(h)]))
        den[it] = (s_inter[it] * jnp.sum(q.astype(F32) * n_prev[it], axis=1, keepdims=True)
                   + jnp.sum(p[it], axis=1, keepdims=True))
    for it in range(n_items):
        hid[it] = num[it] / jnp.maximum(jnp.abs(den[it]), jnp.exp(-m_t[it]))
        ms[it] = jnp.mean(hid[it] * hid[it], axis=1, keepdims=True)
    for it, (ch, h) in enumerate(items):
        rs, vs = row_slice(ch), v_slice(h)
        out = hid[it] * lax.rsqrt(ms[it] + NORM_EPS) * nw_ref[:, vs] * _sigmoid(o_ref[rs, vs].astype(F32))
        out_ref[rs, vs] = out.astype(out_ref.dtype)


def _mlstm(proj, gcol, grow, norm_w, *, batch, seq):
    tokens = proj.shape[0]
    rows = REC_ROWS
    steps = seq // rows
    row = lambda b, t: b * steps + t
    return pl.pallas_call(
        _mlstm_kernel,
        grid=(batch, steps),
        in_specs=[
            pl.BlockSpec((rows, ML_QK_WIDTH), lambda b, t: (row(b, t), 0)),
            pl.BlockSpec((rows, ML_QK_WIDTH), lambda b, t: (row(b, t), 1)),
            pl.BlockSpec((rows, ML_V_WIDTH), lambda b, t: (row(b, t), 1)),
            pl.BlockSpec((rows, ML_V_WIDTH), lambda b, t: (row(b, t), 2)),
            pl.BlockSpec((rows, LANES), lambda b, t: (row(b, t), 0)),
            pl.BlockSpec((rows // CHUNK, ML_GATE_ROWS, CHUNK), lambda b, t: (row(b, t), 0, 0)),
            pl.BlockSpec((1, ML_V_WIDTH), lambda b, t: (0, 0)),
        ],
        out_specs=pl.BlockSpec((rows, ML_V_WIDTH), lambda b, t: (row(b, t), 0)),
        out_shape=jax.ShapeDtypeStruct((tokens, ML_V_WIDTH), BF16),
        scratch_shapes=[
            pltpu.VMEM((ML_HEADS, ML_QK_DIM, ML_V_DIM), F32),
            pltpu.VMEM((ML_HEADS, 1, ML_QK_DIM), F32),
            pltpu.VMEM((ML_HEADS, 1, LANES), F32),
            pltpu.VMEM((ML_HEADS * rows // CHUNK, ML_QK_DIM, ML_V_DIM), BF16),
        ],
        compiler_params=_params("parallel", "arbitrary"),
        name="mlstm",
    )(proj, proj, proj, proj, gcol, grow, norm_w)


def _gdn_kernel(q_ref, k_ref, v_ref, z_ref, gcol_ref, grow_ref, nw_ref, out_ref,
                s_scr, wq_scr, attn_scr, kdt_scr, u_scr, eg_scr):
    rows = q_ref.shape[0]
    rep = GDN_V_HEADS // GDN_K_HEADS
    dh = GDN_HEAD_DIM
    n_kh = q_ref.shape[1] // dh
    n_vh = n_kh * rep
    n_ch = rows // CHUNK
    group = pl.program_id(1)

    @pl.when(pl.program_id(2) == 0)
    def _():
        s_scr[...] = jnp.zeros_like(s_scr)

    ri = _iota((CHUNK, CHUNK), 0)
    ci = _iota((CHUNK, CHUNK), 1)
    causal = ri >= ci
    strict = ri > ci
    eye = jnp.where(ri == ci, 1.0, 0.0).astype(F32)
    lane = _iota((CHUNK, LANES), 1)
    merge_mask = {}
    s = 1
    while s < CHUNK:
        merge_mask[s] = (ri // s - ci // s == 1) & (ri // (2 * s) == ci // (2 * s))
        s *= 2

    kk, qk, qs, ks = {}, {}, {}, {}
    for ch in range(n_ch):
        rs = slice(ch * CHUNK, (ch + 1) * CHUNK)
        for kh in range(n_kh):
            hs = slice(kh * dh, (kh + 1) * dh)
            qb = q_ref[rs, hs]
            kb = k_ref[rs, hs]
            qs[ch, kh], ks[ch, kh] = qb.astype(F32), kb.astype(F32)
            kk[ch, kh] = _dot_nt(kb, kb)
            qk[ch, kh] = _dot_nt(qb, kb)

    items = [(ch, vh) for ch in range(n_ch) for vh in range(n_vh)]
    x, a_mats, rhs = {}, {}, {}
    for it, (ch, vh) in enumerate(items):
        rs = slice(ch * CHUNK, (ch + 1) * CHUNK)
        kh = vh // rep
        head = group * n_vh + vh
        gc = gcol_ref[rs, :]
        beta_c = jnp.sum(jnp.where(lane == head, gc, 0.0), axis=1, keepdims=True)
        g_c = jnp.sum(jnp.where(lane == GDN_V_HEADS + head, gc, 0.0), axis=1, keepdims=True)
        g_r = grow_ref[ch, pl.ds(GDN_V_HEADS + head, 1), :]
        g_last = g_r[:, CHUNK - 1:CHUNK]
        decay = jnp.exp(jnp.where(causal, g_c - g_r, -jnp.inf))
        a_mat = jnp.where(strict, kk[ch, kh] * beta_c * decay, 0.0)
        x[it] = eye - jnp.where(merge_mask[1], a_mat, 0.0)
        a_mats[it] = a_mat

        q, k = qs[ch, kh], ks[ch, kh]
        v = v_ref[rs, vh * dh:(vh + 1) * dh].astype(F32)
        e_g = jnp.exp(g_c)
        rhs[it] = jnp.concatenate([v * beta_c, k * (beta_c * e_g)], axis=1).astype(BF16)
        attn_scr[it] = (qk[ch, kh] * decay).astype(BF16)
        wq_scr[it, CHUNK:2 * CHUNK, :] = (q * e_g).astype(BF16)
        kdt_scr[it] = (k * jnp.exp(g_last - g_c)).T.astype(BF16)
        eg_scr[it] = jnp.broadcast_to(jnp.exp(g_last), (1, LANES))

    s = 2
    while s < CHUNK:
        for it in range(len(items)):
            nb = jnp.where(merge_mask[s], a_mats[it], 0.0).astype(BF16)
            y = _dot(nb, x[it].astype(BF16))
            x[it] = x[it] - _dot(x[it].astype(BF16), y.astype(BF16))
        s *= 2

    for it in range(len(items)):
        sol = _dot(x[it].astype(BF16), rhs[it])
        u_scr[it] = sol[:, :dh]
        wq_scr[it, 0:CHUNK, :] = sol[:, dh:].astype(BF16)

    state = [s_scr[vh] for vh in range(n_vh)]
    for ch in range(n_ch):
        rs = slice(ch * CHUNK, (ch + 1) * CHUNK)
        base = ch * n_vh
        sb = [state[vh].astype(BF16) for vh in range(n_vh)]
        r = [_dot(wq_scr[base + vh], sb[vh]) for vh in range(n_vh)]
        vnb = [(u_scr[base + vh] - r[vh][:CHUNK]).astype(BF16) for vh in range(n_vh)]
        state = [state[vh] * eg_scr[base + vh][:, 0:1] + _dot(kdt_scr[base + vh], vnb[vh]) for vh in range(n_vh)]
        for vh in range(n_vh):
            out = r[vh][CHUNK:] + _dot(attn_scr[base + vh], vnb[vh])
            ms = jnp.mean(out * out, axis=1, keepdims=True)
            z = z_ref[rs, vh * dh:(vh + 1) * dh].astype(F32)
            gated = out * lax.rsqrt(ms + NORM_EPS) * nw_ref[...] * (z * _sigmoid(z))
            out_ref[rs, vh * dh:(vh + 1) * dh] = gated.astype(out_ref.dtype)
    for vh in range(n_vh):
        s_scr[vh] = state[vh]


def _gdn(proj, gcol, grow, norm_w, *, batch, seq):
    tokens = proj.shape[0]
    rows = REC_ROWS
    steps = seq // rows
    rep = GDN_V_HEADS // GDN_K_HEADS
    n_kh = GDN_STEP_K_HEADS
    n_vh = n_kh * rep
    kw = n_kh * GDN_HEAD_DIM
    vw = n_vh * GDN_HEAD_DIM
    n_items = n_vh * rows // CHUNK
    row = lambda b, t: b * steps + t
    return pl.pallas_call(
        _gdn_kernel,
        grid=(batch, GDN_K_HEADS // n_kh, steps),
        in_specs=[
            pl.BlockSpec((rows, kw), lambda b, g, t: (row(b, t), g)),
            pl.BlockSpec((rows, kw), lambda b, g, t: (row(b, t), GDN_KEY_WIDTH // kw + g)),
            pl.BlockSpec((rows, vw), lambda b, g, t: (row(b, t), 2 * GDN_KEY_WIDTH // vw + g)),
            pl.BlockSpec((rows, vw), lambda b, g, t: (row(b, t), GDN_CONV_DIM // vw + g)),
            pl.BlockSpec((rows, LANES), lambda b, g, t: (row(b, t), 0)),
            pl.BlockSpec((rows // CHUNK, GDN_GATE_ROWS, CHUNK), lambda b, g, t: (row(b, t), 0, 0)),
            pl.BlockSpec((1, GDN_HEAD_DIM), lambda b, g, t: (0, 0)),
        ],
        out_specs=pl.BlockSpec((rows, vw), lambda b, g, t: (row(b, t), g)),
        out_shape=jax.ShapeDtypeStruct((tokens, GDN_VAL_WIDTH), BF16),
        scratch_shapes=[
            pltpu.VMEM((n_vh, GDN_HEAD_DIM, GDN_HEAD_DIM), F32),
            pltpu.VMEM((n_items, 2 * CHUNK, GDN_HEAD_DIM), BF16),
            pltpu.VMEM((n_items, CHUNK, CHUNK), BF16),
            pltpu.VMEM((n_items, GDN_HEAD_DIM, CHUNK), BF16),
            pltpu.VMEM((n_items, CHUNK, GDN_HEAD_DIM), F32),
            pltpu.VMEM((n_items, 1, LANES), F32),
        ],
        compiler_params=_params("parallel", "parallel", "arbitrary"),
        name="gdn",
    )(proj, proj, proj, proj, gcol, grow, norm_w)


def _block_tail_kernel(a_ref, wo_ref, x_ref, g1_ref, sc_ref, sh_ref, g2_ref, wgu_ref, wd_ref, fw_ref, o_ref,
                       x1_scr, h_scr, act_scr, *, final_norm):
    d_ff = wd_ref.shape[0]
    x1 = x_ref[...] + g1_ref[0] * _dot(a_ref[...], wo_ref[...])
    x1_scr[...] = x1
    h_scr[...] = _modulated_norm(x1, sc_ref[0], sh_ref[0]).astype(BF16)
    for j in range(d_ff // FFN_TH):
        gate = _dot(h_scr[...], wgu_ref[:, j * FFN_TH:(j + 1) * FFN_TH])
        up = _dot(h_scr[...], wgu_ref[:, d_ff + j * FFN_TH:d_ff + (j + 1) * FFN_TH])
        act_scr[:, j * FFN_TH:(j + 1) * FFN_TH] = (gate * _sigmoid(gate) * up).astype(BF16)
    y = x1_scr[...] + g2_ref[0] * _dot(act_scr[...], wd_ref[...])
    if final_norm:
        ms = jnp.mean(y * y, axis=-1, keepdims=True)
        y = y * lax.rsqrt(ms + NORM_EPS) * fw_ref[...]
    o_ref[...] = y


def _block_tail(mixed, w_out, x2, g1, scale, shift, g2, w_gate_up, w_down, final_w, *, seq, final_norm):
    tokens, d = x2.shape
    kdim = mixed.shape[1]
    d_ff = w_down.shape[0]
    tm = FFN_TM
    per_batch = seq // tm
    mod_spec = pl.BlockSpec((1, 1, d), lambda i: (i // per_batch, 0, 0))
    resident = functools.partial(pl.BlockSpec, index_map=lambda i: (0, 0), pipeline_mode=pl.Buffered(1))
    return pl.pallas_call(
        functools.partial(_block_tail_kernel, final_norm=final_norm),
        grid=(tokens // tm,),
        in_specs=[
            pl.BlockSpec((tm, kdim), lambda i: (i, 0)),
            resident((kdim, d)),
            pl.BlockSpec((tm, d), lambda i: (i, 0)),
            mod_spec, mod_spec, mod_spec, mod_spec,
            resident((d, 2 * d_ff)),
            resident((d_ff, d)),
            resident((1, d)),
        ],
        out_specs=pl.BlockSpec((tm, d), lambda i: (i, 0)),
        out_shape=jax.ShapeDtypeStruct((tokens, d), F32),
        scratch_shapes=[pltpu.VMEM((tm, d), F32), pltpu.VMEM((tm, d), BF16), pltpu.VMEM((tm, d_ff), BF16)],
        compiler_params=_params("parallel"),
        name="block_tail",
    )(mixed, w_out, x2, g1, scale, shift, g2, w_gate_up, w_down, final_w)


def _gate_weights(w_in, start, n_rows):
    wg = w_in[:, start:]
    count = wg.shape[1]
    col = jnp.pad(wg, ((0, 0), (0, LANES - count))).astype(BF16)
    row = jnp.pad(wg.T, ((0, n_rows - count), (0, 0))).astype(BF16)
    return col, row


def _gate_vec(vec, offset, n_rows):
    count = vec.shape[0]
    col = jnp.zeros((1, LANES), F32).at[0, offset:offset + count].set(vec.astype(F32))
    row = jnp.zeros((n_rows, 1), F32).at[offset:offset + count, 0].set(vec.astype(F32))
    return col, row


def kernel(x, c, ada_w, ada_b, ml_w_in, ml_b_if, ml_norm_w, ml_w_out, gdn_w_in, gdn_conv_w, gdn_a_log,
           gdn_dt_bias, gdn_norm_w, gdn_w_out, ffn_w_gate_up, ffn_w_down, final_norm_w):
    batch, seq, d = x.shape
    depth = ada_w.shape[0]
    n_mixers = 2
    tokens = batch * seq
    x2 = x.reshape(tokens, d)

    c_rows = 8
    c_pad = jnp.pad(c, ((0, c_rows - batch), (0, 0)))
    mod = _adaln(c_pad, ada_w, ada_b)[:, :batch, :]

    for layer in range(depth):
        sh1, sc1, g1, sh2, sc2, g2 = [m.reshape(batch, 1, d) for m in jnp.split(mod[layer], 6, axis=-1)]
        jdx = layer // n_mixers
        if layer % n_mixers == 0:
            w_in = ml_w_in[jdx]
            wg_col, wg_row = _gate_weights(w_in, ML_MAIN, ML_GATE_ROWS)
            b_col, b_row = _gate_vec(ml_b_if[jdx], 0, ML_GATE_ROWS)
            proj, gcol, grow = _inproj(x2, sc1, sh1, w_in[:, :ML_MAIN].astype(BF16), wg_col, wg_row,
                                       b_col, b_row, mode="ml", seq=seq)
            mixed = _mlstm(proj, gcol, grow, ml_norm_w[jdx].reshape(1, ML_V_WIDTH).astype(F32),
                           batch=batch, seq=seq)
            w_out = ml_w_out[jdx]
        else:
            w_in = gdn_w_in[jdx]
            wg_col, wg_row = _gate_weights(w_in, GDN_MAIN, GDN_GATE_ROWS)
            b_col, b_row = _gate_vec(gdn_dt_bias[jdx], GDN_V_HEADS, GDN_GATE_ROWS)
            a_col, a_row = _gate_vec(gdn_a_log[jdx], GDN_V_HEADS, GDN_GATE_ROWS)
            proj, gcol, grow = _inproj(x2, sc1, sh1, w_in[:, :GDN_MAIN].astype(BF16), wg_col, wg_row,
                                       b_col, b_row, (a_col, a_row, gdn_conv_w[jdx].astype(F32)),
                                       mode="gdn", seq=seq)
            mixed = _gdn(proj, gcol, grow, gdn_norm_w[jdx].reshape(1, GDN_HEAD_DIM).astype(F32),
                         batch=batch, seq=seq)
            w_out = gdn_w_out[jdx]
        x2 = _block_tail(mixed, w_out.astype(BF16), x2, g1, sc2, sh2, g2,
                         ffn_w_gate_up[layer].astype(BF16), ffn_w_down[layer].astype(BF16),
                         final_norm_w.reshape(1, d).astype(F32), seq=seq, final_norm=(layer == depth - 1))
    return x2.reshape(batch, seq, d)
```

```python
import functools

import jax
import jax.numpy as jnp
from jax import lax
from jax.experimental import pallas as pl
from jax.experimental.pallas import tpu as pltpu

F32 = jnp.float32
BF16 = jnp.bfloat16
HIGHEST = lax.Precision.HIGHEST

LANES = 128
VMEM_LIMIT = 56 * 1024 * 1024

D_MODEL = 1024
CHUNK = 64
NORM_EPS = 1e-6

ML_HEADS = 4
ML_QK_DIM = 128
ML_V_DIM = 256
ML_QK_WIDTH = ML_HEADS * ML_QK_DIM
ML_V_WIDTH = ML_HEADS * ML_V_DIM
ML_MAIN = 2 * ML_QK_WIDTH + 2 * ML_V_WIDTH
ML_GATE_ROWS = 16
ML_GATE_CAP = 15.0

GDN_K_HEADS = 8
GDN_V_HEADS = 16
GDN_HEAD_DIM = 128
GDN_KEY_WIDTH = GDN_K_HEADS * GDN_HEAD_DIM
GDN_VAL_WIDTH = GDN_V_HEADS * GDN_HEAD_DIM
GDN_CONV_DIM = 2 * GDN_KEY_WIDTH + GDN_VAL_WIDTH
GDN_CONV_WIDTH = 4
GDN_MAIN = GDN_CONV_DIM + GDN_VAL_WIDTH
GDN_GATE_ROWS = 2 * GDN_V_HEADS

INPROJ_TM = 512
INPROJ_TN = 256
CONV_HALO = 8
REC_ROWS = 256
GDN_ROWS = 128
FFN_TM = 512
FFN_TH = 256


def _params(*sem):
    return pltpu.CompilerParams(dimension_semantics=sem, vmem_limit_bytes=VMEM_LIMIT)


def _sigmoid(x):
    return 1.0 / (1.0 + jnp.exp(-x))


def _softplus(x):
    return jnp.maximum(x, 0.0) + jnp.log1p(jnp.exp(-jnp.abs(x)))


def _dot(a, b):
    return jnp.dot(a, b, preferred_element_type=F32)


def _dot_nt(a, b):
    return lax.dot_general(a, b, (((1,), (1,)), ((), ())), preferred_element_type=F32)


def _dot_tn(a, b):
    return lax.dot_general(a, b, (((0,), (0,)), ((), ())), preferred_element_type=F32)


def _iota(shape, dim):
    return lax.broadcasted_iota(jnp.int32, shape, dim)


def _adaln_kernel(c_ref, w_ref, b_ref, o_ref):
    c = c_ref[...]
    act = c * _sigmoid(c)
    o_ref[0] = jnp.dot(act, w_ref[0], precision=HIGHEST, preferred_element_type=F32) + b_ref[0]


def _adaln(c_pad, ada_w, ada_b):
    depth, d, n = ada_w.shape
    rows = c_pad.shape[0]
    tn = D_MODEL
    return pl.pallas_call(
        _adaln_kernel,
        grid=(depth, n // tn),
        in_specs=[
            pl.BlockSpec((rows, d), lambda l, j: (0, 0)),
            pl.BlockSpec((1, d, tn), lambda l, j: (l, 0, j)),
            pl.BlockSpec((1, 1, tn), lambda l, j: (l, 0, j)),
        ],
        out_specs=pl.BlockSpec((1, rows, tn), lambda l, j: (l, 0, j)),
        out_shape=jax.ShapeDtypeStruct((depth, rows, n), F32),
        compiler_params=_params("parallel", "parallel"),
        name="adaln",
    )(c_pad, ada_w, ada_b.reshape(depth, 1, n))


def _modulated_norm(x, scale, shift):
    ms = jnp.mean(x * x, axis=-1, keepdims=True)
    return x * lax.rsqrt(ms + NORM_EPS) * (1.0 + scale) + shift


def _ml_gates(pre, idx):
    capped = ML_GATE_CAP * jnp.tanh(pre / ML_GATE_CAP)
    log_f = jnp.minimum(capped, 0.0) - jnp.log1p(jnp.exp(-jnp.abs(capped)))
    is_f = idx >= ML_HEADS
    return jnp.where(is_f, log_f, capped), is_f


def _gdn_gates(pre, a_log, idx):
    is_g = idx >= GDN_V_HEADS
    return jnp.where(is_g, -jnp.exp(a_log) * _softplus(pre), _sigmoid(pre)), is_g


def _conv_silu(y, conv_w, first_of_seq, halo_ref, buf):
    tm = y.shape[0]
    buf[0:CONV_HALO, :] = jnp.where(first_of_seq, 0.0, halo_ref[...])
    buf[CONV_HALO:CONV_HALO + tm, :] = y
    halo_ref[...] = y[tm - CONV_HALO:tm, :]
    acc = conv_w[GDN_CONV_WIDTH - 1:GDN_CONV_WIDTH, :] * y
    for j in range(GDN_CONV_WIDTH - 1):
        start = CONV_HALO - (GDN_CONV_WIDTH - 1 - j)
        acc = acc + conv_w[j:j + 1, :] * buf[start:start + tm, :]
    return acc * _sigmoid(acc)


def _inproj_kernel(*refs, mode, per_batch):
    if mode == "ml":
        (x_ref, sc_ref, sh_ref, w_ref, wg_ref, wgt_ref, bcol_ref, brow_ref,
         proj_ref, gcol_ref, grow_ref, h_scr) = refs
    else:
        (x_ref, sc_ref, sh_ref, w_ref, wg_ref, wgt_ref, bcol_ref, brow_ref, acol_ref, arow_ref, cw_ref,
         proj_ref, gcol_ref, grow_ref, h_scr, halo_scr, conv_buf) = refs
    tm = x_ref.shape[0]
    n = w_ref.shape[1]
    n_rows = wgt_ref.shape[0]
    tn = INPROJ_TN

    h_scr[...] = _modulated_norm(x_ref[...], sc_ref[0], sh_ref[0]).astype(BF16)

    def gate_stages():
        chunks = [slice(ch * CHUNK, (ch + 1) * CHUNK) for ch in range(tm // CHUNK)]
        lane = _iota((CHUNK, LANES), 1)
        row = _iota((n_rows, CHUNK), 0)
        rr = _iota((CHUNK, CHUNK), 0)
        cc = _iota((CHUNK, CHUNK), 1)
        lower = jnp.where(cc <= rr, 1.0, 0.0).astype(F32)
        upper = jnp.where(rr <= cc, 1.0, 0.0).astype(F32)
        pre = _dot(h_scr[...], wg_ref[...]) + bcol_ref[...]
        pre_t = [_dot_nt(wgt_ref[...], h_scr[rs, :]) + brow_ref[...] for rs in chunks]
        yield
        if mode == "ml":
            col = [_ml_gates(pre[rs, :], lane) for rs in chunks]
        else:
            col = [_gdn_gates(pre[rs, :], acol_ref[...], lane) for rs in chunks]
        yield
        if mode == "ml":
            rowg = [_ml_gates(p_t, row) for p_t in pre_t]
        else:
            rowg = [_gdn_gates(p_t, arow_ref[...], row) for p_t in pre_t]
        yield
        csum = [jnp.dot(lower, vals, precision=HIGHEST, preferred_element_type=F32) for vals, _ in col]
        csum_t = [jnp.dot(vals_t, upper, precision=HIGHEST, preferred_element_type=F32) for vals_t, _ in rowg]
        yield
        for ch, rs in enumerate(chunks):
            gcol_ref[rs, :] = jnp.where(col[ch][1], csum[ch], col[ch][0])
            grow_ref[ch] = jnp.where(rowg[ch][1], csum_t[ch], rowg[ch][0])
        yield

    stages = gate_stages()
    order = list(range(n // tn))
    if mode == "gdn":
        first_of_seq = pl.program_id(0) % per_batch == 0
        conv_slabs = [j for j in order if j * tn < GDN_CONV_DIM]
        plain_slabs = [j for j in order if j * tn >= GDN_CONV_DIM]
        order = []
        while conv_slabs or plain_slabs:
            order += conv_slabs[:2] + plain_slabs[:1]
            conv_slabs, plain_slabs = conv_slabs[2:], plain_slabs[1:]
    for j in order:
        next(stages, None)
        cs = slice(j * tn, (j + 1) * tn)
        y = _dot(h_scr[...], w_ref[:, cs])
        if mode == "gdn" and j * tn < GDN_CONV_DIM:
            y = _conv_silu(y, cw_ref[:, cs], first_of_seq, halo_scr.at[:, cs], conv_buf)
            if j * tn < 2 * GDN_KEY_WIDTH:
                q_scale = GDN_HEAD_DIM ** -0.5 if j * tn < GDN_KEY_WIDTH else 1.0
                for hh in range(tn // GDN_HEAD_DIM):
                    hs = slice(hh * GDN_HEAD_DIM, (hh + 1) * GDN_HEAD_DIM)
                    blk = y[:, hs]
                    ss = jnp.sum(blk * blk, axis=1, keepdims=True)
                    proj_ref[:, j * tn + hh * GDN_HEAD_DIM:j * tn + (hh + 1) * GDN_HEAD_DIM] = (
                        blk * (lax.rsqrt(ss + NORM_EPS) * q_scale)).astype(proj_ref.dtype)
                continue
        proj_ref[:, cs] = y.astype(proj_ref.dtype)
    for _ in stages:
        pass


def _inproj(x2, scale, shift, w_main, w_gate, w_gate_t, bias_col, bias_row, gdn_extra=None, *, mode, seq):
    tokens, d = x2.shape
    n = w_main.shape[1]
    n_rows = w_gate_t.shape[0]
    tm = INPROJ_TM
    per_batch = seq // tm
    const = lambda i: (0, 0)
    resident = functools.partial(pl.BlockSpec, index_map=const, pipeline_mode=pl.Buffered(1))
    in_specs = [
        pl.BlockSpec((tm, d), lambda i: (i, 0)),
        pl.BlockSpec((1, 1, d), lambda i: (i // per_batch, 0, 0)),
        pl.BlockSpec((1, 1, d), lambda i: (i // per_batch, 0, 0)),
        resident((d, n)),
        resident((d, LANES)),
        resident((n_rows, d)),
        resident((1, LANES)),
        resident((n_rows, 1)),
    ]
    operands = [x2, scale, shift, w_main, w_gate, w_gate_t, bias_col, bias_row]
    scratch = [pltpu.VMEM((tm, d), BF16)]
    if mode == "gdn":
        alog_col, alog_row, conv_w = gdn_extra
        in_specs += [resident((1, LANES)), resident((n_rows, 1)), resident((GDN_CONV_WIDTH, GDN_CONV_DIM))]
        operands += [alog_col, alog_row, conv_w]
        scratch += [pltpu.VMEM((CONV_HALO, GDN_CONV_DIM), F32),
                    pltpu.VMEM((tm + CONV_HALO, INPROJ_TN), F32)]
    return pl.pallas_call(
        functools.partial(_inproj_kernel, mode=mode, per_batch=per_batch),
        grid=(tokens // tm,),
        in_specs=in_specs,
        out_specs=[
            pl.BlockSpec((tm, n), lambda i: (i, 0)),
            pl.BlockSpec((tm, LANES), lambda i: (i, 0)),
            pl.BlockSpec((tm // CHUNK, n_rows, CHUNK), lambda i: (i, 0, 0)),
        ],
        out_shape=[
            jax.ShapeDtypeStruct((tokens, n), BF16),
            jax.ShapeDtypeStruct((tokens, LANES), F32),
            jax.ShapeDtypeStruct((tokens // CHUNK, n_rows, CHUNK), F32),
        ],
        scratch_shapes=scratch,
        compiler_params=_params("arbitrary"),
        name="inproj_" + mode,
    )(*operands)


def _mlstm_kernel(q_ref, k_ref, v_ref, o_ref, gcol_ref, grow_ref, nw_ref, out_ref,
                  c_scr, n_scr, m_scr, cprev_scr):
    rows = q_ref.shape[0]

    @pl.when(pl.program_id(1) == 0)
    def _():
        c_scr[...] = jnp.zeros_like(c_scr)
        n_scr[...] = jnp.zeros_like(n_scr)
        m_scr[...] = jnp.zeros_like(m_scr)

    causal = _iota((CHUNK, CHUNK), 0) >= _iota((CHUNK, CHUNK), 1)
    k_scale = ML_QK_DIM ** -0.5
    n_ch = rows // CHUNK
    items = [(ch, h) for ch in range(n_ch) for h in range(ML_HEADS)]

    def row_slice(ch):
        return slice(ch * CHUNK, (ch + 1) * CHUNK)

    def qk_slice(h):
        return slice(h * ML_QK_DIM, (h + 1) * ML_QK_DIM)

    def v_slice(h):
        return slice(h * ML_V_DIM, (h + 1) * ML_V_DIM)

    n_items = len(items)

    b_col, b_last, d_mat, g_end, m_intra, m_loc, k_w, d_c, d_n, scores = ({} for _ in range(10))
    for it, (ch, h) in enumerate(items):
        rs = row_slice(ch)
        gc = gcol_ref[rs, :]
        gr = grow_ref[ch]
        li_c = gc[:, h:h + 1]
        b_c = gc[:, ML_HEADS + h:ML_HEADS + h + 1]
        li_r = gr[h:h + 1, :]
        b_r = gr[ML_HEADS + h:ML_HEADS + h + 1, :]
        b_l = b_r[:, CHUNK - 1:CHUNK]
        b_col[it], b_last[it] = b_c, b_l
        d_mat[it] = jnp.where(causal, b_c - b_r + li_r, -jnp.inf)
        g_end[it] = b_l - b_c + li_c
    for it in range(n_items):
        m_intra[it] = jnp.max(d_mat[it], axis=1, keepdims=True)
        m_loc[it] = jnp.max(g_end[it], axis=0, keepdims=True)
    for it, (ch, h) in enumerate(items):
        k_w[it] = (k_ref[row_slice(ch), qk_slice(h)].astype(F32)
                   * (k_scale * jnp.exp(g_end[it] - m_loc[it])))
    for it, (ch, h) in enumerate(items):
        rs = row_slice(ch)
        d_c[it] = _dot_tn(k_w[it].astype(BF16), v_ref[rs, v_slice(h)])
        d_n[it] = jnp.sum(k_w[it], axis=0, keepdims=True)
        scores[it] = _dot_nt(q_ref[rs, qk_slice(h)], k_ref[rs, qk_slice(h)]) * k_scale

    m_prev, n_prev = {}, {}
    m_run = [m_scr[h][:, 0:1] for h in range(ML_HEADS)]
    n_run = [n_scr[h] for h in range(ML_HEADS)]
    for it, (ch, h) in enumerate(items):
        c_run = c_scr[h]
        cprev_scr[it] = c_run.astype(BF16)
        m_prev[it], n_prev[it] = m_run[h], n_run[h]
        m_new = jnp.maximum(b_last[it] + m_run[h], m_loc[it])
        a_old = jnp.exp(b_last[it] + m_run[h] - m_new)
        a_new = jnp.exp(m_loc[it] - m_new)
        c_scr[h] = a_old * c_run + a_new * d_c[it]
        n_run[h] = a_old * n_run[h] + a_new * d_n[it]
        m_run[h] = m_new
    for h in range(ML_HEADS):
        n_scr[h] = n_run[h]
        m_scr[h] = jnp.broadcast_to(m_run[h], (1, LANES))

    m_t, s_inter, p, num, den, hid, ms = ({} for _ in range(7))
    for it in range(n_items):
        m_inter = b_col[it] + m_prev[it]
        m_t[it] = jnp.maximum(m_inter, m_intra[it])
        s_inter[it] = jnp.exp(m_inter - m_t[it])
        p[it] = jnp.exp(d_mat[it] - m_t[it]) * scores[it]
    for it, (ch, h) in enumerate(items):
        rs = row_slice(ch)
        q = q_ref[rs, qk_slice(h)]
        num[it] = (s_inter[it] * _dot(q, cprev_scr[it])
                   + _dot(p[it].astype(BF16), v_ref[rs, v_slice(h)]))
        den[it] = (s_inter[it] * jnp.sum(q.astype(F32) * n_prev[it], axis=1, keepdims=True)
                   + jnp.sum(p[it], axis=1, keepdims=True))
    for it in range(n_items):
        hid[it] = num[it] / jnp.maximum(jnp.abs(den[it]), jnp.exp(-m_t[it]))
        ms[it] = jnp.mean(hid[it] * hid[it], axis=1, keepdims=True)
    for it, (ch, h) in enumerate(items):
        rs, vs = row_slice(ch), v_slice(h)
        out = hid[it] * lax.rsqrt(ms[it] + NORM_EPS) * nw_ref[:, vs] * _sigmoid(o_ref[rs, vs].astype(F32))
        out_ref[rs, vs] = out.astype(out_ref.dtype)


def _mlstm(proj, gcol, grow, norm_w, *, batch, seq):
    tokens = proj.shape[0]
    rows = REC_ROWS
    steps = seq // rows
    row = lambda b, t: b * steps + t
    return pl.pallas_call(
        _mlstm_kernel,
        grid=(batch, steps),
        in_specs=[
            pl.BlockSpec((rows, ML_QK_WIDTH), lambda b, t: (row(b, t), 0)),
            pl.BlockSpec((rows, ML_QK_WIDTH), lambda b, t: (row(b, t), 1)),
            pl.BlockSpec((rows, ML_V_WIDTH), lambda b, t: (row(b, t), 1)),
            pl.BlockSpec((rows, ML_V_WIDTH), lambda b, t: (row(b, t), 2)),
            pl.BlockSpec((rows, LANES), lambda b, t: (row(b, t), 0)),
            pl.BlockSpec((rows // CHUNK, ML_GATE_ROWS, CHUNK), lambda b, t: (row(b, t), 0, 0)),
            pl.BlockSpec((1, ML_V_WIDTH), lambda b, t: (0, 0)),
        ],
        out_specs=pl.BlockSpec((rows, ML_V_WIDTH), lambda b, t: (row(b, t), 0)),
        out_shape=jax.ShapeDtypeStruct((tokens, ML_V_WIDTH), BF16),
        scratch_shapes=[
            pltpu.VMEM((ML_HEADS, ML_QK_DIM, ML_V_DIM), F32),
            pltpu.VMEM((ML_HEADS, 1, ML_QK_DIM), F32),
            pltpu.VMEM((ML_HEADS, 1, LANES), F32),
            pltpu.VMEM((ML_HEADS * rows // CHUNK, ML_QK_DIM, ML_V_DIM), BF16),
        ],
        compiler_params=_params("parallel", "arbitrary"),
        name="mlstm",
    )(proj, proj, proj, proj, gcol, grow, norm_w)


def _gdn_kernel(q_ref, k_ref, v_ref, z_ref, gcol_ref, grow_ref, nw_ref, out_ref,
                s_scr, wq_scr, attn_scr, kdt_scr, u_scr, eg_scr):
    rows = q_ref.shape[0]
    rep = GDN_V_HEADS // GDN_K_HEADS
    dh = GDN_HEAD_DIM
    n_kh = q_ref.shape[1] // dh
    n_vh = n_kh * rep
    n_ch = rows // CHUNK
    @pl.when(pl.program_id(1) == 0)
    def _():
        s_scr[...] = jnp.zeros_like(s_scr)

    ri = _iota((CHUNK, CHUNK), 0)
    ci = _iota((CHUNK, CHUNK), 1)
    causal = ri >= ci
    strict = ri > ci
    eye = jnp.where(ri == ci, 1.0, 0.0).astype(F32)
    lane = _iota((CHUNK, LANES), 1)
    merge_mask = {}
    s = 1
    while s < CHUNK:
        merge_mask[s] = (ri // s - ci // s == 1) & (ri // (2 * s) == ci // (2 * s))
        s *= 2

    kk, qk, qs, ks = {}, {}, {}, {}
    for ch in range(n_ch):
        rs = slice(ch * CHUNK, (ch + 1) * CHUNK)
        for kh in range(n_kh):
            hs = slice(kh * dh, (kh + 1) * dh)
            qb = q_ref[rs, hs]
            kb = k_ref[rs, hs]
            qs[ch, kh], ks[ch, kh] = qb.astype(F32), kb.astype(F32)
            kk[ch, kh] = _dot_nt(kb, kb)
            qk[ch, kh] = _dot_nt(qb, kb)

    items = [(ch, vh) for ch in range(n_ch) for vh in range(n_vh)]
    x, a_mats, rhs = {}, {}, {}
    for it, (ch, vh) in enumerate(items):
        rs = slice(ch * CHUNK, (ch + 1) * CHUNK)
        kh = vh // rep
        gc = gcol_ref[rs, :]
        beta_c = jnp.sum(jnp.where(lane == vh, gc, 0.0), axis=1, keepdims=True)
        g_c = jnp.sum(jnp.where(lane == GDN_V_HEADS + vh, gc, 0.0), axis=1, keepdims=True)
        g_r = grow_ref[ch, GDN_V_HEADS + vh:GDN_V_HEADS + vh + 1, :]
        g_last = g_r[:, CHUNK - 1:CHUNK]
        decay = jnp.exp(jnp.where(causal, g_c - g_r, -jnp.inf))
        a_mat = jnp.where(strict, kk[ch, kh] * beta_c * decay, 0.0)
        x[it] = eye - jnp.where(merge_mask[1], a_mat, 0.0)
        a_mats[it] = a_mat

        q, k = qs[ch, kh], ks[ch, kh]
        v = v_ref[rs, vh * dh:(vh + 1) * dh].astype(F32)
        e_g = jnp.exp(g_c)
        rhs[it] = jnp.concatenate([v * beta_c, k * (beta_c * e_g)], axis=1).astype(BF16)
        attn_scr[it] = (qk[ch, kh] * decay).astype(BF16)
        wq_scr[it, CHUNK:2 * CHUNK, :] = (q * e_g).astype(BF16)
        kdt_scr[it] = (k * jnp.exp(g_last - g_c)).T.astype(BF16)
        eg_scr[it] = jnp.broadcast_to(jnp.exp(g_last), (1, LANES))

    s = 2
    while s < CHUNK:
        for it in range(len(items)):
            nb = jnp.where(merge_mask[s], a_mats[it], 0.0).astype(BF16)
            y = _dot(nb, x[it].astype(BF16))
            x[it] = x[it] - _dot(x[it].astype(BF16), y.astype(BF16))
        s *= 2

    for it in range(len(items)):
        sol = _dot(x[it].astype(BF16), rhs[it])
        u_scr[it] = sol[:, :dh]
        wq_scr[it, 0:CHUNK, :] = sol[:, dh:].astype(BF16)

    state = [s_scr[vh] for vh in range(n_vh)]
    for ch in range(n_ch):
        rs = slice(ch * CHUNK, (ch + 1) * CHUNK)
        base = ch * n_vh
        sb = [state[vh].astype(BF16) for vh in range(n_vh)]
        r = [_dot(wq_scr[base + vh], sb[vh]) for vh in range(n_vh)]
        vnb = [(u_scr[base + vh] - r[vh][:CHUNK]).astype(BF16) for vh in range(n_vh)]
        state = [state[vh] * eg_scr[base + vh][:, 0:1] + _dot(kdt_scr[base + vh], vnb[vh]) for vh in range(n_vh)]
        for vh in range(n_vh):
            out = r[vh][CHUNK:] + _dot(attn_scr[base + vh], vnb[vh])
            ms = jnp.mean(out * out, axis=1, keepdims=True)
            z = z_ref[rs, vh * dh:(vh + 1) * dh].astype(F32)
            gated = out * lax.rsqrt(ms + NORM_EPS) * nw_ref[...] * (z * _sigmoid(z))
            out_ref[rs, vh * dh:(vh + 1) * dh] = gated.astype(out_ref.dtype)
    for vh in range(n_vh):
        s_scr[vh] = state[vh]


def _gdn(proj, gcol, grow, norm_w, *, batch, seq):
    tokens = proj.shape[0]
    rows = GDN_ROWS
    steps = seq // rows
    n_items = GDN_V_HEADS * rows // CHUNK
    row = lambda b, t: b * steps + t
    return pl.pallas_call(
        _gdn_kernel,
        grid=(batch, steps),
        in_specs=[
            pl.BlockSpec((rows, GDN_KEY_WIDTH), lambda b, t: (row(b, t), 0)),
            pl.BlockSpec((rows, GDN_KEY_WIDTH), lambda b, t: (row(b, t), 1)),
            pl.BlockSpec((rows, GDN_VAL_WIDTH), lambda b, t: (row(b, t), 2 * GDN_KEY_WIDTH // GDN_VAL_WIDTH)),
            pl.BlockSpec((rows, GDN_VAL_WIDTH), lambda b, t: (row(b, t), GDN_CONV_DIM // GDN_VAL_WIDTH)),
            pl.BlockSpec((rows, LANES), lambda b, t: (row(b, t), 0)),
            pl.BlockSpec((rows // CHUNK, GDN_GATE_ROWS, CHUNK), lambda b, t: (row(b, t), 0, 0)),
            pl.BlockSpec((1, GDN_HEAD_DIM), lambda b, t: (0, 0)),
        ],
        out_specs=pl.BlockSpec((rows, GDN_VAL_WIDTH), lambda b, t: (row(b, t), 0)),
        out_shape=jax.ShapeDtypeStruct((tokens, GDN_VAL_WIDTH), BF16),
        scratch_shapes=[
            pltpu.VMEM((GDN_V_HEADS, GDN_HEAD_DIM, GDN_HEAD_DIM), F32),
            pltpu.VMEM((n_items, 2 * CHUNK, GDN_HEAD_DIM), BF16),
            pltpu.VMEM((n_items, CHUNK, CHUNK), BF16),
            pltpu.VMEM((n_items, GDN_HEAD_DIM, CHUNK), BF16),
            pltpu.VMEM((n_items, CHUNK, GDN_HEAD_DIM), F32),
            pltpu.VMEM((n_items, 1, LANES), F32),
        ],
        compiler_params=_params("parallel", "arbitrary"),
        name="gdn",
    )(proj, proj, proj, proj, gcol, grow, norm_w)


def _block_tail_kernel(a_ref, wo_ref, x_ref, g1_ref, sc_ref, sh_ref, g2_ref, wgu_ref, wd_ref, fw_ref, o_ref,
                       x1_scr, h_scr, act_scr, *, final_norm):
    d_ff = wd_ref.shape[0]
    x1 = x_ref[...] + g1_ref[0] * _dot(a_ref[...], wo_ref[...])
    x1_scr[...] = x1
    h_scr[...] = _modulated_norm(x1, sc_ref[0], sh_ref[0]).astype(BF16)
    for j in range(d_ff // FFN_TH):
        gate = _dot(h_scr[...], wgu_ref[:, j * FFN_TH:(j + 1) * FFN_TH])
        up = _dot(h_scr[...], wgu_ref[:, d_ff + j * FFN_TH:d_ff + (j + 1) * FFN_TH])
        act_scr[:, j * FFN_TH:(j + 1) * FFN_TH] = (gate * _sigmoid(gate) * up).astype(BF16)
    y = x1_scr[...] + g2_ref[0] * _dot(act_scr[...], wd_ref[...])
    if final_norm:
        ms = jnp.mean(y * y, axis=-1, keepdims=True)
        y = y * lax.rsqrt(ms + NORM_EPS) * fw_ref[...]
    o_ref[...] = y


def _block_tail(mixed, w_out, x2, g1, scale, shift, g2, w_gate_up, w_down, final_w, *, seq, final_norm):
    tokens, d = x2.shape
    kdim = mixed.shape[1]
    d_ff = w_down.shape[0]
    tm = FFN_TM
    per_batch = seq // tm
    mod_spec = pl.BlockSpec((1, 1, d), lambda i: (i // per_batch, 0, 0))
    resident = functools.partial(pl.BlockSpec, index_map=lambda i: (0, 0), pipeline_mode=pl.Buffered(1))
    return pl.pallas_call(
        functools.partial(_block_tail_kernel, final_norm=final_norm),
        grid=(tokens // tm,),
        in_specs=[
            pl.BlockSpec((tm, kdim), lambda i: (i, 0)),
            resident((kdim, d)),
            pl.BlockSpec((tm, d), lambda i: (i, 0)),
            mod_spec, mod_spec, mod_spec, mod_spec,
            resident((d, 2 * d_ff)),
            resident((d_ff, d)),
            resident((1, d)),
        ],
        out_specs=pl.BlockSpec((tm, d), lambda i: (i, 0)),
        out_shape=jax.ShapeDtypeStruct((tokens, d), F32),
        scratch_shapes=[pltpu.VMEM((tm, d), F32), pltpu.VMEM((tm, d), BF16), pltpu.VMEM((tm, d_ff), BF16)],
        compiler_params=_params("parallel"),
        name="block_tail",
    )(mixed, w_out, x2, g1, scale, shift, g2, w_gate_up, w_down, final_w)


def _gate_weights(w_in, start, n_rows):
    wg = w_in[:, start:]
    count = wg.shape[1]
    col = jnp.pad(wg, ((0, 0), (0, LANES - count))).astype(BF16)
    row = jnp.pad(wg.T, ((0, n_rows - count), (0, 0))).astype(BF16)
    return col, row


def _gate_vec(vec, offset, n_rows):
    count = vec.shape[0]
    col = jnp.zeros((1, LANES), F32).at[0, offset:offset + count].set(vec.astype(F32))
    row = jnp.zeros((n_rows, 1), F32).at[offset:offset + count, 0].set(vec.astype(F32))
    return col, row


def kernel(x, c, ada_w, ada_b, ml_w_in, ml_b_if, ml_norm_w, ml_w_out, gdn_w_in, gdn_conv_w, gdn_a_log,
           gdn_dt_bias, gdn_norm_w, gdn_w_out, ffn_w_gate_up, ffn_w_down, final_norm_w):
    batch, seq, d = x.shape
    depth = ada_w.shape[0]
    n_mixers = 2
    tokens = batch * seq
    x2 = x.reshape(tokens, d)

    c_rows = 8
    c_pad = jnp.pad(c, ((0, c_rows - batch), (0, 0)))
    mod = _adaln(c_pad, ada_w, ada_b)[:, :batch, :]

    for layer in range(depth):
        sh1, sc1, g1, sh2, sc2, g2 = [m.reshape(batch, 1, d) for m in jnp.split(mod[layer], 6, axis=-1)]
        jdx = layer // n_mixers
        if layer % n_mixers == 0:
            w_in = ml_w_in[jdx]
            wg_col, wg_row = _gate_weights(w_in, ML_MAIN, ML_GATE_ROWS)
            b_col, b_row = _gate_vec(ml_b_if[jdx], 0, ML_GATE_ROWS)
            proj, gcol, grow = _inproj(x2, sc1, sh1, w_in[:, :ML_MAIN].astype(BF16), wg_col, wg_row,
                                       b_col, b_row, mode="ml", seq=seq)
            mixed = _mlstm(proj, gcol, grow, ml_norm_w[jdx].reshape(1, ML_V_WIDTH).astype(F32),
                           batch=batch, seq=seq)
            w_out = ml_w_out[jdx]
        else:
            w_in = gdn_w_in[jdx]
            wg_col, wg_row = _gate_weights(w_in, GDN_MAIN, GDN_GATE_ROWS)
            b_col, b_row = _gate_vec(gdn_dt_bias[jdx], GDN_V_HEADS, GDN_GATE_ROWS)
            a_col, a_row = _gate_vec(gdn_a_log[jdx], GDN_V_HEADS, GDN_GATE_ROWS)
            proj, gcol, grow = _inproj(x2, sc1, sh1, w_in[:, :GDN_MAIN].astype(BF16), wg_col, wg_row,
                                       b_col, b_row, (a_col, a_row, gdn_conv_w[jdx].astype(F32)),
                                       mode="gdn", seq=seq)
            mixed = _gdn(proj, gcol, grow, gdn_norm_w[jdx].reshape(1, GDN_HEAD_DIM).astype(F32),
                         batch=batch, seq=seq)
            w_out = gdn_w_out[jdx]
        x2 = _block_tail(mixed, w_out.astype(BF16), x2, g1, sc2, sh2, g2,
                         ffn_w_gate_up[layer].astype(BF16), ffn_w_down[layer].astype(BF16),
                         final_norm_w.reshape(1, d).astype(F32), seq=seq, final_norm=(layer == depth - 1))
    return x2.reshape(batch, seq, d)
```

```python
import functools

import jax
import jax.numpy as jnp
from jax import lax
from jax.experimental import pallas as pl
from jax.experimental.pallas import tpu as pltpu

F32 = jnp.float32
BF16 = jnp.bfloat16
HIGHEST = lax.Precision.HIGHEST

LANES = 128
VMEM_LIMIT = 56 * 1024 * 1024

D_MODEL = 1024
CHUNK = 64
NORM_EPS = 1e-6

ML_HEADS = 4
ML_QK_DIM = 128
ML_V_DIM = 256
ML_QK_WIDTH = ML_HEADS * ML_QK_DIM
ML_V_WIDTH = ML_HEADS * ML_V_DIM
ML_MAIN = 2 * ML_QK_WIDTH + 2 * ML_V_WIDTH
ML_GATE_ROWS = 16
ML_GATE_CAP = 15.0

GDN_K_HEADS = 8
GDN_V_HEADS = 16
GDN_HEAD_DIM = 128
GDN_KEY_WIDTH = GDN_K_HEADS * GDN_HEAD_DIM
GDN_VAL_WIDTH = GDN_V_HEADS * GDN_HEAD_DIM
GDN_CONV_DIM = 2 * GDN_KEY_WIDTH + GDN_VAL_WIDTH
GDN_CONV_WIDTH = 4
GDN_MAIN = GDN_CONV_DIM + GDN_VAL_WIDTH
GDN_GATE_ROWS = 2 * GDN_V_HEADS

ADALN_TN = 2048
INPROJ_TM = 512
INPROJ_TN = 256
CONV_HALO = 8
PIPE_PIECES = 4
REC_ROWS = 256
GDN_ROWS = 128
FFN_TM = 512
FFN_TH = 256


def _params(*sem):
    return pltpu.CompilerParams(dimension_semantics=sem, vmem_limit_bytes=VMEM_LIMIT)


def _sigmoid(x):
    return 1.0 / (1.0 + jnp.exp(-x))


def _softplus(x):
    return jnp.maximum(x, 0.0) + jnp.log1p(jnp.exp(-jnp.abs(x)))


def _dot(a, b):
    return jnp.dot(a, b, preferred_element_type=F32)


def _dot_nt(a, b):
    return lax.dot_general(a, b, (((1,), (1,)), ((), ())), preferred_element_type=F32)


def _dot_tn(a, b):
    return lax.dot_general(a, b, (((0,), (0,)), ((), ())), preferred_element_type=F32)


def _iota(shape, dim):
    return lax.broadcasted_iota(jnp.int32, shape, dim)


def _adaln_kernel(c_ref, w_ref, b_ref, o_ref):
    c = c_ref[...]
    act = c * _sigmoid(c)
    o_ref[0] = jnp.dot(act, w_ref[0], precision=HIGHEST, preferred_element_type=F32) + b_ref[0]


def _adaln(c_pad, ada_w, ada_b):
    depth, d, n = ada_w.shape
    rows = c_pad.shape[0]
    tn = ADALN_TN
    return pl.pallas_call(
        _adaln_kernel,
        grid=(depth, n // tn),
        in_specs=[
            pl.BlockSpec((rows, d), lambda l, j: (0, 0)),
            pl.BlockSpec((1, d, tn), lambda l, j: (l, 0, j)),
            pl.BlockSpec((1, 1, tn), lambda l, j: (l, 0, j)),
        ],
        out_specs=pl.BlockSpec((1, rows, tn), lambda l, j: (l, 0, j)),
        out_shape=jax.ShapeDtypeStruct((depth, rows, n), F32),
        compiler_params=_params("parallel", "parallel"),
        name="adaln",
    )(c_pad, ada_w, ada_b.reshape(depth, 1, n))


def _modulated_norm(x, scale, shift):
    ms = jnp.mean(x * x, axis=-1, keepdims=True)
    return x * lax.rsqrt(ms + NORM_EPS) * (1.0 + scale) + shift


def _ml_gates(pre, idx):
    capped = ML_GATE_CAP * jnp.tanh(pre / ML_GATE_CAP)
    log_f = jnp.minimum(capped, 0.0) - jnp.log1p(jnp.exp(-jnp.abs(capped)))
    is_f = idx >= ML_HEADS
    return jnp.where(is_f, log_f, capped), is_f


def _gdn_gates(pre, a_log, idx):
    is_g = idx >= GDN_V_HEADS
    return jnp.where(is_g, -jnp.exp(a_log) * _softplus(pre), _sigmoid(pre)), is_g


def _inproj_kernel(*refs, mode, per_batch):
    if mode == "ml":
        (x_ref, sc_ref, sh_ref, w_ref, wg_ref, wgt_ref, bcol_ref, brow_ref,
         proj_ref, gcol_ref, grow_ref, h_scr) = refs
    else:
        (x_ref, sc_ref, sh_ref, w_ref, wg_ref, wgt_ref, bcol_ref, brow_ref, acol_ref, arow_ref, cw_ref,
         proj_ref, gcol_ref, grow_ref, h_scr, halo_scr, conv_buf) = refs
    tm = x_ref.shape[0]
    n = proj_ref.shape[1]
    n_rows = wgt_ref.shape[0]
    tn = INPROJ_TN

    h_scr[...] = _modulated_norm(x_ref[...], sc_ref[0], sh_ref[0]).astype(BF16)

    def gate_stages():
        chunks = [slice(ch * CHUNK, (ch + 1) * CHUNK) for ch in range(tm // CHUNK)]
        lane = _iota((CHUNK, LANES), 1)
        row = _iota((n_rows, CHUNK), 0)
        rr = _iota((CHUNK, CHUNK), 0)
        cc = _iota((CHUNK, CHUNK), 1)
        lower = jnp.where(cc <= rr, 1.0, 0.0).astype(F32)
        upper = jnp.where(rr <= cc, 1.0, 0.0).astype(F32)
        pre = _dot(h_scr[...], wg_ref[...]) + bcol_ref[...]
        pre_t = [_dot_nt(wgt_ref[...], h_scr[rs, :]) + brow_ref[...] for rs in chunks]
        yield
        if mode == "ml":
            col = [_ml_gates(pre[rs, :], lane) for rs in chunks]
        else:
            col = [_gdn_gates(pre[rs, :], acol_ref[...], lane) for rs in chunks]
        yield
        if mode == "ml":
            rowg = [_ml_gates(p_t, row) for p_t in pre_t]
        else:
            rowg = [_gdn_gates(p_t, arow_ref[...], row) for p_t in pre_t]
        yield
        csum = [jnp.dot(lower, vals, precision=HIGHEST, preferred_element_type=F32) for vals, _ in col]
        csum_t = [jnp.dot(vals_t, upper, precision=HIGHEST, preferred_element_type=F32) for vals_t, _ in rowg]
        yield
        for ch, rs in enumerate(chunks):
            gcol_ref[rs, :] = jnp.where(col[ch][1], csum[ch], col[ch][0])
            grow_ref[ch] = jnp.where(rowg[ch][1], csum_t[ch], rowg[ch][0])
        yield

    stages = gate_stages()
    order = list(range(n // tn))
    if mode == "gdn":
        first_of_seq = pl.program_id(0) % per_batch == 0
        conv_slabs = [j for j in order if j * tn < GDN_CONV_DIM]
        plain_slabs = [j for j in order if j * tn >= GDN_CONV_DIM]
        order = []
        while conv_slabs or plain_slabs:
            order += conv_slabs[:2] + plain_slabs[:1]
            conv_slabs, plain_slabs = conv_slabs[2:], plain_slabs[1:]
    d = x_ref.shape[1]
    kp, rp = d // PIPE_PIECES, tm // PIPE_PIECES

    def slab_cols(j):
        return slice(j * tn, (j + 1) * tn)

    def matmul_piece(j, kk):
        return _dot(h_scr[:, kk * kp:(kk + 1) * kp], w_ref[kk * kp:(kk + 1) * kp, slab_cols(j)])

    def start_epilogue(j, y):
        if mode == "gdn" and j * tn < GDN_CONV_DIM:
            halo_ref = halo_scr.at[:, slab_cols(j)]
            conv_buf[0:CONV_HALO, :] = jnp.where(first_of_seq, 0.0, halo_ref[...])
            conv_buf[CONV_HALO:CONV_HALO + tm, :] = y
            halo_ref[...] = y[tm - CONV_HALO:tm, :]

    def epilogue_piece(j, y, r):
        cs, rows = slab_cols(j), slice(r * rp, (r + 1) * rp)
        if not (mode == "gdn" and j * tn < GDN_CONV_DIM):
            proj_ref[rows, cs] = y[rows, :].astype(proj_ref.dtype)
            return
        conv_w = cw_ref[:, cs]
        acc = conv_w[GDN_CONV_WIDTH - 1:GDN_CONV_WIDTH, :] * y[rows, :]
        for tap in range(GDN_CONV_WIDTH - 1):
            first = CONV_HALO - (GDN_CONV_WIDTH - 1 - tap) + r * rp
            acc = acc + conv_w[tap:tap + 1, :] * conv_buf[first:first + rp, :]
        acc = acc * _sigmoid(acc)
        if j * tn >= 2 * GDN_KEY_WIDTH:
            proj_ref[rows, cs] = acc.astype(proj_ref.dtype)
            return
        q_scale = GDN_HEAD_DIM ** -0.5 if j * tn < GDN_KEY_WIDTH else 1.0
        for hh in range(tn // GDN_HEAD_DIM):
            hs = slice(hh * GDN_HEAD_DIM, (hh + 1) * GDN_HEAD_DIM)
            blk = acc[:, hs]
            ss = jnp.sum(blk * blk, axis=1, keepdims=True)
            proj_ref[rows, j * tn + hh * GDN_HEAD_DIM:j * tn + (hh + 1) * GDN_HEAD_DIM] = (
                blk * (lax.rsqrt(ss + NORM_EPS) * q_scale)).astype(proj_ref.dtype)

    y = _dot(h_scr[...], w_ref[:, slab_cols(order[0])])
    for idx, j in enumerate(order):
        next(stages, None)
        start_epilogue(j, y)
        following = order[idx + 1] if idx + 1 < len(order) else None
        y_next = None
        for piece in range(PIPE_PIECES):
            if following is not None:
                part = matmul_piece(following, piece)
                y_next = part if y_next is None else y_next + part
            epilogue_piece(j, y, piece)
        y = y_next
    for _ in stages:
        pass


def _inproj(x2, scale, shift, w_all, w_gate, w_gate_t, bias_col, bias_row, gdn_extra=None, *, mode, seq):
    tokens, d = x2.shape
    n = ML_MAIN if mode == "ml" else GDN_MAIN
    n_rows = w_gate_t.shape[0]
    tm = INPROJ_TM
    per_batch = seq // tm
    const = lambda i: (0, 0)
    resident = functools.partial(pl.BlockSpec, index_map=const, pipeline_mode=pl.Buffered(1))
    in_specs = [
        pl.BlockSpec((tm, d), lambda i: (i, 0)),
        pl.BlockSpec((1, 1, d), lambda i: (i // per_batch, 0, 0)),
        pl.BlockSpec((1, 1, d), lambda i: (i // per_batch, 0, 0)),
        resident(w_all.shape),
        resident((d, LANES)),
        resident((n_rows, d)),
        resident((1, LANES)),
        resident((n_rows, 1)),
    ]
    operands = [x2, scale, shift, w_all, w_gate, w_gate_t, bias_col, bias_row]
    scratch = [pltpu.VMEM((tm, d), BF16)]
    if mode == "gdn":
        alog_col, alog_row, conv_w = gdn_extra
        in_specs += [resident((1, LANES)), resident((n_rows, 1)), resident((GDN_CONV_WIDTH, GDN_CONV_DIM))]
        operands += [alog_col, alog_row, conv_w]
        scratch += [pltpu.VMEM((CONV_HALO, GDN_CONV_DIM), F32),
                    pltpu.VMEM((tm + CONV_HALO, INPROJ_TN), F32)]
    return pl.pallas_call(
        functools.partial(_inproj_kernel, mode=mode, per_batch=per_batch),
        grid=(tokens // tm,),
        in_specs=in_specs,
        out_specs=[
            pl.BlockSpec((tm, n), lambda i: (i, 0)),
            pl.BlockSpec((tm, LANES), lambda i: (i, 0)),
            pl.BlockSpec((tm // CHUNK, n_rows, CHUNK), lambda i: (i, 0, 0)),
        ],
        out_shape=[
            jax.ShapeDtypeStruct((tokens, n), BF16),
            jax.ShapeDtypeStruct((tokens, LANES), F32),
            jax.ShapeDtypeStruct((tokens // CHUNK, n_rows, CHUNK), F32),
        ],
        scratch_shapes=scratch,
        compiler_params=_params("arbitrary"),
        name="inproj_" + mode,
    )(*operands)


def _mlstm_kernel(q_ref, k_ref, v_ref, o_ref, gcol_ref, grow_ref, nw_ref, out_ref,
                  c_scr, n_scr, m_scr, cprev_scr):
    rows = q_ref.shape[0]

    @pl.when(pl.program_id(1) == 0)
    def _():
        c_scr[...] = jnp.zeros_like(c_scr)
        n_scr[...] = jnp.zeros_like(n_scr)
        m_scr[...] = jnp.zeros_like(m_scr)

    causal = _iota((CHUNK, CHUNK), 0) >= _iota((CHUNK, CHUNK), 1)
    k_scale = ML_QK_DIM ** -0.5
    n_ch = rows // CHUNK
    items = [(ch, h) for ch in range(n_ch) for h in range(ML_HEADS)]

    def row_slice(ch):
        return slice(ch * CHUNK, (ch + 1) * CHUNK)

    def qk_slice(h):
        return slice(h * ML_QK_DIM, (h + 1) * ML_QK_DIM)

    def v_slice(h):
        return slice(h * ML_V_DIM, (h + 1) * ML_V_DIM)

    n_items = len(items)

    lane = _iota((CHUNK, LANES), 1)

    def lane_rep(col):
        return jnp.broadcast_to(col, (CHUNK, LANES))

    def wide(a):
        return jnp.concatenate([a] * (ML_V_DIM // LANES), axis=1)

    b_col, b_last, d_mat, g_end, m_intra, m_loc, k_w, d_c, d_n, scores = ({} for _ in range(10))
    for it, (ch, h) in enumerate(items):
        rs = row_slice(ch)
        gc = gcol_ref[rs, :]
        gr = grow_ref[ch]
        li_c = lane_rep(jnp.sum(jnp.where(lane == h, gc, 0.0), axis=1, keepdims=True))
        b_c = lane_rep(jnp.sum(jnp.where(lane == ML_HEADS + h, gc, 0.0), axis=1, keepdims=True))
        li_r = gr[h:h + 1, :]
        b_r = gr[ML_HEADS + h:ML_HEADS + h + 1, :]
        b_col[it], b_last[it] = b_c, b_c[CHUNK - 1:CHUNK, :]
        d_mat[it] = jnp.where(causal, b_c[:, :CHUNK] - b_r + li_r, -jnp.inf)
        g_end[it] = b_last[it] - b_c + li_c
    for it in range(n_items):
        m_intra[it] = jnp.max(d_mat[it], axis=1, keepdims=True)
        m_loc[it] = jnp.max(g_end[it], axis=0, keepdims=True)
    for it, (ch, h) in enumerate(items):
        k_w[it] = (k_ref[row_slice(ch), qk_slice(h)].astype(F32)
                   * (k_scale * jnp.exp(g_end[it] - m_loc[it])))
    for it, (ch, h) in enumerate(items):
        rs = row_slice(ch)
        d_c[it] = _dot_tn(k_w[it].astype(BF16), v_ref[rs, v_slice(h)])
        d_n[it] = jnp.sum(k_w[it], axis=0, keepdims=True)
        scores[it] = _dot_nt(q_ref[rs, qk_slice(h)], k_ref[rs, qk_slice(h)]) * k_scale

    m_prev, n_prev = {}, {}
    m_run = [m_scr[h] for h in range(ML_HEADS)]
    n_run = [n_scr[h] for h in range(ML_HEADS)]
    for it, (ch, h) in enumerate(items):
        c_run = c_scr[h]
        cprev_scr[it] = c_run.astype(BF16)
        m_prev[it], n_prev[it] = m_run[h], n_run[h]
        m_new = jnp.maximum(b_last[it] + m_run[h], m_loc[it])
        a_old = jnp.exp(b_last[it] + m_run[h] - m_new)
        a_new = jnp.exp(m_loc[it] - m_new)
        c_scr[h] = wide(a_old) * c_run + wide(a_new) * d_c[it]
        n_run[h] = a_old * n_run[h] + a_new * d_n[it]
        m_run[h] = m_new
    for h in range(ML_HEADS):
        n_scr[h] = n_run[h]
        m_scr[h] = m_run[h]

    m_t, s_inter, p, num, den, hid, ms = ({} for _ in range(7))
    for it in range(n_items):
        m_inter = b_col[it] + m_prev[it]
        m_t[it] = jnp.maximum(m_inter, m_intra[it])
        s_inter[it] = jnp.exp(m_inter - m_t[it])
        p[it] = jnp.exp(d_mat[it] - m_t[it][:, :CHUNK]) * scores[it]
    for it, (ch, h) in enumerate(items):
        rs = row_slice(ch)
        q = q_ref[rs, qk_slice(h)]
        num[it] = (wide(s_inter[it]) * _dot(q, cprev_scr[it])
                   + _dot(p[it].astype(BF16), v_ref[rs, v_slice(h)]))
        den[it] = (s_inter[it] * jnp.sum(q.astype(F32) * n_prev[it], axis=1, keepdims=True)
                   + jnp.sum(p[it], axis=1, keepdims=True))
    for it in range(n_items):
        hid[it] = num[it] * wide(1.0 / jnp.maximum(jnp.abs(den[it]), jnp.exp(-m_t[it])))
        ms[it] = jnp.mean(hid[it] * hid[it], axis=1, keepdims=True)
    for it, (ch, h) in enumerate(items):
        rs, vs = row_slice(ch), v_slice(h)
        out = hid[it] * lax.rsqrt(ms[it] + NORM_EPS) * nw_ref[:, vs] * _sigmoid(o_ref[rs, vs].astype(F32))
        out_ref[rs, vs] = out.astype(out_ref.dtype)


def _mlstm(proj, gcol, grow, norm_w, *, batch, seq):
    tokens = proj.shape[0]
    rows = REC_ROWS
    steps = seq // rows
    row = lambda b, t: b * steps + t
    return pl.pallas_call(
        _mlstm_kernel,
        grid=(batch, steps),
        in_specs=[
            pl.BlockSpec((rows, ML_QK_WIDTH), lambda b, t: (row(b, t), 0)),
            pl.BlockSpec((rows, ML_QK_WIDTH), lambda b, t: (row(b, t), 1)),
            pl.BlockSpec((rows, ML_V_WIDTH), lambda b, t: (row(b, t), 1)),
            pl.BlockSpec((rows, ML_V_WIDTH), lambda b, t: (row(b, t), 2)),
            pl.BlockSpec((rows, LANES), lambda b, t: (row(b, t), 0)),
            pl.BlockSpec((rows // CHUNK, ML_GATE_ROWS, CHUNK), lambda b, t: (row(b, t), 0, 0)),
            pl.BlockSpec((1, ML_V_WIDTH), lambda b, t: (0, 0)),
        ],
        out_specs=pl.BlockSpec((rows, ML_V_WIDTH), lambda b, t: (row(b, t), 0)),
        out_shape=jax.ShapeDtypeStruct((tokens, ML_V_WIDTH), BF16),
        scratch_shapes=[
            pltpu.VMEM((ML_HEADS, ML_QK_DIM, ML_V_DIM), F32),
            pltpu.VMEM((ML_HEADS, 1, ML_QK_DIM), F32),
            pltpu.VMEM((ML_HEADS, 1, LANES), F32),
            pltpu.VMEM((ML_HEADS * rows // CHUNK, ML_QK_DIM, ML_V_DIM), BF16),
        ],
        compiler_params=_params("parallel", "arbitrary"),
        name="mlstm",
    )(proj, proj, proj, proj, gcol, grow, norm_w)


def _gdn_kernel(q_ref, k_ref, v_ref, z_ref, gcol_ref, grow_ref, nw_ref, out_ref,
                s_scr, wq_scr, attn_scr, kdt_scr, u_scr, eg_scr):
    rows = q_ref.shape[0]
    rep = GDN_V_HEADS // GDN_K_HEADS
    dh = GDN_HEAD_DIM
    n_kh = q_ref.shape[1] // dh
    n_vh = n_kh * rep
    n_ch = rows // CHUNK
    @pl.when(pl.program_id(1) == 0)
    def _():
        s_scr[...] = jnp.zeros_like(s_scr)

    ri = _iota((CHUNK, CHUNK), 0)
    ci = _iota((CHUNK, CHUNK), 1)
    causal = ri >= ci
    strict = ri > ci
    eye = jnp.where(ri == ci, 1.0, 0.0).astype(F32)
    lane = _iota((CHUNK, LANES), 1)
    merge_mask = {}
    s = 1
    while s < CHUNK:
        merge_mask[s] = (ri // s - ci // s == 1) & (ri // (2 * s) == ci // (2 * s))
        s *= 2

    kk, qk, qs, ks = {}, {}, {}, {}
    for ch in range(n_ch):
        rs = slice(ch * CHUNK, (ch + 1) * CHUNK)
        for kh in range(n_kh):
            hs = slice(kh * dh, (kh + 1) * dh)
            qb = q_ref[rs, hs]
            kb = k_ref[rs, hs]
            qs[ch, kh], ks[ch, kh] = qb.astype(F32), kb.astype(F32)
            kk[ch, kh] = _dot_nt(kb, kb)
            qk[ch, kh] = _dot_nt(qb, kb)

    items = [(ch, vh) for ch in range(n_ch) for vh in range(n_vh)]
    x, a_mats, rhs = {}, {}, {}
    for it, (ch, vh) in enumerate(items):
        rs = slice(ch * CHUNK, (ch + 1) * CHUNK)
        kh = vh // rep
        gc = gcol_ref[rs, :]
        beta_c = jnp.sum(jnp.where(lane == vh, gc, 0.0), axis=1, keepdims=True)
        g_c = jnp.sum(jnp.where(lane == GDN_V_HEADS + vh, gc, 0.0), axis=1, keepdims=True)
        g_r = grow_ref[ch, GDN_V_HEADS + vh:GDN_V_HEADS + vh + 1, :]
        g_last = g_r[:, CHUNK - 1:CHUNK]
        decay = jnp.exp(jnp.where(causal, g_c - g_r, -jnp.inf))
        a_mat = jnp.where(strict, kk[ch, kh] * beta_c * decay, 0.0)
        x[it] = eye - jnp.where(merge_mask[1], a_mat, 0.0)
        a_mats[it] = a_mat

        q, k = qs[ch, kh], ks[ch, kh]
        v = v_ref[rs, vh * dh:(vh + 1) * dh].astype(F32)
        e_g = jnp.exp(g_c)
        rhs[it] = jnp.concatenate([v * beta_c, k * (beta_c * e_g)], axis=1).astype(BF16)
        attn_scr[it] = (qk[ch, kh] * decay).astype(BF16)
        wq_scr[it, CHUNK:2 * CHUNK, :] = (q * e_g).astype(BF16)
        kdt_scr[it] = (k * jnp.exp(g_last - g_c)).T.astype(BF16)
        eg_scr[it] = jnp.broadcast_to(jnp.exp(g_last), (1, LANES))

    s = 2
    while s < CHUNK:
        for it in range(len(items)):
            nb = jnp.where(merge_mask[s], a_mats[it], 0.0).astype(BF16)
            y = _dot(nb, x[it].astype(BF16))
            x[it] = x[it] - _dot(x[it].astype(BF16), y.astype(BF16))
        s *= 2

    for it in range(len(items)):
        sol = _dot(x[it].astype(BF16), rhs[it])
        u_scr[it] = sol[:, :dh]
        wq_scr[it, 0:CHUNK, :] = sol[:, dh:].astype(BF16)

    state = [s_scr[vh] for vh in range(n_vh)]
    for ch in range(n_ch):
        rs = slice(ch * CHUNK, (ch + 1) * CHUNK)
        base = ch * n_vh
        sb = [state[vh].astype(BF16) for vh in range(n_vh)]
        r = [_dot(wq_scr[base + vh], sb[vh]) for vh in range(n_vh)]
        vnb = [(u_scr[base + vh] - r[vh][:CHUNK]).astype(BF16) for vh in range(n_vh)]
        state = [state[vh] * eg_scr[base + vh] + _dot(kdt_scr[base + vh], vnb[vh]) for vh in range(n_vh)]
        for vh in range(n_vh):
            out = r[vh][CHUNK:] + _dot(attn_scr[base + vh], vnb[vh])
            ms = jnp.mean(out * out, axis=1, keepdims=True)
            z = z_ref[rs, vh * dh:(vh + 1) * dh].astype(F32)
            gated = out * lax.rsqrt(ms + NORM_EPS) * nw_ref[...] * (z * _sigmoid(z))
            out_ref[rs, vh * dh:(vh + 1) * dh] = gated.astype(out_ref.dtype)
    for vh in range(n_vh):
        s_scr[vh] = state[vh]


def _gdn(proj, gcol, grow, norm_w, *, batch, seq):
    tokens = proj.shape[0]
    rows = GDN_ROWS
    steps = seq // rows
    n_items = GDN_V_HEADS * rows // CHUNK
    row = lambda b, t: b * steps + t
    return pl.pallas_call(
        _gdn_kernel,
        grid=(batch, steps),
        in_specs=[
            pl.BlockSpec((rows, GDN_KEY_WIDTH), lambda b, t: (row(b, t), 0)),
            pl.BlockSpec((rows, GDN_KEY_WIDTH), lambda b, t: (row(b, t), 1)),
            pl.BlockSpec((rows, GDN_VAL_WIDTH), lambda b, t: (row(b, t), 2 * GDN_KEY_WIDTH // GDN_VAL_WIDTH)),
            pl.BlockSpec((rows, GDN_VAL_WIDTH), lambda b, t: (row(b, t), GDN_CONV_DIM // GDN_VAL_WIDTH)),
            pl.BlockSpec((rows, LANES), lambda b, t: (row(b, t), 0)),
            pl.BlockSpec((rows // CHUNK, GDN_GATE_ROWS, CHUNK), lambda b, t: (row(b, t), 0, 0)),
            pl.BlockSpec((1, GDN_HEAD_DIM), lambda b, t: (0, 0)),
        ],
        out_specs=pl.BlockSpec((rows, GDN_VAL_WIDTH), lambda b, t: (row(b, t), 0)),
        out_shape=jax.ShapeDtypeStruct((tokens, GDN_VAL_WIDTH), BF16),
        scratch_shapes=[
            pltpu.VMEM((GDN_V_HEADS, GDN_HEAD_DIM, GDN_HEAD_DIM), F32),
            pltpu.VMEM((n_items, 2 * CHUNK, GDN_HEAD_DIM), BF16),
            pltpu.VMEM((n_items, CHUNK, CHUNK), BF16),
            pltpu.VMEM((n_items, GDN_HEAD_DIM, CHUNK), BF16),
            pltpu.VMEM((n_items, CHUNK, GDN_HEAD_DIM), F32),
            pltpu.VMEM((n_items, 1, LANES), F32),
        ],
        compiler_params=_params("parallel", "arbitrary"),
        name="gdn",
    )(proj, proj, proj, proj, gcol, grow, norm_w)


def _block_tail_kernel(a_ref, wo_ref, x_ref, g1_ref, sc_ref, sh_ref, g2_ref, wgu_ref, wd_ref, fw_ref, o_ref,
                       x1_scr, h_scr, act_scr, *, final_norm):
    d_ff = wd_ref.shape[0]
    x1 = x_ref[...] + g1_ref[0] * _dot(a_ref[...], wo_ref[...])
    x1_scr[...] = x1
    h_scr[...] = _modulated_norm(x1, sc_ref[0], sh_ref[0]).astype(BF16)
    for j in range(d_ff // FFN_TH):
        gate = _dot(h_scr[...], wgu_ref[:, j * FFN_TH:(j + 1) * FFN_TH])
        up = _dot(h_scr[...], wgu_ref[:, d_ff + j * FFN_TH:d_ff + (j + 1) * FFN_TH])
        act_scr[:, j * FFN_TH:(j + 1) * FFN_TH] = (gate * _sigmoid(gate) * up).astype(BF16)
    y = x1_scr[...] + g2_ref[0] * _dot(act_scr[...], wd_ref[...])
    if final_norm:
        ms = jnp.mean(y * y, axis=-1, keepdims=True)
        y = y * lax.rsqrt(ms + NORM_EPS) * fw_ref[...]
    o_ref[...] = y


def _block_tail(mixed, w_out, x2, g1, scale, shift, g2, w_gate_up, w_down, final_w, *, seq, final_norm):
    tokens, d = x2.shape
    kdim = mixed.shape[1]
    d_ff = w_down.shape[0]
    tm = FFN_TM
    per_batch = seq // tm
    mod_spec = pl.BlockSpec((1, 1, d), lambda i: (i // per_batch, 0, 0))
    resident = functools.partial(pl.BlockSpec, index_map=lambda i: (0, 0), pipeline_mode=pl.Buffered(1))
    return pl.pallas_call(
        functools.partial(_block_tail_kernel, final_norm=final_norm),
        grid=(tokens // tm,),
        in_specs=[
            pl.BlockSpec((tm, kdim), lambda i: (i, 0)),
            resident((kdim, d)),
            pl.BlockSpec((tm, d), lambda i: (i, 0)),
            mod_spec, mod_spec, mod_spec, mod_spec,
            resident((d, 2 * d_ff)),
            resident((d_ff, d)),
            resident((1, d)),
        ],
        out_specs=pl.BlockSpec((tm, d), lambda i: (i, 0)),
        out_shape=jax.ShapeDtypeStruct((tokens, d), F32),
        scratch_shapes=[pltpu.VMEM((tm, d), F32), pltpu.VMEM((tm, d), BF16), pltpu.VMEM((tm, d_ff), BF16)],
        compiler_params=_params("parallel"),
        name="block_tail",
    )(mixed, w_out, x2, g1, scale, shift, g2, w_gate_up, w_down, final_w)


def _gate_weights(w_in, start, n_rows):
    wg = w_in[:, start:]
    count = wg.shape[1]
    col = jnp.pad(wg, ((0, 0), (0, LANES - count))).astype(BF16)
    row = jnp.pad(wg.T, ((0, n_rows - count), (0, 0))).astype(BF16)
    return col, row


def _gate_vec(vec, offset, n_rows):
    count = vec.shape[0]
    col = jnp.zeros((1, LANES), F32).at[0, offset:offset + count].set(vec.astype(F32))
    row = jnp.zeros((n_rows, 1), F32).at[offset:offset + count, 0].set(vec.astype(F32))
    return col, row


def kernel(x, c, ada_w, ada_b, ml_w_in, ml_b_if, ml_norm_w, ml_w_out, gdn_w_in, gdn_conv_w, gdn_a_log,
           gdn_dt_bias, gdn_norm_w, gdn_w_out, ffn_w_gate_up, ffn_w_down, final_norm_w):
    batch, seq, d = x.shape
    depth = ada_w.shape[0]
    n_mixers = 2
    tokens = batch * seq
    x2 = x.reshape(tokens, d)

    c_rows = 8
    c_pad = jnp.pad(c, ((0, c_rows - batch), (0, 0)))
    mod = _adaln(c_pad, ada_w, ada_b)[:, :batch, :]

    for layer in range(depth):
        sh1, sc1, g1, sh2, sc2, g2 = [m.reshape(batch, 1, d) for m in jnp.split(mod[layer], 6, axis=-1)]
        jdx = layer // n_mixers
        if layer % n_mixers == 0:
            w_in = ml_w_in[jdx]
            wg_col, wg_row = _gate_weights(w_in, ML_MAIN, ML_GATE_ROWS)
            b_col, b_row = _gate_vec(ml_b_if[jdx], 0, ML_GATE_ROWS)
            proj, gcol, grow = _inproj(x2, sc1, sh1, w_in.astype(BF16), wg_col, wg_row,
                                       b_col, b_row, mode="ml", seq=seq)
            mixed = _mlstm(proj, gcol, grow, ml_norm_w[jdx].reshape(1, ML_V_WIDTH).astype(F32),
                           batch=batch, seq=seq)
            w_out = ml_w_out[jdx]
        else:
            w_in = gdn_w_in[jdx]
            wg_col, wg_row = _gate_weights(w_in, GDN_MAIN, GDN_GATE_ROWS)
            b_col, b_row = _gate_vec(gdn_dt_bias[jdx], GDN_V_HEADS, GDN_GATE_ROWS)
            a_col, a_row = _gate_vec(gdn_a_log[jdx], GDN_V_HEADS, GDN_GATE_ROWS)
            proj, gcol, grow = _inproj(x2, sc1, sh1, w_in.astype(BF16), wg_col, wg_row,
                                       b_col, b_row, (a_col, a_row, gdn_conv_w[jdx].astype(F32)),
                                       mode="gdn", seq=seq)
            mixed = _gdn(proj, gcol, grow, gdn_norm_w[jdx].reshape(1, GDN_HEAD_DIM).astype(F32),
                         batch=batch, seq=seq)
            w_out = gdn_w_out[jdx]
        x2 = _block_tail(mixed, w_out.astype(BF16), x2, g1, sc2, sh2, g2,
                         ffn_w_gate_up[layer].astype(BF16), ffn_w_down[layer].astype(BF16),
                         final_norm_w.reshape(1, d).astype(F32), seq=seq, final_norm=(layer == depth - 1))
    return x2.reshape(batch, seq, d)
```

```python
import functools

import jax
import jax.numpy as jnp
from jax import lax
from jax.experimental import pallas as pl
from jax.experimental.pallas import tpu as pltpu

F32 = jnp.float32
BF16 = jnp.bfloat16
HIGHEST = lax.Precision.HIGHEST

LANES = 128
VMEM_LIMIT = 56 * 1024 * 1024

D_MODEL = 1024
CHUNK = 64
NORM_EPS = 1e-6

ML_HEADS = 4
ML_QK_DIM = 128
ML_V_DIM = 256
ML_QK_WIDTH = ML_HEADS * ML_QK_DIM
ML_V_WIDTH = ML_HEADS * ML_V_DIM
ML_MAIN = 2 * ML_QK_WIDTH + 2 * ML_V_WIDTH
ML_GATE_ROWS = 16
ML_GATE_CAP = 15.0

GDN_K_HEADS = 8
GDN_V_HEADS = 16
GDN_HEAD_DIM = 128
GDN_KEY_WIDTH = GDN_K_HEADS * GDN_HEAD_DIM
GDN_VAL_WIDTH = GDN_V_HEADS * GDN_HEAD_DIM
GDN_CONV_DIM = 2 * GDN_KEY_WIDTH + GDN_VAL_WIDTH
GDN_CONV_WIDTH = 4
GDN_MAIN = GDN_CONV_DIM + GDN_VAL_WIDTH
GDN_GATE_ROWS = 2 * GDN_V_HEADS

WEIGHT_PREP_ROWS = 256
ADALN_TN = 2048
INPROJ_TM = 512
INPROJ_TN = 256
CONV_HALO = 8
PIPE_PIECES = 4
REC_ROWS = 256
GDN_ROWS = 128
FFN_TM = 512
FFN_TH = 256


def _params(*sem):
    return pltpu.CompilerParams(dimension_semantics=sem, vmem_limit_bytes=VMEM_LIMIT)


def _sigmoid(x):
    return 1.0 / (1.0 + jnp.exp(-x))


def _softplus(x):
    return jnp.maximum(x, 0.0) + jnp.log1p(jnp.exp(-jnp.abs(x)))


def _dot(a, b):
    return jnp.dot(a, b, preferred_element_type=F32)


def _dot_nt(a, b):
    return lax.dot_general(a, b, (((1,), (1,)), ((), ())), preferred_element_type=F32)


def _dot_tn(a, b):
    return lax.dot_general(a, b, (((0,), (0,)), ((), ())), preferred_element_type=F32)


def _iota(shape, dim):
    return lax.broadcasted_iota(jnp.int32, shape, dim)


def _adaln_kernel(c_ref, w_ref, b_ref, o_ref):
    c = c_ref[...]
    act = c * _sigmoid(c)
    o_ref[0] = jnp.dot(act, w_ref[0], precision=HIGHEST, preferred_element_type=F32) + b_ref[0]


def _adaln(c_pad, ada_w, ada_b):
    depth, d, n = ada_w.shape
    rows = c_pad.shape[0]
    tn = ADALN_TN
    return pl.pallas_call(
        _adaln_kernel,
        grid=(depth, n // tn),
        in_specs=[
            pl.BlockSpec((rows, d), lambda l, j: (0, 0)),
            pl.BlockSpec((1, d, tn), lambda l, j: (l, 0, j)),
            pl.BlockSpec((1, 1, tn), lambda l, j: (l, 0, j)),
        ],
        out_specs=pl.BlockSpec((1, rows, tn), lambda l, j: (l, 0, j)),
        out_shape=jax.ShapeDtypeStruct((depth, rows, n), F32),
        compiler_params=_params("parallel", "parallel"),
        name="adaln",
    )(c_pad, ada_w, ada_b.reshape(depth, 1, n))


def _modulated_norm(x, scale, shift):
    ms = jnp.mean(x * x, axis=-1, keepdims=True)
    return x * lax.rsqrt(ms + NORM_EPS) * (1.0 + scale) + shift


def _ml_gates(pre, idx):
    capped = ML_GATE_CAP * jnp.tanh(pre / ML_GATE_CAP)
    log_f = jnp.minimum(capped, 0.0) - jnp.log1p(jnp.exp(-jnp.abs(capped)))
    is_f = idx >= ML_HEADS
    return jnp.where(is_f, log_f, capped), is_f


def _gdn_gates(pre, a_log, idx):
    is_g = idx >= GDN_V_HEADS
    return jnp.where(is_g, -jnp.exp(a_log) * _softplus(pre), _sigmoid(pre)), is_g


def _inproj_kernel(*refs, mode, per_batch):
    if mode == "ml":
        (x_ref, sc_ref, sh_ref, w_ref, bcol_ref, brow_ref,
         proj_ref, gcol_ref, grow_ref, h_scr) = refs
    else:
        (x_ref, sc_ref, sh_ref, w_ref, bcol_ref, brow_ref, acol_ref, arow_ref, cw_ref,
         proj_ref, gcol_ref, grow_ref, h_scr, halo_scr, conv_buf) = refs
    tm = x_ref.shape[0]
    n = proj_ref.shape[1]
    n_rows = brow_ref.shape[0]
    tn = INPROJ_TN

    h_scr[...] = _modulated_norm(x_ref[...], sc_ref[0], sh_ref[0]).astype(BF16)

    def gate_stages():
        chunks = [slice(ch * CHUNK, (ch + 1) * CHUNK) for ch in range(tm // CHUNK)]
        lane = _iota((CHUNK, LANES), 1)
        row = _iota((n_rows, CHUNK), 0)
        rr = _iota((CHUNK, CHUNK), 0)
        cc = _iota((CHUNK, CHUNK), 1)
        lower = jnp.where(cc <= rr, 1.0, 0.0).astype(F32)
        upper = jnp.where(rr <= cc, 1.0, 0.0).astype(F32)
        wg = w_ref[:, n:n + LANES]
        wg_t = wg.astype(F32).T[:n_rows, :].astype(BF16)
        pre = _dot(h_scr[...], wg) + bcol_ref[...]
        pre_t = [_dot_nt(wg_t, h_scr[rs, :]) + brow_ref[...] for rs in chunks]
        yield
        if mode == "ml":
            col = [_ml_gates(pre[rs, :], lane) for rs in chunks]
        else:
            col = [_gdn_gates(pre[rs, :], acol_ref[...], lane) for rs in chunks]
        yield
        if mode == "ml":
            rowg = [_ml_gates(p_t, row) for p_t in pre_t]
        else:
            rowg = [_gdn_gates(p_t, arow_ref[...], row) for p_t in pre_t]
        yield
        csum = [jnp.dot(lower, vals, precision=HIGHEST, preferred_element_type=F32) for vals, _ in col]
        csum_t = [jnp.dot(vals_t, upper, precision=HIGHEST, preferred_element_type=F32) for vals_t, _ in rowg]
        yield
        for ch, rs in enumerate(chunks):
            gcol_ref[rs, :] = jnp.where(col[ch][1], csum[ch], col[ch][0])
            grow_ref[ch] = jnp.where(rowg[ch][1], csum_t[ch], rowg[ch][0])
        yield

    stages = gate_stages()
    order = list(range(n // tn))
    if mode == "gdn":
        first_of_seq = pl.program_id(0) % per_batch == 0
        conv_slabs = [j for j in order if j * tn < GDN_CONV_DIM]
        plain_slabs = [j for j in order if j * tn >= GDN_CONV_DIM]
        order = []
        while conv_slabs or plain_slabs:
            order += conv_slabs[:2] + plain_slabs[:1]
            conv_slabs, plain_slabs = conv_slabs[2:], plain_slabs[1:]
    d = x_ref.shape[1]
    kp, rp = d // PIPE_PIECES, tm // PIPE_PIECES

    def slab_cols(j):
        return slice(j * tn, (j + 1) * tn)

    def matmul_piece(j, kk):
        return _dot(h_scr[:, kk * kp:(kk + 1) * kp], w_ref[kk * kp:(kk + 1) * kp, slab_cols(j)])

    def start_epilogue(j, y):
        if mode == "gdn" and j * tn < GDN_CONV_DIM:
            halo_ref = halo_scr.at[:, slab_cols(j)]
            conv_buf[0:CONV_HALO, :] = jnp.where(first_of_seq, 0.0, halo_ref[...])
            conv_buf[CONV_HALO:CONV_HALO + tm, :] = y
            halo_ref[...] = y[tm - CONV_HALO:tm, :]

    def epilogue_piece(j, y, r):
        cs, rows = slab_cols(j), slice(r * rp, (r + 1) * rp)
        if not (mode == "gdn" and j * tn < GDN_CONV_DIM):
            proj_ref[rows, cs] = y[rows, :].astype(proj_ref.dtype)
            return
        conv_w = cw_ref[:, cs]
        acc = conv_w[GDN_CONV_WIDTH - 1:GDN_CONV_WIDTH, :] * y[rows, :]
        for tap in range(GDN_CONV_WIDTH - 1):
            first = CONV_HALO - (GDN_CONV_WIDTH - 1 - tap) + r * rp
            acc = acc + conv_w[tap:tap + 1, :] * conv_buf[first:first + rp, :]
        acc = acc * _sigmoid(acc)
        if j * tn >= 2 * GDN_KEY_WIDTH:
            proj_ref[rows, cs] = acc.astype(proj_ref.dtype)
            return
        q_scale = GDN_HEAD_DIM ** -0.5 if j * tn < GDN_KEY_WIDTH else 1.0
        for hh in range(tn // GDN_HEAD_DIM):
            hs = slice(hh * GDN_HEAD_DIM, (hh + 1) * GDN_HEAD_DIM)
            blk = acc[:, hs]
            ss = jnp.sum(blk * blk, axis=1, keepdims=True)
            proj_ref[rows, j * tn + hh * GDN_HEAD_DIM:j * tn + (hh + 1) * GDN_HEAD_DIM] = (
                blk * (lax.rsqrt(ss + NORM_EPS) * q_scale)).astype(proj_ref.dtype)

    y = _dot(h_scr[...], w_ref[:, slab_cols(order[0])])
    for idx, j in enumerate(order):
        next(stages, None)
        start_epilogue(j, y)
        following = order[idx + 1] if idx + 1 < len(order) else None
        y_next = None
        for piece in range(PIPE_PIECES):
            if following is not None:
                part = matmul_piece(following, piece)
                y_next = part if y_next is None else y_next + part
            epilogue_piece(j, y, piece)
        y = y_next
    for _ in stages:
        pass


def _inproj(x2, scale, shift, w_all, bias_col, bias_row, gdn_extra=None, *, mode, layer, seq):
    tokens, d = x2.shape
    n = ML_MAIN if mode == "ml" else GDN_MAIN
    n_rows = bias_row.shape[0]
    tm = INPROJ_TM
    per_batch = seq // tm
    const = lambda i: (0, 0)
    resident = functools.partial(pl.BlockSpec, index_map=const, pipeline_mode=pl.Buffered(1))
    in_specs = [
        pl.BlockSpec((tm, d), lambda i: (i, 0)),
        pl.BlockSpec((1, 1, d), lambda i: (i // per_batch, 0, 0)),
        pl.BlockSpec((1, 1, d), lambda i: (i // per_batch, 0, 0)),
        pl.BlockSpec((None,) + w_all.shape[1:], lambda i: (layer, 0, 0), pipeline_mode=pl.Buffered(1)),
        resident((1, LANES)),
        resident((n_rows, 1)),
    ]
    operands = [x2, scale, shift, w_all, bias_col, bias_row]
    scratch = [pltpu.VMEM((tm, d), BF16)]
    if mode == "gdn":
        alog_col, alog_row, conv_w = gdn_extra
        in_specs += [resident((1, LANES)), resident((n_rows, 1)), resident((GDN_CONV_WIDTH, GDN_CONV_DIM))]
        operands += [alog_col, alog_row, conv_w]
        scratch += [pltpu.VMEM((CONV_HALO, GDN_CONV_DIM), F32),
                    pltpu.VMEM((tm + CONV_HALO, INPROJ_TN), F32)]
    return pl.pallas_call(
        functools.partial(_inproj_kernel, mode=mode, per_batch=per_batch),
        grid=(tokens // tm,),
        in_specs=in_specs,
        out_specs=[
            pl.BlockSpec((tm, n), lambda i: (i, 0)),
            pl.BlockSpec((tm, LANES), lambda i: (i, 0)),
            pl.BlockSpec((tm // CHUNK, n_rows, CHUNK), lambda i: (i, 0, 0)),
        ],
        out_shape=[
            jax.ShapeDtypeStruct((tokens, n), BF16),
            jax.ShapeDtypeStruct((tokens, LANES), F32),
            jax.ShapeDtypeStruct((tokens // CHUNK, n_rows, CHUNK), F32),
        ],
        scratch_shapes=scratch,
        compiler_params=_params("arbitrary"),
        name="inproj_" + mode,
    )(*operands)


def _mlstm_kernel(q_ref, k_ref, v_ref, o_ref, gcol_ref, grow_ref, nw_ref, out_ref,
                  c_scr, n_scr, m_scr, cprev_scr):
    rows = q_ref.shape[0]

    @pl.when(pl.program_id(1) == 0)
    def _():
        c_scr[...] = jnp.zeros_like(c_scr)
        n_scr[...] = jnp.zeros_like(n_scr)
        m_scr[...] = jnp.zeros_like(m_scr)

    causal = _iota((CHUNK, CHUNK), 0) >= _iota((CHUNK, CHUNK), 1)
    k_scale = ML_QK_DIM ** -0.5
    n_ch = rows // CHUNK
    items = [(ch, h) for ch in range(n_ch) for h in range(ML_HEADS)]

    def row_slice(ch):
        return slice(ch * CHUNK, (ch + 1) * CHUNK)

    def qk_slice(h):
        return slice(h * ML_QK_DIM, (h + 1) * ML_QK_DIM)

    def v_slice(h):
        return slice(h * ML_V_DIM, (h + 1) * ML_V_DIM)

    n_items = len(items)

    lane = _iota((CHUNK, LANES), 1)

    def lane_rep(col):
        return jnp.broadcast_to(col, (CHUNK, LANES))

    def wide(a):
        return jnp.concatenate([a] * (ML_V_DIM // LANES), axis=1)

    b_col, b_last, d_mat, g_end, m_intra, m_loc, k_w, d_c, d_n, scores = ({} for _ in range(10))
    for it, (ch, h) in enumerate(items):
        rs = row_slice(ch)
        gc = gcol_ref[rs, :]
        gr = grow_ref[ch]
        li_c = lane_rep(jnp.sum(jnp.where(lane == h, gc, 0.0), axis=1, keepdims=True))
        b_c = lane_rep(jnp.sum(jnp.where(lane == ML_HEADS + h, gc, 0.0), axis=1, keepdims=True))
        li_r = gr[h:h + 1, :]
        b_r = gr[ML_HEADS + h:ML_HEADS + h + 1, :]
        b_col[it], b_last[it] = b_c, b_c[CHUNK - 1:CHUNK, :]
        d_mat[it] = jnp.where(causal, b_c[:, :CHUNK] - b_r + li_r, -jnp.inf)
        g_end[it] = b_last[it] - b_c + li_c
    for it in range(n_items):
        m_intra[it] = jnp.max(d_mat[it], axis=1, keepdims=True)
        m_loc[it] = jnp.max(g_end[it], axis=0, keepdims=True)
    for it, (ch, h) in enumerate(items):
        k_w[it] = (k_ref[row_slice(ch), qk_slice(h)].astype(F32)
                   * (k_scale * jnp.exp(g_end[it] - m_loc[it])))
    for it, (ch, h) in enumerate(items):
        rs = row_slice(ch)
        d_c[it] = _dot_tn(k_w[it].astype(BF16), v_ref[rs, v_slice(h)])
        d_n[it] = jnp.sum(k_w[it], axis=0, keepdims=True)
        scores[it] = _dot_nt(q_ref[rs, qk_slice(h)], k_ref[rs, qk_slice(h)]) * k_scale

    m_prev, n_prev = {}, {}
    m_run = [m_scr[h] for h in range(ML_HEADS)]
    n_run = [n_scr[h] for h in range(ML_HEADS)]
    for it, (ch, h) in enumerate(items):
        c_run = c_scr[h]
        cprev_scr[it] = c_run.astype(BF16)
        m_prev[it], n_prev[it] = m_run[h], n_run[h]
        m_new = jnp.maximum(b_last[it] + m_run[h], m_loc[it])
        a_old = jnp.exp(b_last[it] + m_run[h] - m_new)
        a_new = jnp.exp(m_loc[it] - m_new)
        c_scr[h] = wide(a_old) * c_run + wide(a_new) * d_c[it]
        n_run[h] = a_old * n_run[h] + a_new * d_n[it]
        m_run[h] = m_new
    for h in range(ML_HEADS):
        n_scr[h] = n_run[h]
        m_scr[h] = m_run[h]

    m_t, s_inter, p, num, den, hid, ms = ({} for _ in range(7))
    for it in range(n_items):
        m_inter = b_col[it] + m_prev[it]
        m_t[it] = jnp.maximum(m_inter, m_intra[it])
        s_inter[it] = jnp.exp(m_inter - m_t[it])
        p[it] = jnp.exp(d_mat[it] - m_t[it][:, :CHUNK]) * scores[it]
    for it, (ch, h) in enumerate(items):
        rs = row_slice(ch)
        q = q_ref[rs, qk_slice(h)]
        num[it] = (wide(s_inter[it]) * _dot(q, cprev_scr[it])
                   + _dot(p[it].astype(BF16), v_ref[rs, v_slice(h)]))
        den[it] = (s_inter[it] * jnp.sum(q.astype(F32) * n_prev[it], axis=1, keepdims=True)
                   + jnp.sum(p[it], axis=1, keepdims=True))
    for it in range(n_items):
        hid[it] = num[it] * wide(1.0 / jnp.maximum(jnp.abs(den[it]), jnp.exp(-m_t[it])))
        ms[it] = jnp.mean(hid[it] * hid[it], axis=1, keepdims=True)
    for it, (ch, h) in enumerate(items):
        rs, vs = row_slice(ch), v_slice(h)
        out = hid[it] * lax.rsqrt(ms[it] + NORM_EPS) * nw_ref[:, vs] * _sigmoid(o_ref[rs, vs].astype(F32))
        out_ref[rs, vs] = out.astype(out_ref.dtype)


def _mlstm(proj, gcol, grow, norm_w, *, batch, seq):
    tokens = proj.shape[0]
    rows = REC_ROWS
    steps = seq // rows
    row = lambda b, t: b * steps + t
    return pl.pallas_call(
        _mlstm_kernel,
        grid=(batch, steps),
        in_specs=[
            pl.BlockSpec((rows, ML_QK_WIDTH), lambda b, t: (row(b, t), 0)),
            pl.BlockSpec((rows, ML_QK_WIDTH), lambda b, t: (row(b, t), 1)),
            pl.BlockSpec((rows, ML_V_WIDTH), lambda b, t: (row(b, t), 1)),
            pl.BlockSpec((rows, ML_V_WIDTH), lambda b, t: (row(b, t), 2)),
            pl.BlockSpec((rows, LANES), lambda b, t: (row(b, t), 0)),
            pl.BlockSpec((rows // CHUNK, ML_GATE_ROWS, CHUNK), lambda b, t: (row(b, t), 0, 0)),
            pl.BlockSpec((1, ML_V_WIDTH), lambda b, t: (0, 0)),
        ],
        out_specs=pl.BlockSpec((rows, ML_V_WIDTH), lambda b, t: (row(b, t), 0)),
        out_shape=jax.ShapeDtypeStruct((tokens, ML_V_WIDTH), BF16),
        scratch_shapes=[
            pltpu.VMEM((ML_HEADS, ML_QK_DIM, ML_V_DIM), F32),
            pltpu.VMEM((ML_HEADS, 1, ML_QK_DIM), F32),
            pltpu.VMEM((ML_HEADS, 1, LANES), F32),
            pltpu.VMEM((ML_HEADS * rows // CHUNK, ML_QK_DIM, ML_V_DIM), BF16),
        ],
        compiler_params=_params("parallel", "arbitrary"),
        name="mlstm",
    )(proj, proj, proj, proj, gcol, grow, norm_w)


def _gdn_kernel(q_ref, k_ref, v_ref, z_ref, gcol_ref, grow_ref, nw_ref, out_ref,
                s_scr, wq_scr, attn_scr, kdt_scr, u_scr, eg_scr):
    rows = q_ref.shape[0]
    rep = GDN_V_HEADS // GDN_K_HEADS
    dh = GDN_HEAD_DIM
    n_kh = q_ref.shape[1] // dh
    n_vh = n_kh * rep
    n_ch = rows // CHUNK
    @pl.when(pl.program_id(1) == 0)
    def _():
        s_scr[...] = jnp.zeros_like(s_scr)

    ri = _iota((CHUNK, CHUNK), 0)
    ci = _iota((CHUNK, CHUNK), 1)
    causal = ri >= ci
    strict = ri > ci
    eye = jnp.where(ri == ci, 1.0, 0.0).astype(F32)
    lane = _iota((CHUNK, LANES), 1)
    merge_mask = {}
    s = 1
    while s < CHUNK:
        merge_mask[s] = (ri // s - ci // s == 1) & (ri // (2 * s) == ci // (2 * s))
        s *= 2

    kk, qk, qs, ks = {}, {}, {}, {}
    for ch in range(n_ch):
        rs = slice(ch * CHUNK, (ch + 1) * CHUNK)
        for kh in range(n_kh):
            hs = slice(kh * dh, (kh + 1) * dh)
            qb = q_ref[rs, hs]
            kb = k_ref[rs, hs]
            qs[ch, kh], ks[ch, kh] = qb.astype(F32), kb.astype(F32)
            kk[ch, kh] = _dot_nt(kb, kb)
            qk[ch, kh] = _dot_nt(qb, kb)

    items = [(ch, vh) for ch in range(n_ch) for vh in range(n_vh)]
    x, a_mats, rhs = {}, {}, {}
    for it, (ch, vh) in enumerate(items):
        rs = slice(ch * CHUNK, (ch + 1) * CHUNK)
        kh = vh // rep
        gc = gcol_ref[rs, :]
        beta_c = jnp.sum(jnp.where(lane == vh, gc, 0.0), axis=1, keepdims=True)
        g_c = jnp.sum(jnp.where(lane == GDN_V_HEADS + vh, gc, 0.0), axis=1, keepdims=True)
        g_r = grow_ref[ch, GDN_V_HEADS + vh:GDN_V_HEADS + vh + 1, :]
        g_last = g_r[:, CHUNK - 1:CHUNK]
        decay = jnp.exp(jnp.where(causal, g_c - g_r, -jnp.inf))
        a_mat = jnp.where(strict, kk[ch, kh] * beta_c * decay, 0.0)
        x[it] = eye - jnp.where(merge_mask[1], a_mat, 0.0)
        a_mats[it] = a_mat

        q, k = qs[ch, kh], ks[ch, kh]
        v = v_ref[rs, vh * dh:(vh + 1) * dh].astype(F32)
        e_g = jnp.exp(g_c)
        rhs[it] = jnp.concatenate([v * beta_c, k * (beta_c * e_g)], axis=1).astype(BF16)
        attn_scr[it] = (qk[ch, kh] * decay).astype(BF16)
        wq_scr[it, CHUNK:2 * CHUNK, :] = (q * e_g).astype(BF16)
        kdt_scr[it] = (k * jnp.exp(g_last - g_c)).T.astype(BF16)
        eg_scr[it] = jnp.broadcast_to(jnp.exp(g_last), (1, LANES))

    s = 2
    while s < CHUNK:
        for it in range(len(items)):
            nb = jnp.where(merge_mask[s], a_mats[it], 0.0).astype(BF16)
            y = _dot(nb, x[it].astype(BF16))
            x[it] = x[it] - _dot(x[it].astype(BF16), y.astype(BF16))
        s *= 2

    for it in range(len(items)):
        sol = _dot(x[it].astype(BF16), rhs[it])
        u_scr[it] = sol[:, :dh]
        wq_scr[it, 0:CHUNK, :] = sol[:, dh:].astype(BF16)

    state = [s_scr[vh] for vh in range(n_vh)]
    for ch in range(n_ch):
        rs = slice(ch * CHUNK, (ch + 1) * CHUNK)
        base = ch * n_vh
        sb = [state[vh].astype(BF16) for vh in range(n_vh)]
        r = [_dot(wq_scr[base + vh], sb[vh]) for vh in range(n_vh)]
        vnb = [(u_scr[base + vh] - r[vh][:CHUNK]).astype(BF16) for vh in range(n_vh)]
        state = [state[vh] * eg_scr[base + vh] + _dot(kdt_scr[base + vh], vnb[vh]) for vh in range(n_vh)]
        for vh in range(n_vh):
            out = r[vh][CHUNK:] + _dot(attn_scr[base + vh], vnb[vh])
            ms = jnp.mean(out * out, axis=1, keepdims=True)
            z = z_ref[rs, vh * dh:(vh + 1) * dh].astype(F32)
            gated = out * lax.rsqrt(ms + NORM_EPS) * nw_ref[...] * (z * _sigmoid(z))
            out_ref[rs, vh * dh:(vh + 1) * dh] = gated.astype(out_ref.dtype)
    for vh in range(n_vh):
        s_scr[vh] = state[vh]


def _gdn(proj, gcol, grow, norm_w, *, batch, seq):
    tokens = proj.shape[0]
    rows = GDN_ROWS
    steps = seq // rows
    n_items = GDN_V_HEADS * rows // CHUNK
    row = lambda b, t: b * steps + t
    return pl.pallas_call(
        _gdn_kernel,
        grid=(batch, steps),
        in_specs=[
            pl.BlockSpec((rows, GDN_KEY_WIDTH), lambda b, t: (row(b, t), 0)),
            pl.BlockSpec((rows, GDN_KEY_WIDTH), lambda b, t: (row(b, t), 1)),
            pl.BlockSpec((rows, GDN_VAL_WIDTH), lambda b, t: (row(b, t), 2 * GDN_KEY_WIDTH // GDN_VAL_WIDTH)),
            pl.BlockSpec((rows, GDN_VAL_WIDTH), lambda b, t: (row(b, t), GDN_CONV_DIM // GDN_VAL_WIDTH)),
            pl.BlockSpec((rows, LANES), lambda b, t: (row(b, t), 0)),
            pl.BlockSpec((rows // CHUNK, GDN_GATE_ROWS, CHUNK), lambda b, t: (row(b, t), 0, 0)),
            pl.BlockSpec((1, GDN_HEAD_DIM), lambda b, t: (0, 0)),
        ],
        out_specs=pl.BlockSpec((rows, GDN_VAL_WIDTH), lambda b, t: (row(b, t), 0)),
        out_shape=jax.ShapeDtypeStruct((tokens, GDN_VAL_WIDTH), BF16),
        scratch_shapes=[
            pltpu.VMEM((GDN_V_HEADS, GDN_HEAD_DIM, GDN_HEAD_DIM), F32),
            pltpu.VMEM((n_items, 2 * CHUNK, GDN_HEAD_DIM), BF16),
            pltpu.VMEM((n_items, CHUNK, CHUNK), BF16),
            pltpu.VMEM((n_items, GDN_HEAD_DIM, CHUNK), BF16),
            pltpu.VMEM((n_items, CHUNK, GDN_HEAD_DIM), F32),
            pltpu.VMEM((n_items, 1, LANES), F32),
        ],
        compiler_params=_params("parallel", "arbitrary"),
        name="gdn",
    )(proj, proj, proj, proj, gcol, grow, norm_w)


def _block_tail_kernel(a_ref, wo_ref, x_ref, g1_ref, sc_ref, sh_ref, g2_ref, wgu_ref, wd_ref, fw_ref, o_ref,
                       x1_scr, h_scr, act_scr, *, final_norm):
    d_ff = wd_ref.shape[0]
    x1 = x_ref[...] + g1_ref[0] * _dot(a_ref[...], wo_ref[...])
    x1_scr[...] = x1
    h_scr[...] = _modulated_norm(x1, sc_ref[0], sh_ref[0]).astype(BF16)
    for j in range(d_ff // FFN_TH):
        gate = _dot(h_scr[...], wgu_ref[:, j * FFN_TH:(j + 1) * FFN_TH])
        up = _dot(h_scr[...], wgu_ref[:, d_ff + j * FFN_TH:d_ff + (j + 1) * FFN_TH])
        act_scr[:, j * FFN_TH:(j + 1) * FFN_TH] = (gate * _sigmoid(gate) * up).astype(BF16)
    y = x1_scr[...] + g2_ref[0] * _dot(act_scr[...], wd_ref[...])
    if final_norm:
        ms = jnp.mean(y * y, axis=-1, keepdims=True)
        y = y * lax.rsqrt(ms + NORM_EPS) * fw_ref[...]
    o_ref[...] = y


def _block_tail(mixed, w_out, x2, g1, scale, shift, g2, w_gate_up, w_down, final_w, *, layer, seq, final_norm):
    tokens, d = x2.shape
    kdim = mixed.shape[1]
    d_ff = w_down.shape[1]
    tm = FFN_TM
    per_batch = seq // tm
    mod_spec = pl.BlockSpec((1, 1, d), lambda i: (i // per_batch, 0, 0))
    resident = functools.partial(pl.BlockSpec, index_map=lambda i: (0, 0), pipeline_mode=pl.Buffered(1))
    stacked = functools.partial(pl.BlockSpec, index_map=lambda i: (layer, 0, 0), pipeline_mode=pl.Buffered(1))
    return pl.pallas_call(
        functools.partial(_block_tail_kernel, final_norm=final_norm),
        grid=(tokens // tm,),
        in_specs=[
            pl.BlockSpec((tm, kdim), lambda i: (i, 0)),
            resident((kdim, d)),
            pl.BlockSpec((tm, d), lambda i: (i, 0)),
            mod_spec, mod_spec, mod_spec, mod_spec,
            stacked((None, d, 2 * d_ff)),
            stacked((None, d_ff, d)),
            resident((1, d)),
        ],
        out_specs=pl.BlockSpec((tm, d), lambda i: (i, 0)),
        out_shape=jax.ShapeDtypeStruct((tokens, d), F32),
        scratch_shapes=[pltpu.VMEM((tm, d), F32), pltpu.VMEM((tm, d), BF16), pltpu.VMEM((tm, d_ff), BF16)],
        compiler_params=_params("parallel"),
        name="block_tail",
    )(mixed, w_out, x2, g1, scale, shift, g2, w_gate_up, w_down, final_w)


def _pad_cast_kernel(w_ref, o_ref):
    n_main = o_ref.shape[1] - LANES
    gates = w_ref.shape[1] - n_main
    o_ref[:, :n_main] = w_ref[:, :n_main].astype(o_ref.dtype)
    tail = jnp.concatenate([w_ref[:, n_main:], jnp.zeros((w_ref.shape[0], LANES - gates), w_ref.dtype)], axis=1)
    o_ref[:, n_main:] = tail.astype(o_ref.dtype)


def _padded_in_weights(w_in, n_main):
    layers, d, width = w_in.shape
    rows = WEIGHT_PREP_ROWS
    return pl.pallas_call(
        _pad_cast_kernel,
        grid=(layers, d // rows),
        in_specs=[pl.BlockSpec((None, rows, width), lambda l, i: (l, i, 0))],
        out_specs=pl.BlockSpec((None, rows, n_main + LANES), lambda l, i: (l, i, 0)),
        out_shape=jax.ShapeDtypeStruct((layers, d, n_main + LANES), BF16),
        compiler_params=_params("parallel", "parallel"),
        name="pad_cast",
    )(w_in)


def _gate_vec(vec, offset, n_rows):
    count = vec.shape[0]
    col = jnp.zeros((1, LANES), F32).at[0, offset:offset + count].set(vec.astype(F32))
    row = jnp.zeros((n_rows, 1), F32).at[offset:offset + count, 0].set(vec.astype(F32))
    return col, row


def kernel(x, c, ada_w, ada_b, ml_w_in, ml_b_if, ml_norm_w, ml_w_out, gdn_w_in, gdn_conv_w, gdn_a_log,
           gdn_dt_bias, gdn_norm_w, gdn_w_out, ffn_w_gate_up, ffn_w_down, final_norm_w):
    batch, seq, d = x.shape
    depth = ada_w.shape[0]
    n_mixers = 2
    tokens = batch * seq
    x2 = x.reshape(tokens, d)

    c_rows = 8
    c_pad = jnp.pad(c, ((0, c_rows - batch), (0, 0)))
    mod = _adaln(c_pad, ada_w, ada_b)[:, :batch, :]

    w_gate_up = ffn_w_gate_up.astype(BF16)
    w_down = ffn_w_down.astype(BF16)
    for layer in range(depth):
        sh1, sc1, g1, sh2, sc2, g2 = [m.reshape(batch, 1, d) for m in jnp.split(mod[layer], 6, axis=-1)]
        jdx = layer // n_mixers
        if layer % n_mixers == 0:
            b_col, b_row = _gate_vec(ml_b_if[jdx], 0, ML_GATE_ROWS)
            proj, gcol, grow = _inproj(x2, sc1, sh1, _padded_in_weights(ml_w_in, ML_MAIN),
                                       b_col, b_row, mode="ml", layer=jdx, seq=seq)
            mixed = _mlstm(proj, gcol, grow, ml_norm_w[jdx].reshape(1, ML_V_WIDTH).astype(F32),
                           batch=batch, seq=seq)
            w_out = ml_w_out[jdx]
        else:
            b_col, b_row = _gate_vec(gdn_dt_bias[jdx], GDN_V_HEADS, GDN_GATE_ROWS)
            a_col, a_row = _gate_vec(gdn_a_log[jdx], GDN_V_HEADS, GDN_GATE_ROWS)
            proj, gcol, grow = _inproj(x2, sc1, sh1, _padded_in_weights(gdn_w_in, GDN_MAIN),
                                       b_col, b_row, (a_col, a_row, gdn_conv_w[jdx].astype(F32)),
                                       mode="gdn", layer=jdx, seq=seq)
            mixed = _gdn(proj, gcol, grow, gdn_norm_w[jdx].reshape(1, GDN_HEAD_DIM).astype(F32),
                         batch=batch, seq=seq)
            w_out = gdn_w_out[jdx]
        x2 = _block_tail(mixed, w_out.astype(BF16), x2, g1, sc2, sh2, g2,
                         w_gate_up, w_down, final_norm_w.reshape(1, d).astype(F32),
                         layer=layer, seq=seq, final_norm=(layer == depth - 1))
    return x2.reshape(batch, seq, d)
```

```python
import functools

import jax
import jax.numpy as jnp
from jax import lax
from jax.experimental import pallas as pl
from jax.experimental.pallas import tpu as pltpu

F32 = jnp.float32
BF16 = jnp.bfloat16
HIGHEST = lax.Precision.HIGHEST

LANES = 128
VMEM_LIMIT = 56 * 1024 * 1024

D_MODEL = 1024
CHUNK = 64
NORM_EPS = 1e-6

ML_HEADS = 4
ML_QK_DIM = 128
ML_V_DIM = 256
ML_QK_WIDTH = ML_HEADS * ML_QK_DIM
ML_V_WIDTH = ML_HEADS * ML_V_DIM
ML_MAIN = 2 * ML_QK_WIDTH + 2 * ML_V_WIDTH
ML_GATE_ROWS = 16
ML_GATE_CAP = 15.0

GDN_K_HEADS = 8
GDN_V_HEADS = 16
GDN_HEAD_DIM = 128
GDN_KEY_WIDTH = GDN_K_HEADS * GDN_HEAD_DIM
GDN_VAL_WIDTH = GDN_V_HEADS * GDN_HEAD_DIM
GDN_CONV_DIM = 2 * GDN_KEY_WIDTH + GDN_VAL_WIDTH
GDN_CONV_WIDTH = 4
GDN_MAIN = GDN_CONV_DIM + GDN_VAL_WIDTH
GDN_GATE_ROWS = 2 * GDN_V_HEADS

WEIGHT_PREP_MAX_TILES = 8
ADALN_TN = 2048
INPROJ_TM = 512
INPROJ_TN = 256
CONV_HALO = 8
PIPE_PIECES = 4
REC_ROWS = 256
GDN_ROWS = 128
FFN_TM = 512
FFN_TH = 256


def _params(*sem):
    return pltpu.CompilerParams(dimension_semantics=sem, vmem_limit_bytes=VMEM_LIMIT)


def _sigmoid(x):
    return 1.0 / (1.0 + jnp.exp(-x))


def _softplus(x):
    return jnp.maximum(x, 0.0) + jnp.log1p(jnp.exp(-jnp.abs(x)))


def _dot(a, b):
    return jnp.dot(a, b, preferred_element_type=F32)


def _dot_nt(a, b):
    return lax.dot_general(a, b, (((1,), (1,)), ((), ())), preferred_element_type=F32)


def _dot_tn(a, b):
    return lax.dot_general(a, b, (((0,), (0,)), ((), ())), preferred_element_type=F32)


def _iota(shape, dim):
    return lax.broadcasted_iota(jnp.int32, shape, dim)


def _adaln_kernel(c_ref, w_ref, b_ref, o_ref):
    c = c_ref[...]
    act = c * _sigmoid(c)
    o_ref[0] = jnp.dot(act, w_ref[0], precision=HIGHEST, preferred_element_type=F32) + b_ref[0]


def _adaln(c_pad, ada_w, ada_b):
    depth, d, n = ada_w.shape
    rows = c_pad.shape[0]
    tn = ADALN_TN
    return pl.pallas_call(
        _adaln_kernel,
        grid=(depth, n // tn),
        in_specs=[
            pl.BlockSpec((rows, d), lambda l, j: (0, 0)),
            pl.BlockSpec((1, d, tn), lambda l, j: (l, 0, j)),
            pl.BlockSpec((1, 1, tn), lambda l, j: (l, 0, j)),
        ],
        out_specs=pl.BlockSpec((1, rows, tn), lambda l, j: (l, 0, j)),
        out_shape=jax.ShapeDtypeStruct((depth, rows, n), F32),
        compiler_params=_params("parallel", "parallel"),
        name="adaln",
    )(c_pad, ada_w, ada_b.reshape(depth, 1, n))


def _modulated_norm(x, scale, shift):
    ms = jnp.mean(x * x, axis=-1, keepdims=True)
    return x * lax.rsqrt(ms + NORM_EPS) * (1.0 + scale) + shift


def _ml_gates(pre, idx):
    capped = ML_GATE_CAP * jnp.tanh(pre / ML_GATE_CAP)
    log_f = jnp.minimum(capped, 0.0) - jnp.log1p(jnp.exp(-jnp.abs(capped)))
    is_f = idx >= ML_HEADS
    return jnp.where(is_f, log_f, capped), is_f


def _gdn_gates(pre, a_log, idx):
    is_g = idx >= GDN_V_HEADS
    return jnp.where(is_g, -jnp.exp(a_log) * _softplus(pre), _sigmoid(pre)), is_g


def _inproj_kernel(*refs, mode, per_batch):
    if mode == "ml":
        (x_ref, sc_ref, sh_ref, w_ref, bcol_ref, brow_ref,
         proj_ref, gcol_ref, grow_ref, h_scr) = refs
    else:
        (x_ref, sc_ref, sh_ref, w_ref, bcol_ref, brow_ref, acol_ref, arow_ref, cw_ref,
         proj_ref, gcol_ref, grow_ref, h_scr, halo_scr, conv_buf) = refs
    tm = x_ref.shape[0]
    n = proj_ref.shape[1]
    n_rows = brow_ref.shape[0]
    tn = INPROJ_TN

    h_scr[...] = _modulated_norm(x_ref[...], sc_ref[0], sh_ref[0]).astype(BF16)

    def gate_stages():
        chunks = [slice(ch * CHUNK, (ch + 1) * CHUNK) for ch in range(tm // CHUNK)]
        lane = _iota((CHUNK, LANES), 1)
        row = _iota((n_rows, CHUNK), 0)
        rr = _iota((CHUNK, CHUNK), 0)
        cc = _iota((CHUNK, CHUNK), 1)
        lower = jnp.where(cc <= rr, 1.0, 0.0).astype(F32)
        upper = jnp.where(rr <= cc, 1.0, 0.0).astype(F32)
        wg = w_ref[:, n:n + LANES]
        wg_t = wg.astype(F32).T[:n_rows, :].astype(BF16)
        pre = _dot(h_scr[...], wg) + bcol_ref[...]
        pre_t = [_dot_nt(wg_t, h_scr[rs, :]) + brow_ref[...] for rs in chunks]
        yield
        if mode == "ml":
            col = [_ml_gates(pre[rs, :], lane) for rs in chunks]
        else:
            col = [_gdn_gates(pre[rs, :], acol_ref[...], lane) for rs in chunks]
        yield
        if mode == "ml":
            rowg = [_ml_gates(p_t, row) for p_t in pre_t]
        else:
            rowg = [_gdn_gates(p_t, arow_ref[...], row) for p_t in pre_t]
        yield
        csum = [jnp.dot(lower, vals, precision=HIGHEST, preferred_element_type=F32) for vals, _ in col]
        csum_t = [jnp.dot(vals_t, upper, precision=HIGHEST, preferred_element_type=F32) for vals_t, _ in rowg]
        yield
        for ch, rs in enumerate(chunks):
            gcol_ref[rs, :] = jnp.where(col[ch][1], csum[ch], col[ch][0])
            grow_ref[ch] = jnp.where(rowg[ch][1], csum_t[ch], rowg[ch][0])
        yield

    stages = gate_stages()
    order = list(range(n // tn))
    if mode == "gdn":
        first_of_seq = pl.program_id(0) % per_batch == 0
        conv_slabs = [j for j in order if j * tn < GDN_CONV_DIM]
        plain_slabs = [j for j in order if j * tn >= GDN_CONV_DIM]
        order = []
        while conv_slabs or plain_slabs:
            order += conv_slabs[:2] + plain_slabs[:1]
            conv_slabs, plain_slabs = conv_slabs[2:], plain_slabs[1:]
    d = x_ref.shape[1]
    kp, rp = d // PIPE_PIECES, tm // PIPE_PIECES

    def slab_cols(j):
        return slice(j * tn, (j + 1) * tn)

    def matmul_piece(j, kk):
        return _dot(h_scr[:, kk * kp:(kk + 1) * kp], w_ref[kk * kp:(kk + 1) * kp, slab_cols(j)])

    def start_epilogue(j, y):
        if mode == "gdn" and j * tn < GDN_CONV_DIM:
            halo_ref = halo_scr.at[:, slab_cols(j)]
            conv_buf[0:CONV_HALO, :] = jnp.where(first_of_seq, 0.0, halo_ref[...])
            conv_buf[CONV_HALO:CONV_HALO + tm, :] = y
            halo_ref[...] = y[tm - CONV_HALO:tm, :]

    def epilogue_piece(j, y, r):
        cs, rows = slab_cols(j), slice(r * rp, (r + 1) * rp)
        if not (mode == "gdn" and j * tn < GDN_CONV_DIM):
            proj_ref[rows, cs] = y[rows, :].astype(proj_ref.dtype)
            return
        conv_w = cw_ref[:, cs]
        acc = conv_w[GDN_CONV_WIDTH - 1:GDN_CONV_WIDTH, :] * y[rows, :]
        for tap in range(GDN_CONV_WIDTH - 1):
            first = CONV_HALO - (GDN_CONV_WIDTH - 1 - tap) + r * rp
            acc = acc + conv_w[tap:tap + 1, :] * conv_buf[first:first + rp, :]
        acc = acc * _sigmoid(acc)
        if j * tn >= 2 * GDN_KEY_WIDTH:
            proj_ref[rows, cs] = acc.astype(proj_ref.dtype)
            return
        q_scale = GDN_HEAD_DIM ** -0.5 if j * tn < GDN_KEY_WIDTH else 1.0
        for hh in range(tn // GDN_HEAD_DIM):
            hs = slice(hh * GDN_HEAD_DIM, (hh + 1) * GDN_HEAD_DIM)
            blk = acc[:, hs]
            ss = jnp.sum(blk * blk, axis=1, keepdims=True)
            proj_ref[rows, j * tn + hh * GDN_HEAD_DIM:j * tn + (hh + 1) * GDN_HEAD_DIM] = (
                blk * (lax.rsqrt(ss + NORM_EPS) * q_scale)).astype(proj_ref.dtype)

    y = _dot(h_scr[...], w_ref[:, slab_cols(order[0])])
    for idx, j in enumerate(order):
        next(stages, None)
        start_epilogue(j, y)
        following = order[idx + 1] if idx + 1 < len(order) else None
        y_next = None
        for piece in range(PIPE_PIECES):
            if following is not None:
                part = matmul_piece(following, piece)
                y_next = part if y_next is None else y_next + part
            epilogue_piece(j, y, piece)
        y = y_next
    for _ in stages:
        pass


def _inproj(x2, scale, shift, w_all, bias_col, bias_row, gdn_extra=None, *, mode, layer, seq):
    tokens, d = x2.shape
    n = ML_MAIN if mode == "ml" else GDN_MAIN
    n_rows = bias_row.shape[0]
    tm = INPROJ_TM
    per_batch = seq // tm
    const = lambda i: (0, 0)
    resident = functools.partial(pl.BlockSpec, index_map=const, pipeline_mode=pl.Buffered(1))
    in_specs = [
        pl.BlockSpec((tm, d), lambda i: (i, 0)),
        pl.BlockSpec((1, 1, d), lambda i: (i // per_batch, 0, 0)),
        pl.BlockSpec((1, 1, d), lambda i: (i // per_batch, 0, 0)),
        pl.BlockSpec((None,) + w_all.shape[1:], lambda i: (layer, 0, 0), pipeline_mode=pl.Buffered(1)),
        resident((1, LANES)),
        resident((n_rows, 1)),
    ]
    operands = [x2, scale, shift, w_all, bias_col, bias_row]
    scratch = [pltpu.VMEM((tm, d), BF16)]
    if mode == "gdn":
        alog_col, alog_row, conv_w = gdn_extra
        in_specs += [resident((1, LANES)), resident((n_rows, 1)), resident((GDN_CONV_WIDTH, GDN_CONV_DIM))]
        operands += [alog_col, alog_row, conv_w]
        scratch += [pltpu.VMEM((CONV_HALO, GDN_CONV_DIM), F32),
                    pltpu.VMEM((tm + CONV_HALO, INPROJ_TN), F32)]
    return pl.pallas_call(
        functools.partial(_inproj_kernel, mode=mode, per_batch=per_batch),
        grid=(tokens // tm,),
        in_specs=in_specs,
        out_specs=[
            pl.BlockSpec((tm, n), lambda i: (i, 0)),
            pl.BlockSpec((tm, LANES), lambda i: (i, 0)),
            pl.BlockSpec((tm // CHUNK, n_rows, CHUNK), lambda i: (i, 0, 0)),
        ],
        out_shape=[
            jax.ShapeDtypeStruct((tokens, n), BF16),
            jax.ShapeDtypeStruct((tokens, LANES), F32),
            jax.ShapeDtypeStruct((tokens // CHUNK, n_rows, CHUNK), F32),
        ],
        scratch_shapes=scratch,
        compiler_params=_params("arbitrary"),
        name="inproj_" + mode,
    )(*operands)


def _mlstm_kernel(q_ref, k_ref, v_ref, o_ref, gcol_ref, grow_ref, nw_ref, out_ref,
                  c_scr, n_scr, m_scr, cprev_scr):
    rows = q_ref.shape[0]

    @pl.when(pl.program_id(1) == 0)
    def _():
        c_scr[...] = jnp.zeros_like(c_scr)
        n_scr[...] = jnp.zeros_like(n_scr)
        m_scr[...] = jnp.zeros_like(m_scr)

    causal = _iota((CHUNK, CHUNK), 0) >= _iota((CHUNK, CHUNK), 1)
    k_scale = ML_QK_DIM ** -0.5
    n_ch = rows // CHUNK
    items = [(ch, h) for ch in range(n_ch) for h in range(ML_HEADS)]

    def row_slice(ch):
        return slice(ch * CHUNK, (ch + 1) * CHUNK)

    def qk_slice(h):
        return slice(h * ML_QK_DIM, (h + 1) * ML_QK_DIM)

    def v_slice(h):
        return slice(h * ML_V_DIM, (h + 1) * ML_V_DIM)

    n_items = len(items)

    lane = _iota((CHUNK, LANES), 1)

    def lane_rep(col):
        return jnp.broadcast_to(col, (CHUNK, LANES))

    def wide(a):
        return jnp.concatenate([a] * (ML_V_DIM // LANES), axis=1)

    b_col, b_last, d_mat, g_end, m_intra, m_loc, k_w, d_c, d_n, scores = ({} for _ in range(10))
    for it, (ch, h) in enumerate(items):
        rs = row_slice(ch)
        gc = gcol_ref[rs, :]
        gr = grow_ref[ch]
        li_c = lane_rep(jnp.sum(jnp.where(lane == h, gc, 0.0), axis=1, keepdims=True))
        b_c = lane_rep(jnp.sum(jnp.where(lane == ML_HEADS + h, gc, 0.0), axis=1, keepdims=True))
        li_r = gr[h:h + 1, :]
        b_r = gr[ML_HEADS + h:ML_HEADS + h + 1, :]
        b_col[it], b_last[it] = b_c, b_c[CHUNK - 1:CHUNK, :]
        d_mat[it] = jnp.where(causal, b_c[:, :CHUNK] - b_r + li_r, -jnp.inf)
        g_end[it] = b_last[it] - b_c + li_c
    for it in range(n_items):
        m_intra[it] = jnp.max(d_mat[it], axis=1, keepdims=True)
        m_loc[it] = jnp.max(g_end[it], axis=0, keepdims=True)
    for it, (ch, h) in enumerate(items):
        k_w[it] = (k_ref[row_slice(ch), qk_slice(h)].astype(F32)
                   * (k_scale * jnp.exp(g_end[it] - m_loc[it])))
    for it, (ch, h) in enumerate(items):
        rs = row_slice(ch)
        d_c[it] = _dot_tn(k_w[it].astype(BF16), v_ref[rs, v_slice(h)])
        d_n[it] = jnp.sum(k_w[it], axis=0, keepdims=True)
        scores[it] = _dot_nt(q_ref[rs, qk_slice(h)], k_ref[rs, qk_slice(h)]) * k_scale

    m_prev, n_prev = {}, {}
    m_run = [m_scr[h] for h in range(ML_HEADS)]
    n_run = [n_scr[h] for h in range(ML_HEADS)]
    for it, (ch, h) in enumerate(items):
        c_run = c_scr[h]
        cprev_scr[it] = c_run.astype(BF16)
        m_prev[it], n_prev[it] = m_run[h], n_run[h]
        m_new = jnp.maximum(b_last[it] + m_run[h], m_loc[it])
        a_old = jnp.exp(b_last[it] + m_run[h] - m_new)
        a_new = jnp.exp(m_loc[it] - m_new)
        c_scr[h] = wide(a_old) * c_run + wide(a_new) * d_c[it]
        n_run[h] = a_old * n_run[h] + a_new * d_n[it]
        m_run[h] = m_new
    for h in range(ML_HEADS):
        n_scr[h] = n_run[h]
        m_scr[h] = m_run[h]

    m_t, s_inter, p, num, den, hid, ms = ({} for _ in range(7))
    for it in range(n_items):
        m_inter = b_col[it] + m_prev[it]
        m_t[it] = jnp.maximum(m_inter, m_intra[it])
        s_inter[it] = jnp.exp(m_inter - m_t[it])
        p[it] = jnp.exp(d_mat[it] - m_t[it][:, :CHUNK]) * scores[it]
    for it, (ch, h) in enumerate(items):
        rs = row_slice(ch)
        q = q_ref[rs, qk_slice(h)]
        num[it] = (wide(s_inter[it]) * _dot(q, cprev_scr[it])
                   + _dot(p[it].astype(BF16), v_ref[rs, v_slice(h)]))
        den[it] = (s_inter[it] * jnp.sum(q.astype(F32) * n_prev[it], axis=1, keepdims=True)
                   + jnp.sum(p[it], axis=1, keepdims=True))
    for it in range(n_items):
        hid[it] = num[it] * wide(1.0 / jnp.maximum(jnp.abs(den[it]), jnp.exp(-m_t[it])))
        ms[it] = jnp.mean(hid[it] * hid[it], axis=1, keepdims=True)
    for it, (ch, h) in enumerate(items):
        rs, vs = row_slice(ch), v_slice(h)
        out = hid[it] * lax.rsqrt(ms[it] + NORM_EPS) * nw_ref[:, vs] * _sigmoid(o_ref[rs, vs].astype(F32))
        out_ref[rs, vs] = out.astype(out_ref.dtype)


def _mlstm(proj, gcol, grow, norm_w, *, batch, seq):
    tokens = proj.shape[0]
    rows = REC_ROWS
    steps = seq // rows
    row = lambda b, t: b * steps + t
    return pl.pallas_call(
        _mlstm_kernel,
        grid=(batch, steps),
        in_specs=[
            pl.BlockSpec((rows, ML_QK_WIDTH), lambda b, t: (row(b, t), 0)),
            pl.BlockSpec((rows, ML_QK_WIDTH), lambda b, t: (row(b, t), 1)),
            pl.BlockSpec((rows, ML_V_WIDTH), lambda b, t: (row(b, t), 1)),
            pl.BlockSpec((rows, ML_V_WIDTH), lambda b, t: (row(b, t), 2)),
            pl.BlockSpec((rows, LANES), lambda b, t: (row(b, t), 0)),
            pl.BlockSpec((rows // CHUNK, ML_GATE_ROWS, CHUNK), lambda b, t: (row(b, t), 0, 0)),
            pl.BlockSpec((1, ML_V_WIDTH), lambda b, t: (0, 0)),
        ],
        out_specs=pl.BlockSpec((rows, ML_V_WIDTH), lambda b, t: (row(b, t), 0)),
        out_shape=jax.ShapeDtypeStruct((tokens, ML_V_WIDTH), BF16),
        scratch_shapes=[
            pltpu.VMEM((ML_HEADS, ML_QK_DIM, ML_V_DIM), F32),
            pltpu.VMEM((ML_HEADS, 1, ML_QK_DIM), F32),
            pltpu.VMEM((ML_HEADS, 1, LANES), F32),
            pltpu.VMEM((ML_HEADS * rows // CHUNK, ML_QK_DIM, ML_V_DIM), BF16),
        ],
        compiler_params=_params("parallel", "arbitrary"),
        name="mlstm",
    )(proj, proj, proj, proj, gcol, grow, norm_w)


def _gdn_kernel(q_ref, k_ref, v_ref, z_ref, gcol_ref, grow_ref, nw_ref, out_ref,
                s_scr, wq_scr, attn_scr, kdt_scr, u_scr, eg_scr):
    rows = q_ref.shape[0]
    rep = GDN_V_HEADS // GDN_K_HEADS
    dh = GDN_HEAD_DIM
    n_kh = q_ref.shape[1] // dh
    n_vh = n_kh * rep
    n_ch = rows // CHUNK
    @pl.when(pl.program_id(1) == 0)
    def _():
        s_scr[...] = jnp.zeros_like(s_scr)

    ri = _iota((CHUNK, CHUNK), 0)
    ci = _iota((CHUNK, CHUNK), 1)
    causal = ri >= ci
    strict = ri > ci
    eye = jnp.where(ri == ci, 1.0, 0.0).astype(F32)
    lane = _iota((CHUNK, LANES), 1)
    merge_mask = {}
    s = 1
    while s < CHUNK:
        merge_mask[s] = (ri // s - ci // s == 1) & (ri // (2 * s) == ci // (2 * s))
        s *= 2

    kk, qk, qs, ks = {}, {}, {}, {}
    for ch in range(n_ch):
        rs = slice(ch * CHUNK, (ch + 1) * CHUNK)
        for kh in range(n_kh):
            hs = slice(kh * dh, (kh + 1) * dh)
            qb = q_ref[rs, hs]
            kb = k_ref[rs, hs]
            qs[ch, kh], ks[ch, kh] = qb.astype(F32), kb.astype(F32)
            kk[ch, kh] = _dot_nt(kb, kb)
            qk[ch, kh] = _dot_nt(qb, kb)

    items = [(ch, vh) for ch in range(n_ch) for vh in range(n_vh)]
    x, a_mats, rhs = {}, {}, {}
    for it, (ch, vh) in enumerate(items):
        rs = slice(ch * CHUNK, (ch + 1) * CHUNK)
        kh = vh // rep
        gc = gcol_ref[rs, :]
        beta_c = jnp.sum(jnp.where(lane == vh, gc, 0.0), axis=1, keepdims=True)
        g_c = jnp.sum(jnp.where(lane == GDN_V_HEADS + vh, gc, 0.0), axis=1, keepdims=True)
        g_r = grow_ref[ch, GDN_V_HEADS + vh:GDN_V_HEADS + vh + 1, :]
        g_last = g_r[:, CHUNK - 1:CHUNK]
        decay = jnp.exp(jnp.where(causal, g_c - g_r, -jnp.inf))
        a_mat = jnp.where(strict, kk[ch, kh] * beta_c * decay, 0.0)
        x[it] = eye - jnp.where(merge_mask[1], a_mat, 0.0)
        a_mats[it] = a_mat

        q, k = qs[ch, kh], ks[ch, kh]
        v = v_ref[rs, vh * dh:(vh + 1) * dh].astype(F32)
        e_g = jnp.exp(g_c)
        rhs[it] = jnp.concatenate([v * beta_c, k * (beta_c * e_g)], axis=1).astype(BF16)
        attn_scr[it] = (qk[ch, kh] * decay).astype(BF16)
        wq_scr[it, CHUNK:2 * CHUNK, :] = (q * e_g).astype(BF16)
        kdt_scr[it] = (k * jnp.exp(g_last - g_c)).T.astype(BF16)
        eg_scr[it] = jnp.broadcast_to(jnp.exp(g_last), (1, LANES))

    s = 2
    while s < CHUNK:
        for it in range(len(items)):
            nb = jnp.where(merge_mask[s], a_mats[it], 0.0).astype(BF16)
            y = _dot(nb, x[it].astype(BF16))
            x[it] = x[it] - _dot(x[it].astype(BF16), y.astype(BF16))
        s *= 2

    for it in range(len(items)):
        sol = _dot(x[it].astype(BF16), rhs[it])
        u_scr[it] = sol[:, :dh]
        wq_scr[it, 0:CHUNK, :] = sol[:, dh:].astype(BF16)

    state = [s_scr[vh] for vh in range(n_vh)]
    for ch in range(n_ch):
        rs = slice(ch * CHUNK, (ch + 1) * CHUNK)
        base = ch * n_vh
        sb = [state[vh].astype(BF16) for vh in range(n_vh)]
        r = [_dot(wq_scr[base + vh], sb[vh]) for vh in range(n_vh)]
        vnb = [(u_scr[base + vh] - r[vh][:CHUNK]).astype(BF16) for vh in range(n_vh)]
        state = [state[vh] * eg_scr[base + vh] + _dot(kdt_scr[base + vh], vnb[vh]) for vh in range(n_vh)]
        for vh in range(n_vh):
            out = r[vh][CHUNK:] + _dot(attn_scr[base + vh], vnb[vh])
            ms = jnp.mean(out * out, axis=1, keepdims=True)
            z = z_ref[rs, vh * dh:(vh + 1) * dh].astype(F32)
            gated = out * lax.rsqrt(ms + NORM_EPS) * nw_ref[...] * (z * _sigmoid(z))
            out_ref[rs, vh * dh:(vh + 1) * dh] = gated.astype(out_ref.dtype)
    for vh in range(n_vh):
        s_scr[vh] = state[vh]


def _gdn(proj, gcol, grow, norm_w, *, batch, seq):
    tokens = proj.shape[0]
    rows = GDN_ROWS
    steps = seq // rows
    n_items = GDN_V_HEADS * rows // CHUNK
    row = lambda b, t: b * steps + t
    return pl.pallas_call(
        _gdn_kernel,
        grid=(batch, steps),
        in_specs=[
            pl.BlockSpec((rows, GDN_KEY_WIDTH), lambda b, t: (row(b, t), 0)),
            pl.BlockSpec((rows, GDN_KEY_WIDTH), lambda b, t: (row(b, t), 1)),
            pl.BlockSpec((rows, GDN_VAL_WIDTH), lambda b, t: (row(b, t), 2 * GDN_KEY_WIDTH // GDN_VAL_WIDTH)),
            pl.BlockSpec((rows, GDN_VAL_WIDTH), lambda b, t: (row(b, t), GDN_CONV_DIM // GDN_VAL_WIDTH)),
            pl.BlockSpec((rows, LANES), lambda b, t: (row(b, t), 0)),
            pl.BlockSpec((rows // CHUNK, GDN_GATE_ROWS, CHUNK), lambda b, t: (row(b, t), 0, 0)),
            pl.BlockSpec((1, GDN_HEAD_DIM), lambda b, t: (0, 0)),
        ],
        out_specs=pl.BlockSpec((rows, GDN_VAL_WIDTH), lambda b, t: (row(b, t), 0)),
        out_shape=jax.ShapeDtypeStruct((tokens, GDN_VAL_WIDTH), BF16),
        scratch_shapes=[
            pltpu.VMEM((GDN_V_HEADS, GDN_HEAD_DIM, GDN_HEAD_DIM), F32),
            pltpu.VMEM((n_items, 2 * CHUNK, GDN_HEAD_DIM), BF16),
            pltpu.VMEM((n_items, CHUNK, CHUNK), BF16),
            pltpu.VMEM((n_items, GDN_HEAD_DIM, CHUNK), BF16),
            pltpu.VMEM((n_items, CHUNK, GDN_HEAD_DIM), F32),
            pltpu.VMEM((n_items, 1, LANES), F32),
        ],
        compiler_params=_params("parallel", "arbitrary"),
        name="gdn",
    )(proj, proj, proj, proj, gcol, grow, norm_w)


def _block_tail_kernel(a_ref, wo_ref, x_ref, g1_ref, sc_ref, sh_ref, g2_ref, wgu_ref, wd_ref, fw_ref, o_ref,
                       x1_scr, h_scr, act_scr, *, final_norm):
    d_ff = wd_ref.shape[0]
    x1 = x_ref[...] + g1_ref[0] * _dot(a_ref[...], wo_ref[...])
    x1_scr[...] = x1
    h_scr[...] = _modulated_norm(x1, sc_ref[0], sh_ref[0]).astype(BF16)
    for j in range(d_ff // FFN_TH):
        gate = _dot(h_scr[...], wgu_ref[:, j * FFN_TH:(j + 1) * FFN_TH])
        up = _dot(h_scr[...], wgu_ref[:, d_ff + j * FFN_TH:d_ff + (j + 1) * FFN_TH])
        act_scr[:, j * FFN_TH:(j + 1) * FFN_TH] = (gate * _sigmoid(gate) * up).astype(BF16)
    y = x1_scr[...] + g2_ref[0] * _dot(act_scr[...], wd_ref[...])
    if final_norm:
        ms = jnp.mean(y * y, axis=-1, keepdims=True)
        y = y * lax.rsqrt(ms + NORM_EPS) * fw_ref[...]
    o_ref[...] = y


def _block_tail(mixed, w_out, x2, g1, scale, shift, g2, w_gate_up, w_down, final_w, *, layer, seq, final_norm):
    tokens, d = x2.shape
    kdim = mixed.shape[1]
    d_ff = w_down.shape[1]
    tm = FFN_TM
    per_batch = seq // tm
    mod_spec = pl.BlockSpec((1, 1, d), lambda i: (i // per_batch, 0, 0))
    resident = functools.partial(pl.BlockSpec, index_map=lambda i: (0, 0), pipeline_mode=pl.Buffered(1))
    stacked = functools.partial(pl.BlockSpec, index_map=lambda i: (layer, 0, 0), pipeline_mode=pl.Buffered(1))
    return pl.pallas_call(
        functools.partial(_block_tail_kernel, final_norm=final_norm),
        grid=(tokens // tm,),
        in_specs=[
            pl.BlockSpec((tm, kdim), lambda i: (i, 0)),
            resident((kdim, d)),
            pl.BlockSpec((tm, d), lambda i: (i, 0)),
            mod_spec, mod_spec, mod_spec, mod_spec,
            stacked((None, d, 2 * d_ff)),
            stacked((None, d_ff, d)),
            resident((1, d)),
        ],
        out_specs=pl.BlockSpec((tm, d), lambda i: (i, 0)),
        out_shape=jax.ShapeDtypeStruct((tokens, d), F32),
        scratch_shapes=[pltpu.VMEM((tm, d), F32), pltpu.VMEM((tm, d), BF16), pltpu.VMEM((tm, d_ff), BF16)],
        compiler_params=_params("parallel"),
        name="block_tail",
    )(mixed, w_out, x2, g1, scale, shift, g2, w_gate_up, w_down, final_w)


def _pad_cast_kernel(wt_ref, o_ref, *, width):
    cols = wt_ref.shape[0]
    col = pl.program_id(1) * cols + _iota(wt_ref.shape, 0)
    o_ref[...] = jnp.where(col < width, wt_ref[...], 0.0).T.astype(o_ref.dtype)


def _padded_in_weights(w_in, n_main):
    layers, d, width = w_in.shape
    n_pad = n_main + LANES
    tiles = n_pad // LANES
    per_slab = max(t for t in range(1, WEIGHT_PREP_MAX_TILES + 1) if tiles % t == 0)
    cols = per_slab * LANES
    return pl.pallas_call(
        functools.partial(_pad_cast_kernel, width=width),
        grid=(layers, tiles // per_slab),
        in_specs=[pl.BlockSpec((None, cols, d), lambda l, j: (l, j, 0))],
        out_specs=pl.BlockSpec((None, d, cols), lambda l, j: (l, 0, j)),
        out_shape=jax.ShapeDtypeStruct((layers, d, n_pad), BF16),
        compiler_params=_params("parallel", "parallel"),
        name="pad_cast",
    )(jnp.swapaxes(w_in, 1, 2))


def _gate_vec(vec, offset, n_rows):
    count = vec.shape[0]
    col = jnp.zeros((1, LANES), F32).at[0, offset:offset + count].set(vec.astype(F32))
    row = jnp.zeros((n_rows, 1), F32).at[offset:offset + count, 0].set(vec.astype(F32))
    return col, row


def kernel(x, c, ada_w, ada_b, ml_w_in, ml_b_if, ml_norm_w, ml_w_out, gdn_w_in, gdn_conv_w, gdn_a_log,
           gdn_dt_bias, gdn_norm_w, gdn_w_out, ffn_w_gate_up, ffn_w_down, final_norm_w):
    batch, seq, d = x.shape
    depth = ada_w.shape[0]
    n_mixers = 2
    tokens = batch * seq
    x2 = x.reshape(tokens, d)

    c_rows = 8
    c_pad = jnp.pad(c, ((0, c_rows - batch), (0, 0)))
    mod = _adaln(c_pad, ada_w, ada_b)[:, :batch, :]

    w_gate_up = ffn_w_gate_up.astype(BF16)
    w_down = ffn_w_down.astype(BF16)
    for layer in range(depth):
        sh1, sc1, g1, sh2, sc2, g2 = [m.reshape(batch, 1, d) for m in jnp.split(mod[layer], 6, axis=-1)]
        jdx = layer // n_mixers
        if layer % n_mixers == 0:
            b_col, b_row = _gate_vec(ml_b_if[jdx], 0, ML_GATE_ROWS)
            proj, gcol, grow = _inproj(x2, sc1, sh1, _padded_in_weights(ml_w_in, ML_MAIN),
                                       b_col, b_row, mode="ml", layer=jdx, seq=seq)
            mixed = _mlstm(proj, gcol, grow, ml_norm_w[jdx].reshape(1, ML_V_WIDTH).astype(F32),
                           batch=batch, seq=seq)
            w_out = ml_w_out[jdx]
        else:
            b_col, b_row = _gate_vec(gdn_dt_bias[jdx], GDN_V_HEADS, GDN_GATE_ROWS)
            a_col, a_row = _gate_vec(gdn_a_log[jdx], GDN_V_HEADS, GDN_GATE_ROWS)
            proj, gcol, grow = _inproj(x2, sc1, sh1, _padded_in_weights(gdn_w_in, GDN_MAIN),
                                       b_col, b_row, (a_col, a_row, gdn_conv_w[jdx].astype(F32)),
                                       mode="gdn", layer=jdx, seq=seq)
            mixed = _gdn(proj, gcol, grow, gdn_norm_w[jdx].reshape(1, GDN_HEAD_DIM).astype(F32),
                         batch=batch, seq=seq)
            w_out = gdn_w_out[jdx]
        x2 = _block_tail(mixed, w_out.astype(BF16), x2, g1, sc2, sh2, g2,
                         w_gate_up, w_down, final_norm_w.reshape(1, d).astype(F32),
                         layer=layer, seq=seq, final_norm=(layer == depth - 1))
    return x2.reshape(batch, seq, d)
```

```python
import functools

import jax
import jax.numpy as jnp
from jax import lax
from jax.experimental import pallas as pl
from jax.experimental.pallas import tpu as pltpu

F32 = jnp.float32
BF16 = jnp.bfloat16
HIGHEST = lax.Precision.HIGHEST

LANES = 128
VMEM_LIMIT = 56 * 1024 * 1024

D_MODEL = 1024
CHUNK = 64
NORM_EPS = 1e-6

ML_HEADS = 4
ML_QK_DIM = 128
ML_V_DIM = 256
ML_QK_WIDTH = ML_HEADS * ML_QK_DIM
ML_V_WIDTH = ML_HEADS * ML_V_DIM
ML_MAIN = 2 * ML_QK_WIDTH + 2 * ML_V_WIDTH
ML_GATE_ROWS = 16
ML_GATE_CAP = 15.0

GDN_K_HEADS = 8
GDN_V_HEADS = 16
GDN_HEAD_DIM = 128
GDN_KEY_WIDTH = GDN_K_HEADS * GDN_HEAD_DIM
GDN_VAL_WIDTH = GDN_V_HEADS * GDN_HEAD_DIM
GDN_CONV_DIM = 2 * GDN_KEY_WIDTH + GDN_VAL_WIDTH
GDN_CONV_WIDTH = 4
GDN_MAIN = GDN_CONV_DIM + GDN_VAL_WIDTH
GDN_GATE_ROWS = 2 * GDN_V_HEADS

WEIGHT_PREP_MAX_TILES = 8
ADALN_TN = 2048
INPROJ_TM = 512
INPROJ_TN = 256
CONV_HALO = 8
PIPE_PIECES = 4
REC_ROWS = 256
GDN_ROWS = 256
FFN_TM = 512
FFN_TH = 256


def _params(*sem):
    return pltpu.CompilerParams(dimension_semantics=sem, vmem_limit_bytes=VMEM_LIMIT)


def _sigmoid(x):
    return 1.0 / (1.0 + jnp.exp(-x))


def _softplus(x):
    return jnp.maximum(x, 0.0) + jnp.log1p(jnp.exp(-jnp.abs(x)))


def _dot(a, b):
    return jnp.dot(a, b, preferred_element_type=F32)


def _dot_nt(a, b):
    return lax.dot_general(a, b, (((1,), (1,)), ((), ())), preferred_element_type=F32)


def _dot_tn(a, b):
    return lax.dot_general(a, b, (((0,), (0,)), ((), ())), preferred_element_type=F32)


def _iota(shape, dim):
    return lax.broadcasted_iota(jnp.int32, shape, dim)


def _adaln_kernel(c_ref, w_ref, b_ref, o_ref):
    c = c_ref[...]
    act = c * _sigmoid(c)
    o_ref[0] = jnp.dot(act, w_ref[0], precision=HIGHEST, preferred_element_type=F32) + b_ref[0]


def _adaln(c_pad, ada_w, ada_b):
    depth, d, n = ada_w.shape
    rows = c_pad.shape[0]
    tn = ADALN_TN
    return pl.pallas_call(
        _adaln_kernel,
        grid=(depth, n // tn),
        in_specs=[
            pl.BlockSpec((rows, d), lambda l, j: (0, 0)),
            pl.BlockSpec((1, d, tn), lambda l, j: (l, 0, j)),
            pl.BlockSpec((1, 1, tn), lambda l, j: (l, 0, j)),
        ],
        out_specs=pl.BlockSpec((1, rows, tn), lambda l, j: (l, 0, j)),
        out_shape=jax.ShapeDtypeStruct((depth, rows, n), F32),
        compiler_params=_params("parallel", "parallel"),
        name="adaln",
    )(c_pad, ada_w, ada_b.reshape(depth, 1, n))


def _modulated_norm(x, scale, shift):
    ms = jnp.mean(x * x, axis=-1, keepdims=True)
    return x * lax.rsqrt(ms + NORM_EPS) * (1.0 + scale) + shift


def _ml_gates(pre, idx):
    capped = ML_GATE_CAP * jnp.tanh(pre / ML_GATE_CAP)
    log_f = jnp.minimum(capped, 0.0) - jnp.log1p(jnp.exp(-jnp.abs(capped)))
    is_f = idx >= ML_HEADS
    return jnp.where(is_f, log_f, capped), is_f


def _gdn_gates(pre, a_log, idx):
    is_g = idx >= GDN_V_HEADS
    return jnp.where(is_g, -jnp.exp(a_log) * _softplus(pre), _sigmoid(pre)), is_g


def _inproj_kernel(*refs, mode, per_batch):
    if mode == "ml":
        (x_ref, sc_ref, sh_ref, w_ref, bcol_ref, brow_ref,
         proj_ref, gcol_ref, grow_ref, h_scr) = refs
    else:
        (x_ref, sc_ref, sh_ref, w_ref, bcol_ref, brow_ref, acol_ref, arow_ref, cw_ref,
         proj_ref, gcol_ref, grow_ref, h_scr, halo_scr, conv_buf) = refs
    tm = x_ref.shape[0]
    n = proj_ref.shape[1]
    n_rows = brow_ref.shape[0]
    tn = INPROJ_TN

    h_scr[...] = _modulated_norm(x_ref[...], sc_ref[0], sh_ref[0]).astype(BF16)

    def gate_stages():
        chunks = [slice(ch * CHUNK, (ch + 1) * CHUNK) for ch in range(tm // CHUNK)]
        lane = _iota((CHUNK, LANES), 1)
        row = _iota((n_rows, CHUNK), 0)
        rr = _iota((CHUNK, CHUNK), 0)
        cc = _iota((CHUNK, CHUNK), 1)
        lower = jnp.where(cc <= rr, 1.0, 0.0).astype(F32)
        upper = jnp.where(rr <= cc, 1.0, 0.0).astype(F32)
        wg = w_ref[:, n:n + LANES]
        wg_t = wg.astype(F32).T[:n_rows, :].astype(BF16)
        pre = _dot(h_scr[...], wg) + bcol_ref[...]
        pre_t = [_dot_nt(wg_t, h_scr[rs, :]) + brow_ref[...] for rs in chunks]
        yield
        if mode == "ml":
            col = [_ml_gates(pre[rs, :], lane) for rs in chunks]
        else:
            col = [_gdn_gates(pre[rs, :], acol_ref[...], lane) for rs in chunks]
        yield
        if mode == "ml":
            rowg = [_ml_gates(p_t, row) for p_t in pre_t]
        else:
            rowg = [_gdn_gates(p_t, arow_ref[...], row) for p_t in pre_t]
        yield
        csum = [jnp.dot(lower, vals, precision=HIGHEST, preferred_element_type=F32) for vals, _ in col]
        csum_t = [jnp.dot(vals_t, upper, precision=HIGHEST, preferred_element_type=F32) for vals_t, _ in rowg]
        yield
        for ch, rs in enumerate(chunks):
            gcol_ref[rs, :] = jnp.where(col[ch][1], csum[ch], col[ch][0])
            grow_ref[ch] = jnp.where(rowg[ch][1], csum_t[ch], rowg[ch][0])
        yield

    stages = gate_stages()
    order = list(range(n // tn))
    if mode == "gdn":
        first_of_seq = pl.program_id(0) % per_batch == 0
        conv_slabs = [j for j in order if j * tn < GDN_CONV_DIM]
        plain_slabs = [j for j in order if j * tn >= GDN_CONV_DIM]
        order = []
        while conv_slabs or plain_slabs:
            order += conv_slabs[:2] + plain_slabs[:1]
            conv_slabs, plain_slabs = conv_slabs[2:], plain_slabs[1:]
    d = x_ref.shape[1]
    kp, rp = d // PIPE_PIECES, tm // PIPE_PIECES

    def slab_cols(j):
        return slice(j * tn, (j + 1) * tn)

    def matmul_piece(j, kk):
        return _dot(h_scr[:, kk * kp:(kk + 1) * kp], w_ref[kk * kp:(kk + 1) * kp, slab_cols(j)])

    def start_epilogue(j, y):
        if mode == "gdn" and j * tn < GDN_CONV_DIM:
            halo_ref = halo_scr.at[:, slab_cols(j)]
            conv_buf[0:CONV_HALO, :] = jnp.where(first_of_seq, 0.0, halo_ref[...])
            conv_buf[CONV_HALO:CONV_HALO + tm, :] = y
            halo_ref[...] = y[tm - CONV_HALO:tm, :]

    def epilogue_piece(j, y, r):
        cs, rows = slab_cols(j), slice(r * rp, (r + 1) * rp)
        if not (mode == "gdn" and j * tn < GDN_CONV_DIM):
            proj_ref[rows, cs] = y[rows, :].astype(proj_ref.dtype)
            return
        conv_w = cw_ref[:, cs]
        acc = conv_w[GDN_CONV_WIDTH - 1:GDN_CONV_WIDTH, :] * y[rows, :]
        for tap in range(GDN_CONV_WIDTH - 1):
            first = CONV_HALO - (GDN_CONV_WIDTH - 1 - tap) + r * rp
            acc = acc + conv_w[tap:tap + 1, :] * conv_buf[first:first + rp, :]
        acc = acc * _sigmoid(acc)
        if j * tn >= 2 * GDN_KEY_WIDTH:
            proj_ref[rows, cs] = acc.astype(proj_ref.dtype)
            return
        q_scale = GDN_HEAD_DIM ** -0.5 if j * tn < GDN_KEY_WIDTH else 1.0
        for hh in range(tn // GDN_HEAD_DIM):
            hs = slice(hh * GDN_HEAD_DIM, (hh + 1) * GDN_HEAD_DIM)
            blk = acc[:, hs]
            ss = jnp.sum(blk * blk, axis=1, keepdims=True)
            proj_ref[rows, j * tn + hh * GDN_HEAD_DIM:j * tn + (hh + 1) * GDN_HEAD_DIM] = (
                blk * (lax.rsqrt(ss + NORM_EPS) * q_scale)).astype(proj_ref.dtype)

    y = _dot(h_scr[...], w_ref[:, slab_cols(order[0])])
    for idx, j in enumerate(order):
        next(stages, None)
        start_epilogue(j, y)
        following = order[idx + 1] if idx + 1 < len(order) else None
        y_next = None
        for piece in range(PIPE_PIECES):
            if following is not None:
                part = matmul_piece(following, piece)
                y_next = part if y_next is None else y_next + part
            epilogue_piece(j, y, piece)
        y = y_next
    for _ in stages:
        pass


def _inproj(x2, scale, shift, w_all, bias_col, bias_row, gdn_extra=None, *, mode, layer, seq):
    tokens, d = x2.shape
    n = ML_MAIN if mode == "ml" else GDN_MAIN
    n_rows = bias_row.shape[0]
    tm = INPROJ_TM
    per_batch = seq // tm
    const = lambda i: (0, 0)
    resident = functools.partial(pl.BlockSpec, index_map=const, pipeline_mode=pl.Buffered(1))
    in_specs = [
        pl.BlockSpec((tm, d), lambda i: (i, 0)),
        pl.BlockSpec((1, 1, d), lambda i: (i // per_batch, 0, 0)),
        pl.BlockSpec((1, 1, d), lambda i: (i // per_batch, 0, 0)),
        pl.BlockSpec((None,) + w_all.shape[1:], lambda i: (layer, 0, 0), pipeline_mode=pl.Buffered(1)),
        resident((1, LANES)),
        resident((n_rows, 1)),
    ]
    operands = [x2, scale, shift, w_all, bias_col, bias_row]
    scratch = [pltpu.VMEM((tm, d), BF16)]
    if mode == "gdn":
        alog_col, alog_row, conv_w = gdn_extra
        in_specs += [resident((1, LANES)), resident((n_rows, 1)), resident((GDN_CONV_WIDTH, GDN_CONV_DIM))]
        operands += [alog_col, alog_row, conv_w]
        scratch += [pltpu.VMEM((CONV_HALO, GDN_CONV_DIM), F32),
                    pltpu.VMEM((tm + CONV_HALO, INPROJ_TN), F32)]
    return pl.pallas_call(
        functools.partial(_inproj_kernel, mode=mode, per_batch=per_batch),
        grid=(tokens // tm,),
        in_specs=in_specs,
        out_specs=[
            pl.BlockSpec((tm, n), lambda i: (i, 0)),
            pl.BlockSpec((tm, LANES), lambda i: (i, 0)),
            pl.BlockSpec((tm // CHUNK, n_rows, CHUNK), lambda i: (i, 0, 0)),
        ],
        out_shape=[
            jax.ShapeDtypeStruct((tokens, n), BF16),
            jax.ShapeDtypeStruct((tokens, LANES), F32),
            jax.ShapeDtypeStruct((tokens // CHUNK, n_rows, CHUNK), F32),
        ],
        scratch_shapes=scratch,
        compiler_params=_params("arbitrary"),
        name="inproj_" + mode,
    )(*operands)


def _mlstm_kernel(q_ref, k_ref, v_ref, o_ref, gcol_ref, grow_ref, nw_ref, out_ref,
                  c_scr, n_scr, m_scr, cprev_scr):
    rows = q_ref.shape[0]

    @pl.when(pl.program_id(1) == 0)
    def _():
        c_scr[...] = jnp.zeros_like(c_scr)
        n_scr[...] = jnp.zeros_like(n_scr)
        m_scr[...] = jnp.zeros_like(m_scr)

    causal = _iota((CHUNK, CHUNK), 0) >= _iota((CHUNK, CHUNK), 1)
    k_scale = ML_QK_DIM ** -0.5
    n_ch = rows // CHUNK
    items = [(ch, h) for ch in range(n_ch) for h in range(ML_HEADS)]

    def row_slice(ch):
        return slice(ch * CHUNK, (ch + 1) * CHUNK)

    def qk_slice(h):
        return slice(h * ML_QK_DIM, (h + 1) * ML_QK_DIM)

    def v_slice(h):
        return slice(h * ML_V_DIM, (h + 1) * ML_V_DIM)

    n_items = len(items)

    lane = _iota((CHUNK, LANES), 1)

    def lane_rep(col):
        return jnp.broadcast_to(col, (CHUNK, LANES))

    def wide(a):
        return jnp.concatenate([a] * (ML_V_DIM // LANES), axis=1)

    b_col, b_last, d_mat, g_end, m_intra, m_loc, k_w, d_c, d_n, scores = ({} for _ in range(10))
    for it, (ch, h) in enumerate(items):
        rs = row_slice(ch)
        gc = gcol_ref[rs, :]
        gr = grow_ref[ch]
        li_c = lane_rep(jnp.sum(jnp.where(lane == h, gc, 0.0), axis=1, keepdims=True))
        b_c = lane_rep(jnp.sum(jnp.where(lane == ML_HEADS + h, gc, 0.0), axis=1, keepdims=True))
        li_r = gr[h:h + 1, :]
        b_r = gr[ML_HEADS + h:ML_HEADS + h + 1, :]
        b_col[it], b_last[it] = b_c, b_c[CHUNK - 1:CHUNK, :]
        d_mat[it] = jnp.where(causal, b_c[:, :CHUNK] - b_r + li_r, -jnp.inf)
        g_end[it] = b_last[it] - b_c + li_c
    for it in range(n_items):
        m_intra[it] = jnp.max(d_mat[it], axis=1, keepdims=True)
        m_loc[it] = jnp.max(g_end[it], axis=0, keepdims=True)
    for it, (ch, h) in enumerate(items):
        k_w[it] = (k_ref[row_slice(ch), qk_slice(h)].astype(F32)
                   * (k_scale * jnp.exp(g_end[it] - m_loc[it])))
    for it, (ch, h) in enumerate(items):
        rs = row_slice(ch)
        d_c[it] = _dot_tn(k_w[it].astype(BF16), v_ref[rs, v_slice(h)])
        d_n[it] = jnp.sum(k_w[it], axis=0, keepdims=True)
        scores[it] = _dot_nt(q_ref[rs, qk_slice(h)], k_ref[rs, qk_slice(h)]) * k_scale

    m_prev, n_prev = {}, {}
    m_run = [m_scr[h] for h in range(ML_HEADS)]
    n_run = [n_scr[h] for h in range(ML_HEADS)]
    for it, (ch, h) in enumerate(items):
        c_run = c_scr[h]
        cprev_scr[it] = c_run.astype(BF16)
        m_prev[it], n_prev[it] = m_run[h], n_run[h]
        m_new = jnp.maximum(b_last[it] + m_run[h], m_loc[it])
        a_old = jnp.exp(b_last[it] + m_run[h] - m_new)
        a_new = jnp.exp(m_loc[it] - m_new)
        c_scr[h] = wide(a_old) * c_run + wide(a_new) * d_c[it]
        n_run[h] = a_old * n_run[h] + a_new * d_n[it]
        m_run[h] = m_new
    for h in range(ML_HEADS):
        n_scr[h] = n_run[h]
        m_scr[h] = m_run[h]

    m_t, s_inter, p, num, den, hid, ms = ({} for _ in range(7))
    for it in range(n_items):
        m_inter = b_col[it] + m_prev[it]
        m_t[it] = jnp.maximum(m_inter, m_intra[it])
        s_inter[it] = jnp.exp(m_inter - m_t[it])
        p[it] = jnp.exp(d_mat[it] - m_t[it][:, :CHUNK]) * scores[it]
    for it, (ch, h) in enumerate(items):
        rs = row_slice(ch)
        q = q_ref[rs, qk_slice(h)]
        num[it] = (wide(s_inter[it]) * _dot(q, cprev_scr[it])
                   + _dot(p[it].astype(BF16), v_ref[rs, v_slice(h)]))
        den[it] = (s_inter[it] * jnp.sum(q.astype(F32) * n_prev[it], axis=1, keepdims=True)
                   + jnp.sum(p[it], axis=1, keepdims=True))
    for it in range(n_items):
        hid[it] = num[it] * wide(1.0 / jnp.maximum(jnp.abs(den[it]), jnp.exp(-m_t[it])))
        ms[it] = jnp.mean(hid[it] * hid[it], axis=1, keepdims=True)
    for it, (ch, h) in enumerate(items):
        rs, vs = row_slice(ch), v_slice(h)
        out = hid[it] * lax.rsqrt(ms[it] + NORM_EPS) * nw_ref[:, vs] * _sigmoid(o_ref[rs, vs].astype(F32))
        out_ref[rs, vs] = out.astype(out_ref.dtype)


def _mlstm(proj, gcol, grow, norm_w, *, batch, seq):
    tokens = proj.shape[0]
    rows = REC_ROWS
    steps = seq // rows
    row = lambda b, t: b * steps + t
    return pl.pallas_call(
        _mlstm_kernel,
        grid=(batch, steps),
        in_specs=[
            pl.BlockSpec((rows, ML_QK_WIDTH), lambda b, t: (row(b, t), 0)),
            pl.BlockSpec((rows, ML_QK_WIDTH), lambda b, t: (row(b, t), 1)),
            pl.BlockSpec((rows, ML_V_WIDTH), lambda b, t: (row(b, t), 1)),
            pl.BlockSpec((rows, ML_V_WIDTH), lambda b, t: (row(b, t), 2)),
            pl.BlockSpec((rows, LANES), lambda b, t: (row(b, t), 0)),
            pl.BlockSpec((rows // CHUNK, ML_GATE_ROWS, CHUNK), lambda b, t: (row(b, t), 0, 0)),
            pl.BlockSpec((1, ML_V_WIDTH), lambda b, t: (0, 0)),
        ],
        out_specs=pl.BlockSpec((rows, ML_V_WIDTH), lambda b, t: (row(b, t), 0)),
        out_shape=jax.ShapeDtypeStruct((tokens, ML_V_WIDTH), BF16),
        scratch_shapes=[
            pltpu.VMEM((ML_HEADS, ML_QK_DIM, ML_V_DIM), F32),
            pltpu.VMEM((ML_HEADS, 1, ML_QK_DIM), F32),
            pltpu.VMEM((ML_HEADS, 1, LANES), F32),
            pltpu.VMEM((ML_HEADS * rows // CHUNK, ML_QK_DIM, ML_V_DIM), BF16),
        ],
        compiler_params=_params("parallel", "arbitrary"),
        name="mlstm",
    )(proj, proj, proj, proj, gcol, grow, norm_w)


def _gdn_kernel(q_ref, k_ref, v_ref, z_ref, gcol_ref, grow_ref, nw_ref, out_ref,
                s_scr, wq_scr, attn_scr, kdt_scr, u_scr, eg_scr):
    rows = q_ref.shape[0]
    rep = GDN_V_HEADS // GDN_K_HEADS
    dh = GDN_HEAD_DIM
    n_kh = q_ref.shape[1] // dh
    n_vh = n_kh * rep
    n_ch = rows // CHUNK

    @pl.when(pl.program_id(1) == 0)
    def _():
        s_scr[...] = jnp.zeros_like(s_scr)

    ri = _iota((CHUNK, CHUNK), 0)
    ci = _iota((CHUNK, CHUNK), 1)
    causal = ri >= ci
    strict = ri > ci
    eye = jnp.where(ri == ci, 1.0, 0.0).astype(F32)
    lane = _iota((CHUNK, LANES), 1)
    merge_mask = {}
    s = 1
    while s < CHUNK:
        merge_mask[s] = (ri // s - ci // s == 1) & (ri // (2 * s) == ci // (2 * s))
        s *= 2

    kk, qk, qs, ks = {}, {}, {}, {}
    for ch in range(n_ch):
        rs = slice(ch * CHUNK, (ch + 1) * CHUNK)
        for kh in range(n_kh):
            hs = slice(kh * dh, (kh + 1) * dh)
            qb = q_ref[rs, hs]
            kb = k_ref[rs, hs]
            qs[ch, kh], ks[ch, kh] = qb.astype(F32), kb.astype(F32)
            kk[ch, kh] = _dot_nt(kb, kb)
            qk[ch, kh] = _dot_nt(qb, kb)

    items = [(ch, vh) for ch in range(n_ch) for vh in range(n_vh)]
    x, a_mats, rhs = {}, {}, {}
    for it, (ch, vh) in enumerate(items):
        rs = slice(ch * CHUNK, (ch + 1) * CHUNK)
        kh = vh // rep
        gc = gcol_ref[rs, :]
        beta_c = jnp.sum(jnp.where(lane == vh, gc, 0.0), axis=1, keepdims=True)
        g_c = jnp.sum(jnp.where(lane == GDN_V_HEADS + vh, gc, 0.0), axis=1, keepdims=True)
        g_r = grow_ref[ch, GDN_V_HEADS + vh:GDN_V_HEADS + vh + 1, :]
        g_last = g_r[:, CHUNK - 1:CHUNK]
        decay = jnp.exp(jnp.where(causal, g_c - g_r, -jnp.inf))
        a_mat = jnp.where(strict, kk[ch, kh] * beta_c * decay, 0.0)
        x[it] = eye - jnp.where(merge_mask[1], a_mat, 0.0)
        a_mats[it] = a_mat

        q, k = qs[ch, kh], ks[ch, kh]
        v = v_ref[rs, vh * dh:(vh + 1) * dh].astype(F32)
        e_g = jnp.exp(g_c)
        rhs[it] = jnp.concatenate([v * beta_c, k * (beta_c * e_g)], axis=1).astype(BF16)
        attn_scr[it] = (qk[ch, kh] * decay).astype(BF16)
        wq_scr[it, CHUNK:2 * CHUNK, :] = (q * e_g).astype(BF16)
        kdt_scr[it] = (k * jnp.exp(g_last - g_c)).T.astype(BF16)
        eg_scr[it] = jnp.broadcast_to(jnp.exp(g_last), (1, LANES))

    s = 2
    while s < CHUNK:
        for it in range(len(items)):
            nb = jnp.where(merge_mask[s], a_mats[it], 0.0).astype(BF16)
            y = _dot(nb, x[it].astype(BF16))
            x[it] = x[it] - _dot(x[it].astype(BF16), y.astype(BF16))
        s *= 2

    for it in range(len(items)):
        sol = _dot(x[it].astype(BF16), rhs[it])
        u_scr[it] = sol[:, :dh]
        wq_scr[it, 0:CHUNK, :] = sol[:, dh:].astype(BF16)

    state = [s_scr[vh] for vh in range(n_vh)]
    for ch in range(n_ch):
        rs = slice(ch * CHUNK, (ch + 1) * CHUNK)
        base = ch * n_vh
        sb = [state[vh].astype(BF16) for vh in range(n_vh)]
        r = [_dot(wq_scr[base + vh], sb[vh]) for vh in range(n_vh)]
        vnb = [(u_scr[base + vh] - r[vh][:CHUNK]).astype(BF16) for vh in range(n_vh)]
        state = [state[vh] * eg_scr[base + vh] + _dot(kdt_scr[base + vh], vnb[vh]) for vh in range(n_vh)]
        for vh in range(n_vh):
            out = r[vh][CHUNK:] + _dot(attn_scr[base + vh], vnb[vh])
            ms = jnp.mean(out * out, axis=1, keepdims=True)
            z = z_ref[rs, vh * dh:(vh + 1) * dh].astype(F32)
            gated = out * lax.rsqrt(ms + NORM_EPS) * nw_ref[...] * (z * _sigmoid(z))
            out_ref[rs, vh * dh:(vh + 1) * dh] = gated.astype(out_ref.dtype)
    for vh in range(n_vh):
        s_scr[vh] = state[vh]


def _gdn(proj, gcol, grow, norm_w, *, batch, seq):
    tokens = proj.shape[0]
    rows = GDN_ROWS
    steps = seq // rows
    n_items = GDN_V_HEADS * rows // CHUNK
    row = lambda b, t: b * steps + t
    return pl.pallas_call(
        _gdn_kernel,
        grid=(batch, steps),
        in_specs=[
            pl.BlockSpec((rows, GDN_KEY_WIDTH), lambda b, t: (row(b, t), 0)),
            pl.BlockSpec((rows, GDN_KEY_WIDTH), lambda b, t: (row(b, t), 1)),
            pl.BlockSpec((rows, GDN_VAL_WIDTH), lambda b, t: (row(b, t), 2 * GDN_KEY_WIDTH // GDN_VAL_WIDTH)),
            pl.BlockSpec((rows, GDN_VAL_WIDTH), lambda b, t: (row(b, t), GDN_CONV_DIM // GDN_VAL_WIDTH)),
            pl.BlockSpec((rows, LANES), lambda b, t: (row(b, t), 0)),
            pl.BlockSpec((rows // CHUNK, GDN_GATE_ROWS, CHUNK), lambda b, t: (row(b, t), 0, 0)),
            pl.BlockSpec((1, GDN_HEAD_DIM), lambda b, t: (0, 0)),
        ],
        out_specs=pl.BlockSpec((rows, GDN_VAL_WIDTH), lambda b, t: (row(b, t), 0)),
        out_shape=jax.ShapeDtypeStruct((tokens, GDN_VAL_WIDTH), BF16),
        scratch_shapes=[
            pltpu.VMEM((GDN_V_HEADS, GDN_HEAD_DIM, GDN_HEAD_DIM), F32),
            pltpu.VMEM((n_items, 2 * CHUNK, GDN_HEAD_DIM), BF16),
            pltpu.VMEM((n_items, CHUNK, CHUNK), BF16),
            pltpu.VMEM((n_items, GDN_HEAD_DIM, CHUNK), BF16),
            pltpu.VMEM((n_items, CHUNK, GDN_HEAD_DIM), F32),
            pltpu.VMEM((n_items, 1, LANES), F32),
        ],
        compiler_params=_params("parallel", "arbitrary"),
        name="gdn",
    )(proj, proj, proj, proj, gcol, grow, norm_w)


def _block_tail_kernel(a_ref, wo_ref, x_ref, g1_ref, sc_ref, sh_ref, g2_ref, wgu_ref, wd_ref, fw_ref, o_ref,
                       x1_scr, h_scr, act_scr, *, final_norm):
    d_ff = wd_ref.shape[0]
    x1 = x_ref[...] + g1_ref[0] * _dot(a_ref[...], wo_ref[...])
    x1_scr[...] = x1
    h_scr[...] = _modulated_norm(x1, sc_ref[0], sh_ref[0]).astype(BF16)
    for j in range(d_ff // FFN_TH):
        gate = _dot(h_scr[...], wgu_ref[:, j * FFN_TH:(j + 1) * FFN_TH])
        up = _dot(h_scr[...], wgu_ref[:, d_ff + j * FFN_TH:d_ff + (j + 1) * FFN_TH])
        act_scr[:, j * FFN_TH:(j + 1) * FFN_TH] = (gate * _sigmoid(gate) * up).astype(BF16)
    y = x1_scr[...] + g2_ref[0] * _dot(act_scr[...], wd_ref[...])
    if final_norm:
        ms = jnp.mean(y * y, axis=-1, keepdims=True)
        y = y * lax.rsqrt(ms + NORM_EPS) * fw_ref[...]
    o_ref[...] = y


def _block_tail(mixed, w_out, x2, g1, scale, shift, g2, w_gate_up, w_down, final_w, *, layer, seq, final_norm):
    tokens, d = x2.shape
    kdim = mixed.shape[1]
    d_ff = w_down.shape[1]
    tm = FFN_TM
    per_batch = seq // tm
    mod_spec = pl.BlockSpec((1, 1, d), lambda i: (i // per_batch, 0, 0))
    resident = functools.partial(pl.BlockSpec, index_map=lambda i: (0, 0), pipeline_mode=pl.Buffered(1))
    stacked = functools.partial(pl.BlockSpec, index_map=lambda i: (layer, 0, 0), pipeline_mode=pl.Buffered(1))
    return pl.pallas_call(
        functools.partial(_block_tail_kernel, final_norm=final_norm),
        grid=(tokens // tm,),
        in_specs=[
            pl.BlockSpec((tm, kdim), lambda i: (i, 0)),
            resident((kdim, d)),
            pl.BlockSpec((tm, d), lambda i: (i, 0)),
            mod_spec, mod_spec, mod_spec, mod_spec,
            stacked((None, d, 2 * d_ff)),
            stacked((None, d_ff, d)),
            resident((1, d)),
        ],
        out_specs=pl.BlockSpec((tm, d), lambda i: (i, 0)),
        out_shape=jax.ShapeDtypeStruct((tokens, d), F32),
        scratch_shapes=[pltpu.VMEM((tm, d), F32), pltpu.VMEM((tm, d), BF16), pltpu.VMEM((tm, d_ff), BF16)],
        compiler_params=_params("parallel"),
        name="block_tail",
    )(mixed, w_out, x2, g1, scale, shift, g2, w_gate_up, w_down, final_w)


def _pad_cast_kernel(wt_ref, o_ref, *, width):
    cols = wt_ref.shape[0]
    col = pl.program_id(1) * cols + _iota(wt_ref.shape, 0)
    o_ref[...] = jnp.where(col < width, wt_ref[...], 0.0).T.astype(o_ref.dtype)


def _padded_in_weights(w_in, n_main):
    layers, d, width = w_in.shape
    n_pad = n_main + LANES
    tiles = n_pad // LANES
    per_slab = max(t for t in range(1, WEIGHT_PREP_MAX_TILES + 1) if tiles % t == 0)
    cols = per_slab * LANES
    return pl.pallas_call(
        functools.partial(_pad_cast_kernel, width=width),
        grid=(layers, tiles // per_slab),
        in_specs=[pl.BlockSpec((None, cols, d), lambda l, j: (l, j, 0))],
        out_specs=pl.BlockSpec((None, d, cols), lambda l, j: (l, 0, j)),
        out_shape=jax.ShapeDtypeStruct((layers, d, n_pad), BF16),
        compiler_params=_params("parallel", "parallel"),
        name="pad_cast",
    )(jnp.swapaxes(w_in, 1, 2))


def _gate_vec(vec, offset, n_rows):
    count = vec.shape[0]
    col = jnp.zeros((1, LANES), F32).at[0, offset:offset + count].set(vec.astype(F32))
    row = jnp.zeros((n_rows, 1), F32).at[offset:offset + count, 0].set(vec.astype(F32))
    return col, row


def kernel(x, c, ada_w, ada_b, ml_w_in, ml_b_if, ml_norm_w, ml_w_out, gdn_w_in, gdn_conv_w, gdn_a_log,
           gdn_dt_bias, gdn_norm_w, gdn_w_out, ffn_w_gate_up, ffn_w_down, final_norm_w):
    batch, seq, d = x.shape
    depth = ada_w.shape[0]
    n_mixers = 2
    tokens = batch * seq
    x2 = x.reshape(tokens, d)

    c_rows = 8
    c_pad = jnp.pad(c, ((0, c_rows - batch), (0, 0)))
    mod = _adaln(c_pad, ada_w, ada_b)[:, :batch, :]

    w_gate_up = ffn_w_gate_up.astype(BF16)
    w_down = ffn_w_down.astype(BF16)
    for layer in range(depth):
        sh1, sc1, g1, sh2, sc2, g2 = [m.reshape(batch, 1, d) for m in jnp.split(mod[layer], 6, axis=-1)]
        jdx = layer // n_mixers
        if layer % n_mixers == 0:
            b_col, b_row = _gate_vec(ml_b_if[jdx], 0, ML_GATE_ROWS)
            proj, gcol, grow = _inproj(x2, sc1, sh1, _padded_in_weights(ml_w_in, ML_MAIN),
                                       b_col, b_row, mode="ml", layer=jdx, seq=seq)
            mixed = _mlstm(proj, gcol, grow, ml_norm_w[jdx].reshape(1, ML_V_WIDTH).astype(F32),
                           batch=batch, seq=seq)
            w_out = ml_w_out[jdx]
        else:
            b_col, b_row = _gate_vec(gdn_dt_bias[jdx], GDN_V_HEADS, GDN_GATE_ROWS)
            a_col, a_row = _gate_vec(gdn_a_log[jdx], GDN_V_HEADS, GDN_GATE_ROWS)
            proj, gcol, grow = _inproj(x2, sc1, sh1, _padded_in_weights(gdn_w_in, GDN_MAIN),
                                       b_col, b_row, (a_col, a_row, gdn_conv_w[jdx].astype(F32)),
                                       mode="gdn", layer=jdx, seq=seq)
            mixed = _gdn(proj, gcol, grow, gdn_norm_w[jdx].reshape(1, GDN_HEAD_DIM).astype(F32),
                         batch=batch, seq=seq)
            w_out = gdn_w_out[jdx]
        x2 = _block_tail(mixed, w_out.astype(BF16), x2, g1, sc2, sh2, g2,
                         w_gate_up, w_down, final_norm_w.reshape(1, d).astype(F32),
                         layer=layer, seq=seq, final_norm=(layer == depth - 1))
    return x2.reshape(batch, seq, d)
```

```python
import functools

import jax
import jax.numpy as jnp
from jax import lax
from jax.experimental import pallas as pl
from jax.experimental.pallas import tpu as pltpu

F32 = jnp.float32
BF16 = jnp.bfloat16
HIGHEST = lax.Precision.HIGHEST

LANES = 128
VMEM_LIMIT = 56 * 1024 * 1024

D_MODEL = 1024
CHUNK = 64
NORM_EPS = 1e-6

ML_HEADS = 4
ML_QK_DIM = 128
ML_V_DIM = 256
ML_QK_WIDTH = ML_HEADS * ML_QK_DIM
ML_V_WIDTH = ML_HEADS * ML_V_DIM
ML_MAIN = 2 * ML_QK_WIDTH + 2 * ML_V_WIDTH
ML_GATE_ROWS = 16
ML_GATE_CAP = 15.0

GDN_K_HEADS = 8
GDN_V_HEADS = 16
GDN_HEAD_DIM = 128
GDN_KEY_WIDTH = GDN_K_HEADS * GDN_HEAD_DIM
GDN_VAL_WIDTH = GDN_V_HEADS * GDN_HEAD_DIM
GDN_CONV_DIM = 2 * GDN_KEY_WIDTH + GDN_VAL_WIDTH
GDN_CONV_WIDTH = 4
GDN_MAIN = GDN_CONV_DIM + GDN_VAL_WIDTH
GDN_GATE_ROWS = 2 * GDN_V_HEADS

WEIGHT_PREP_MAX_TILES = 8
ADALN_TN = 2048
INPROJ_TM = 512
INPROJ_TN = 256
CONV_HALO = 8
PIPE_PIECES = 1
REC_ROWS = 512
GDN_ROWS = 256
FFN_TM = 512
FFN_TH = 256


def _params(*sem):
    return pltpu.CompilerParams(dimension_semantics=sem, vmem_limit_bytes=VMEM_LIMIT)


def _sigmoid(x):
    return 1.0 / (1.0 + jnp.exp(-x))


def _softplus(x):
    return jnp.maximum(x, 0.0) + jnp.log1p(jnp.exp(-jnp.abs(x)))


def _dot(a, b):
    return jnp.dot(a, b, preferred_element_type=F32)


def _dot_nt(a, b):
    return lax.dot_general(a, b, (((1,), (1,)), ((), ())), preferred_element_type=F32)


def _dot_tn(a, b):
    return lax.dot_general(a, b, (((0,), (0,)), ((), ())), preferred_element_type=F32)


def _iota(shape, dim):
    return lax.broadcasted_iota(jnp.int32, shape, dim)


def _adaln_kernel(c_ref, w_ref, b_ref, o_ref):
    c = c_ref[...]
    act = c * _sigmoid(c)
    o_ref[0] = jnp.dot(act, w_ref[0], precision=HIGHEST, preferred_element_type=F32) + b_ref[0]


def _adaln(c_pad, ada_w, ada_b):
    depth, d, n = ada_w.shape
    rows = c_pad.shape[0]
    tn = ADALN_TN
    return pl.pallas_call(
        _adaln_kernel,
        grid=(depth, n // tn),
        in_specs=[
            pl.BlockSpec((rows, d), lambda l, j: (0, 0)),
            pl.BlockSpec((1, d, tn), lambda l, j: (l, 0, j)),
            pl.BlockSpec((1, 1, tn), lambda l, j: (l, 0, j)),
        ],
        out_specs=pl.BlockSpec((1, rows, tn), lambda l, j: (l, 0, j)),
        out_shape=jax.ShapeDtypeStruct((depth, rows, n), F32),
        compiler_params=_params("parallel", "parallel"),
        name="adaln",
    )(c_pad, ada_w, ada_b.reshape(depth, 1, n))


def _modulated_norm(x, scale, shift):
    ms = jnp.mean(x * x, axis=-1, keepdims=True)
    return x * lax.rsqrt(ms + NORM_EPS) * (1.0 + scale) + shift


def _ml_gates(pre, idx):
    capped = ML_GATE_CAP * jnp.tanh(pre / ML_GATE_CAP)
    log_f = jnp.minimum(capped, 0.0) - jnp.log1p(jnp.exp(-jnp.abs(capped)))
    is_f = idx >= ML_HEADS
    return jnp.where(is_f, log_f, capped), is_f


def _gdn_gates(pre, a_log, idx):
    is_g = idx >= GDN_V_HEADS
    return jnp.where(is_g, -jnp.exp(a_log) * _softplus(pre), _sigmoid(pre)), is_g


def _inproj_kernel(*refs, mode, per_batch):
    if mode == "ml":
        (x_ref, sc_ref, sh_ref, w_ref, bcol_ref, brow_ref,
         proj_ref, gcol_ref, grow_ref, h_scr) = refs
    else:
        (x_ref, sc_ref, sh_ref, w_ref, bcol_ref, brow_ref, acol_ref, arow_ref, cw_ref,
         proj_ref, gcol_ref, grow_ref, h_scr, halo_scr, conv_buf) = refs
    tm = x_ref.shape[0]
    n = proj_ref.shape[1]
    n_rows = brow_ref.shape[0]
    tn = INPROJ_TN

    h_scr[...] = _modulated_norm(x_ref[...], sc_ref[0], sh_ref[0]).astype(BF16)

    def gate_stages():
        chunks = [slice(ch * CHUNK, (ch + 1) * CHUNK) for ch in range(tm // CHUNK)]
        lane = _iota((CHUNK, LANES), 1)
        row = _iota((n_rows, CHUNK), 0)
        rr = _iota((CHUNK, CHUNK), 0)
        cc = _iota((CHUNK, CHUNK), 1)
        lower = jnp.where(cc <= rr, 1.0, 0.0).astype(F32)
        upper = jnp.where(rr <= cc, 1.0, 0.0).astype(F32)
        wg = w_ref[:, n:n + LANES]
        wg_t = wg.astype(F32).T[:n_rows, :].astype(BF16)
        pre = _dot(h_scr[...], wg) + bcol_ref[...]
        pre_t = [_dot_nt(wg_t, h_scr[rs, :]) + brow_ref[...] for rs in chunks]
        yield
        if mode == "ml":
            col = [_ml_gates(pre[rs, :], lane) for rs in chunks]
        else:
            col = [_gdn_gates(pre[rs, :], acol_ref[...], lane) for rs in chunks]
        yield
        if mode == "ml":
            rowg = [_ml_gates(p_t, row) for p_t in pre_t]
        else:
            rowg = [_gdn_gates(p_t, arow_ref[...], row) for p_t in pre_t]
        yield
        csum = [jnp.dot(lower, vals, precision=HIGHEST, preferred_element_type=F32) for vals, _ in col]
        csum_t = [jnp.dot(vals_t, upper, precision=HIGHEST, preferred_element_type=F32) for vals_t, _ in rowg]
        yield
        for ch, rs in enumerate(chunks):
            gcol_ref[rs, :] = jnp.where(col[ch][1], csum[ch], col[ch][0])
            grow_ref[ch] = jnp.where(rowg[ch][1], csum_t[ch], rowg[ch][0])
        yield

    stages = gate_stages()
    order = list(range(n // tn))
    if mode == "gdn":
        first_of_seq = pl.program_id(0) % per_batch == 0
        conv_slabs = [j for j in order if j * tn < GDN_CONV_DIM]
        plain_slabs = [j for j in order if j * tn >= GDN_CONV_DIM]
        order = []
        while conv_slabs or plain_slabs:
            order += conv_slabs[:2] + plain_slabs[:1]
            conv_slabs, plain_slabs = conv_slabs[2:], plain_slabs[1:]
    d = x_ref.shape[1]
    kp, rp = d // PIPE_PIECES, tm // PIPE_PIECES

    def slab_cols(j):
        return slice(j * tn, (j + 1) * tn)

    def matmul_piece(j, kk):
        return _dot(h_scr[:, kk * kp:(kk + 1) * kp], w_ref[kk * kp:(kk + 1) * kp, slab_cols(j)])

    def start_epilogue(j, y):
        if mode == "gdn" and j * tn < GDN_CONV_DIM:
            halo_ref = halo_scr.at[:, slab_cols(j)]
            conv_buf[0:CONV_HALO, :] = jnp.where(first_of_seq, 0.0, halo_ref[...])
            conv_buf[CONV_HALO:CONV_HALO + tm, :] = y
            halo_ref[...] = y[tm - CONV_HALO:tm, :]

    def epilogue_piece(j, y, r):
        cs, rows = slab_cols(j), slice(r * rp, (r + 1) * rp)
        if not (mode == "gdn" and j * tn < GDN_CONV_DIM):
            proj_ref[rows, cs] = y[rows, :].astype(proj_ref.dtype)
            return
        conv_w = cw_ref[:, cs]
        acc = conv_w[GDN_CONV_WIDTH - 1:GDN_CONV_WIDTH, :] * y[rows, :]
        for tap in range(GDN_CONV_WIDTH - 1):
            first = CONV_HALO - (GDN_CONV_WIDTH - 1 - tap) + r * rp
            acc = acc + conv_w[tap:tap + 1, :] * conv_buf[first:first + rp, :]
        acc = acc * _sigmoid(acc)
        if j * tn >= 2 * GDN_KEY_WIDTH:
            proj_ref[rows, cs] = acc.astype(proj_ref.dtype)
            return
        q_scale = GDN_HEAD_DIM ** -0.5 if j * tn < GDN_KEY_WIDTH else 1.0
        for hh in range(tn // GDN_HEAD_DIM):
            hs = slice(hh * GDN_HEAD_DIM, (hh + 1) * GDN_HEAD_DIM)
            blk = acc[:, hs]
            ss = jnp.sum(blk * blk, axis=1, keepdims=True)
            proj_ref[rows, j * tn + hh * GDN_HEAD_DIM:j * tn + (hh + 1) * GDN_HEAD_DIM] = (
                blk * (lax.rsqrt(ss + NORM_EPS) * q_scale)).astype(proj_ref.dtype)

    y = _dot(h_scr[...], w_ref[:, slab_cols(order[0])])
    for idx, j in enumerate(order):
        next(stages, None)
        start_epilogue(j, y)
        following = order[idx + 1] if idx + 1 < len(order) else None
        y_next = None
        for piece in range(PIPE_PIECES):
            if following is not None:
                part = matmul_piece(following, piece)
                y_next = part if y_next is None else y_next + part
            epilogue_piece(j, y, piece)
        y = y_next
    for _ in stages:
        pass


def _inproj(x2, scale, shift, w_all, bias_col, bias_row, gdn_extra=None, *, mode, layer, seq):
    tokens, d = x2.shape
    n = ML_MAIN if mode == "ml" else GDN_MAIN
    n_rows = bias_row.shape[0]
    tm = INPROJ_TM
    per_batch = seq // tm
    const = lambda i: (0, 0)
    resident = functools.partial(pl.BlockSpec, index_map=const, pipeline_mode=pl.Buffered(1))
    in_specs = [
        pl.BlockSpec((tm, d), lambda i: (i, 0)),
        pl.BlockSpec((1, 1, d), lambda i: (i // per_batch, 0, 0)),
        pl.BlockSpec((1, 1, d), lambda i: (i // per_batch, 0, 0)),
        pl.BlockSpec((None,) + w_all.shape[1:], lambda i: (layer, 0, 0), pipeline_mode=pl.Buffered(1)),
        resident((1, LANES)),
        resident((n_rows, 1)),
    ]
    operands = [x2, scale, shift, w_all, bias_col, bias_row]
    scratch = [pltpu.VMEM((tm, d), BF16)]
    if mode == "gdn":
        alog_col, alog_row, conv_w = gdn_extra
        in_specs += [resident((1, LANES)), resident((n_rows, 1)), resident((GDN_CONV_WIDTH, GDN_CONV_DIM))]
        operands += [alog_col, alog_row, conv_w]
        scratch += [pltpu.VMEM((CONV_HALO, GDN_CONV_DIM), F32),
                    pltpu.VMEM((tm + CONV_HALO, INPROJ_TN), F32)]
    return pl.pallas_call(
        functools.partial(_inproj_kernel, mode=mode, per_batch=per_batch),
        grid=(tokens // tm,),
        in_specs=in_specs,
        out_specs=[
            pl.BlockSpec((tm, n), lambda i: (i, 0)),
            pl.BlockSpec((tm, LANES), lambda i: (i, 0)),
            pl.BlockSpec((tm // CHUNK, n_rows, CHUNK), lambda i: (i, 0, 0)),
        ],
        out_shape=[
            jax.ShapeDtypeStruct((tokens, n), BF16),
            jax.ShapeDtypeStruct((tokens, LANES), F32),
            jax.ShapeDtypeStruct((tokens // CHUNK, n_rows, CHUNK), F32),
        ],
        scratch_shapes=scratch,
        compiler_params=_params("arbitrary"),
        name="inproj_" + mode,
    )(*operands)


def _mlstm_kernel(q_ref, k_ref, v_ref, o_ref, gcol_ref, grow_ref, nw_ref, out_ref,
                  c_scr, n_scr, m_scr, cprev_scr):
    rows = q_ref.shape[0]

    @pl.when(pl.program_id(1) == 0)
    def _():
        c_scr[...] = jnp.zeros_like(c_scr)
        n_scr[...] = jnp.zeros_like(n_scr)
        m_scr[...] = jnp.zeros_like(m_scr)

    causal = _iota((CHUNK, CHUNK), 0) >= _iota((CHUNK, CHUNK), 1)
    k_scale = ML_QK_DIM ** -0.5
    n_ch = rows // CHUNK
    items = [(ch, h) for ch in range(n_ch) for h in range(ML_HEADS)]

    def row_slice(ch):
        return slice(ch * CHUNK, (ch + 1) * CHUNK)

    def qk_slice(h):
        return slice(h * ML_QK_DIM, (h + 1) * ML_QK_DIM)

    def v_slice(h):
        return slice(h * ML_V_DIM, (h + 1) * ML_V_DIM)

    n_items = len(items)

    lane = _iota((CHUNK, LANES), 1)

    def lane_rep(col):
        return jnp.broadcast_to(col, (CHUNK, LANES))

    def wide(a):
        return jnp.concatenate([a] * (ML_V_DIM // LANES), axis=1)

    b_col, b_last, d_mat, g_end, m_intra, m_loc, k_w, d_c, d_n, scores = ({} for _ in range(10))
    for it, (ch, h) in enumerate(items):
        rs = row_slice(ch)
        gc = gcol_ref[rs, :]
        gr = grow_ref[ch]
        li_c = lane_rep(jnp.sum(jnp.where(lane == h, gc, 0.0), axis=1, keepdims=True))
        b_c = lane_rep(jnp.sum(jnp.where(lane == ML_HEADS + h, gc, 0.0), axis=1, keepdims=True))
        li_r = gr[h:h + 1, :]
        b_r = gr[ML_HEADS + h:ML_HEADS + h + 1, :]
        b_col[it], b_last[it] = b_c, b_c[CHUNK - 1:CHUNK, :]
        d_mat[it] = jnp.where(causal, b_c[:, :CHUNK] - b_r + li_r, -jnp.inf)
        g_end[it] = b_last[it] - b_c + li_c
    for it in range(n_items):
        m_intra[it] = jnp.max(d_mat[it], axis=1, keepdims=True)
        m_loc[it] = jnp.max(g_end[it], axis=0, keepdims=True)
    for it, (ch, h) in enumerate(items):
        k_w[it] = (k_ref[row_slice(ch), qk_slice(h)].astype(F32)
                   * (k_scale * jnp.exp(g_end[it] - m_loc[it])))
    for it, (ch, h) in enumerate(items):
        rs = row_slice(ch)
        d_c[it] = _dot_tn(k_w[it].astype(BF16), v_ref[rs, v_slice(h)])
        d_n[it] = jnp.sum(k_w[it], axis=0, keepdims=True)
        scores[it] = _dot_nt(q_ref[rs, qk_slice(h)], k_ref[rs, qk_slice(h)]) * k_scale

    m_prev, n_prev = {}, {}
    m_run = [m_scr[h] for h in range(ML_HEADS)]
    n_run = [n_scr[h] for h in range(ML_HEADS)]
    for it, (ch, h) in enumerate(items):
        c_run = c_scr[h]
        cprev_scr[it] = c_run.astype(BF16)
        m_prev[it], n_prev[it] = m_run[h], n_run[h]
        m_new = jnp.maximum(b_last[it] + m_run[h], m_loc[it])
        a_old = jnp.exp(b_last[it] + m_run[h] - m_new)
        a_new = jnp.exp(m_loc[it] - m_new)
        c_scr[h] = wide(a_old) * c_run + wide(a_new) * d_c[it]
        n_run[h] = a_old * n_run[h] + a_new * d_n[it]
        m_run[h] = m_new
    for h in range(ML_HEADS):
        n_scr[h] = n_run[h]
        m_scr[h] = m_run[h]

    m_t, s_inter, p, num, den, hid, ms = ({} for _ in range(7))
    for it in range(n_items):
        m_inter = b_col[it] + m_prev[it]
        m_t[it] = jnp.maximum(m_inter, m_intra[it])
        s_inter[it] = jnp.exp(m_inter - m_t[it])
        p[it] = jnp.exp(d_mat[it] - m_t[it][:, :CHUNK]) * scores[it]
    for it, (ch, h) in enumerate(items):
        rs = row_slice(ch)
        q = q_ref[rs, qk_slice(h)]
        num[it] = (wide(s_inter[it]) * _dot(q, cprev_scr[it])
                   + _dot(p[it].astype(BF16), v_ref[rs, v_slice(h)]))
        den[it] = (s_inter[it] * jnp.sum(q.astype(F32) * n_prev[it], axis=1, keepdims=True)
                   + jnp.sum(p[it], axis=1, keepdims=True))
    for it in range(n_items):
        hid[it] = num[it] * wide(1.0 / jnp.maximum(jnp.abs(den[it]), jnp.exp(-m_t[it])))
        ms[it] = jnp.mean(hid[it] * hid[it], axis=1, keepdims=True)
    for it, (ch, h) in enumerate(items):
        rs, vs = row_slice(ch), v_slice(h)
        out = hid[it] * lax.rsqrt(ms[it] + NORM_EPS) * nw_ref[:, vs] * _sigmoid(o_ref[rs, vs].astype(F32))
        out_ref[rs, vs] = out.astype(out_ref.dtype)


def _mlstm(proj, gcol, grow, norm_w, *, batch, seq):
    tokens = proj.shape[0]
    rows = REC_ROWS
    steps = seq // rows
    row = lambda b, t: b * steps + t
    return pl.pallas_call(
        _mlstm_kernel,
        grid=(batch, steps),
        in_specs=[
            pl.BlockSpec((rows, ML_QK_WIDTH), lambda b, t: (row(b, t), 0)),
            pl.BlockSpec((rows, ML_QK_WIDTH), lambda b, t: (row(b, t), 1)),
            pl.BlockSpec((rows, ML_V_WIDTH), lambda b, t: (row(b, t), 1)),
            pl.BlockSpec((rows, ML_V_WIDTH), lambda b, t: (row(b, t), 2)),
            pl.BlockSpec((rows, LANES), lambda b, t: (row(b, t), 0)),
            pl.BlockSpec((rows // CHUNK, ML_GATE_ROWS, CHUNK), lambda b, t: (row(b, t), 0, 0)),
            pl.BlockSpec((1, ML_V_WIDTH), lambda b, t: (0, 0)),
        ],
        out_specs=pl.BlockSpec((rows, ML_V_WIDTH), lambda b, t: (row(b, t), 0)),
        out_shape=jax.ShapeDtypeStruct((tokens, ML_V_WIDTH), BF16),
        scratch_shapes=[
            pltpu.VMEM((ML_HEADS, ML_QK_DIM, ML_V_DIM), F32),
            pltpu.VMEM((ML_HEADS, 1, ML_QK_DIM), F32),
            pltpu.VMEM((ML_HEADS, 1, LANES), F32),
            pltpu.VMEM((ML_HEADS * rows // CHUNK, ML_QK_DIM, ML_V_DIM), BF16),
        ],
        compiler_params=_params("parallel", "arbitrary"),
        name="mlstm",
    )(proj, proj, proj, proj, gcol, grow, norm_w)


def _gdn_kernel(q_ref, k_ref, v_ref, z_ref, gcol_ref, grow_ref, nw_ref, out_ref,
                s_scr, wq_scr, attn_scr, kdt_scr, u_scr, eg_scr):
    rows = q_ref.shape[0]
    rep = GDN_V_HEADS // GDN_K_HEADS
    dh = GDN_HEAD_DIM
    n_kh = q_ref.shape[1] // dh
    n_vh = n_kh * rep
    n_ch = rows // CHUNK

    @pl.when(pl.program_id(1) == 0)
    def _():
        s_scr[...] = jnp.zeros_like(s_scr)

    ri = _iota((CHUNK, CHUNK), 0)
    ci = _iota((CHUNK, CHUNK), 1)
    causal = ri >= ci
    strict = ri > ci
    eye = jnp.where(ri == ci, 1.0, 0.0).astype(F32)
    lane = _iota((CHUNK, LANES), 1)
    merge_mask = {}
    s = 1
    while s < CHUNK:
        merge_mask[s] = (ri // s - ci // s == 1) & (ri // (2 * s) == ci // (2 * s))
        s *= 2

    kk, qk, qs, ks = {}, {}, {}, {}
    for ch in range(n_ch):
        rs = slice(ch * CHUNK, (ch + 1) * CHUNK)
        for kh in range(n_kh):
            hs = slice(kh * dh, (kh + 1) * dh)
            qb = q_ref[rs, hs]
            kb = k_ref[rs, hs]
            qs[ch, kh], ks[ch, kh] = qb.astype(F32), kb.astype(F32)
            kk[ch, kh] = _dot_nt(kb, kb)
            qk[ch, kh] = _dot_nt(qb, kb)

    items = [(ch, vh) for ch in range(n_ch) for vh in range(n_vh)]
    x, a_mats, rhs = {}, {}, {}
    for it, (ch, vh) in enumerate(items):
        rs = slice(ch * CHUNK, (ch + 1) * CHUNK)
        kh = vh // rep
        gc = gcol_ref[rs, :]
        beta_c = jnp.sum(jnp.where(lane == vh, gc, 0.0), axis=1, keepdims=True)
        g_c = jnp.sum(jnp.where(lane == GDN_V_HEADS + vh, gc, 0.0), axis=1, keepdims=True)
        g_r = grow_ref[ch, GDN_V_HEADS + vh:GDN_V_HEADS + vh + 1, :]
        g_last = g_r[:, CHUNK - 1:CHUNK]
        decay = jnp.exp(jnp.where(causal, g_c - g_r, -jnp.inf))
        a_mat = jnp.where(strict, kk[ch, kh] * beta_c * decay, 0.0)
        x[it] = eye - jnp.where(merge_mask[1], a_mat, 0.0)
        a_mats[it] = a_mat

        q, k = qs[ch, kh], ks[ch, kh]
        v = v_ref[rs, vh * dh:(vh + 1) * dh].astype(F32)
        e_g = jnp.exp(g_c)
        rhs[it] = jnp.concatenate([v * beta_c, k * (beta_c * e_g)], axis=1).astype(BF16)
        attn_scr[it] = (qk[ch, kh] * decay).astype(BF16)
        wq_scr[it, CHUNK:2 * CHUNK, :] = (q * e_g).astype(BF16)
        kdt_scr[it] = (k * jnp.exp(g_last - g_c)).T.astype(BF16)
        eg_scr[it] = jnp.broadcast_to(jnp.exp(g_last), (1, LANES))

    s = 2
    while s < CHUNK:
        for it in range(len(items)):
            nb = jnp.where(merge_mask[s], a_mats[it], 0.0).astype(BF16)
            y = _dot(nb, x[it].astype(BF16))
            x[it] = x[it] - _dot(x[it].astype(BF16), y.astype(BF16))
        s *= 2

    for it in range(len(items)):
        sol = _dot(x[it].astype(BF16), rhs[it])
        u_scr[it] = sol[:, :dh]
        wq_scr[it, 0:CHUNK, :] = sol[:, dh:].astype(BF16)

    state = [s_scr[vh] for vh in range(n_vh)]
    for ch in range(n_ch):
        rs = slice(ch * CHUNK, (ch + 1) * CHUNK)
        base = ch * n_vh
        sb = [state[vh].astype(BF16) for vh in range(n_vh)]
        r = [_dot(wq_scr[base + vh], sb[vh]) for vh in range(n_vh)]
        vnb = [(u_scr[base + vh] - r[vh][:CHUNK]).astype(BF16) for vh in range(n_vh)]
        state = [state[vh] * eg_scr[base + vh] + _dot(kdt_scr[base + vh], vnb[vh]) for vh in range(n_vh)]
        for vh in range(n_vh):
            out = r[vh][CHUNK:] + _dot(attn_scr[base + vh], vnb[vh])
            ms = jnp.mean(out * out, axis=1, keepdims=True)
            z = z_ref[rs, vh * dh:(vh + 1) * dh].astype(F32)
            gated = out * lax.rsqrt(ms + NORM_EPS) * nw_ref[...] * (z * _sigmoid(z))
            out_ref[rs, vh * dh:(vh + 1) * dh] = gated.astype(out_ref.dtype)
    for vh in range(n_vh):
        s_scr[vh] = state[vh]


def _gdn(proj, gcol, grow, norm_w, *, batch, seq):
    tokens = proj.shape[0]
    rows = GDN_ROWS
    steps = seq // rows
    n_items = GDN_V_HEADS * rows // CHUNK
    row = lambda b, t: b * steps + t
    return pl.pallas_call(
        _gdn_kernel,
        grid=(batch, steps),
        in_specs=[
            pl.BlockSpec((rows, GDN_KEY_WIDTH), lambda b, t: (row(b, t), 0)),
            pl.BlockSpec((rows, GDN_KEY_WIDTH), lambda b, t: (row(b, t), 1)),
            pl.BlockSpec((rows, GDN_VAL_WIDTH), lambda b, t: (row(b, t), 2 * GDN_KEY_WIDTH // GDN_VAL_WIDTH)),
            pl.BlockSpec((rows, GDN_VAL_WIDTH), lambda b, t: (row(b, t), GDN_CONV_DIM // GDN_VAL_WIDTH)),
            pl.BlockSpec((rows, LANES), lambda b, t: (row(b, t), 0)),
            pl.BlockSpec((rows // CHUNK, GDN_GATE_ROWS, CHUNK), lambda b, t: (row(b, t), 0, 0)),
            pl.BlockSpec((1, GDN_HEAD_DIM), lambda b, t: (0, 0)),
        ],
        out_specs=pl.BlockSpec((rows, GDN_VAL_WIDTH), lambda b, t: (row(b, t), 0)),
        out_shape=jax.ShapeDtypeStruct((tokens, GDN_VAL_WIDTH), BF16),
        scratch_shapes=[
            pltpu.VMEM((GDN_V_HEADS, GDN_HEAD_DIM, GDN_HEAD_DIM), F32),
            pltpu.VMEM((n_items, 2 * CHUNK, GDN_HEAD_DIM), BF16),
            pltpu.VMEM((n_items, CHUNK, CHUNK), BF16),
            pltpu.VMEM((n_items, GDN_HEAD_DIM, CHUNK), BF16),
            pltpu.VMEM((n_items, CHUNK, GDN_HEAD_DIM), F32),
            pltpu.VMEM((n_items, 1, LANES), F32),
        ],
        compiler_params=_params("parallel", "arbitrary"),
        name="gdn",
    )(proj, proj, proj, proj, gcol, grow, norm_w)


def _block_tail_kernel(a_ref, wo_ref, x_ref, g1_ref, sc_ref, sh_ref, g2_ref, wgu_ref, wd_ref, fw_ref, o_ref,
                       x1_scr, h_scr, act_scr, *, final_norm):
    d_ff = wd_ref.shape[0]
    x1 = x_ref[...] + g1_ref[0] * _dot(a_ref[...], wo_ref[...])
    x1_scr[...] = x1
    h_scr[...] = _modulated_norm(x1, sc_ref[0], sh_ref[0]).astype(BF16)
    for j in range(d_ff // FFN_TH):
        gate = _dot(h_scr[...], wgu_ref[:, j * FFN_TH:(j + 1) * FFN_TH])
        up = _dot(h_scr[...], wgu_ref[:, d_ff + j * FFN_TH:d_ff + (j + 1) * FFN_TH])
        act_scr[:, j * FFN_TH:(j + 1) * FFN_TH] = (gate * _sigmoid(gate) * up).astype(BF16)
    y = x1_scr[...] + g2_ref[0] * _dot(act_scr[...], wd_ref[...])
    if final_norm:
        ms = jnp.mean(y * y, axis=-1, keepdims=True)
        y = y * lax.rsqrt(ms + NORM_EPS) * fw_ref[...]
    o_ref[...] = y


def _block_tail(mixed, w_out, x2, g1, scale, shift, g2, w_gate_up, w_down, final_w, *, layer, seq, final_norm):
    tokens, d = x2.shape
    kdim = mixed.shape[1]
    d_ff = w_down.shape[1]
    tm = FFN_TM
    per_batch = seq // tm
    mod_spec = pl.BlockSpec((1, 1, d), lambda i: (i // per_batch, 0, 0))
    resident = functools.partial(pl.BlockSpec, index_map=lambda i: (0, 0), pipeline_mode=pl.Buffered(1))
    stacked = functools.partial(pl.BlockSpec, index_map=lambda i: (layer, 0, 0), pipeline_mode=pl.Buffered(1))
    return pl.pallas_call(
        functools.partial(_block_tail_kernel, final_norm=final_norm),
        grid=(tokens // tm,),
        in_specs=[
            pl.BlockSpec((tm, kdim), lambda i: (i, 0)),
            resident((kdim, d)),
            pl.BlockSpec((tm, d), lambda i: (i, 0)),
            mod_spec, mod_spec, mod_spec, mod_spec,
            stacked((None, d, 2 * d_ff)),
            stacked((None, d_ff, d)),
            resident((1, d)),
        ],
        out_specs=pl.BlockSpec((tm, d), lambda i: (i, 0)),
        out_shape=jax.ShapeDtypeStruct((tokens, d), F32),
        scratch_shapes=[pltpu.VMEM((tm, d), F32), pltpu.VMEM((tm, d), BF16), pltpu.VMEM((tm, d_ff), BF16)],
        compiler_params=_params("parallel"),
        name="block_tail",
    )(mixed, w_out, x2, g1, scale, shift, g2, w_gate_up, w_down, final_w)


def _pad_cast_kernel(wt_ref, o_ref, *, width):
    cols = wt_ref.shape[0]
    col = pl.program_id(1) * cols + _iota(wt_ref.shape, 0)
    o_ref[...] = jnp.where(col < width, wt_ref[...], 0.0).T.astype(o_ref.dtype)


def _padded_in_weights(w_in, n_main):
    layers, d, width = w_in.shape
    n_pad = n_main + LANES
    tiles = n_pad // LANES
    per_slab = max(t for t in range(1, WEIGHT_PREP_MAX_TILES + 1) if tiles % t == 0)
    cols = per_slab * LANES
    return pl.pallas_call(
        functools.partial(_pad_cast_kernel, width=width),
        grid=(layers, tiles // per_slab),
        in_specs=[pl.BlockSpec((None, cols, d), lambda l, j: (l, j, 0))],
        out_specs=pl.BlockSpec((None, d, cols), lambda l, j: (l, 0, j)),
        out_shape=jax.ShapeDtypeStruct((layers, d, n_pad), BF16),
        compiler_params=_params("parallel", "parallel"),
        name="pad_cast",
    )(jnp.swapaxes(w_in, 1, 2))


def _gate_vec(vec, offset, n_rows):
    count = vec.shape[0]
    col = jnp.zeros((1, LANES), F32).at[0, offset:offset + count].set(vec.astype(F32))
    row = jnp.zeros((n_rows, 1), F32).at[offset:offset + count, 0].set(vec.astype(F32))
    return col, row


def kernel(x, c, ada_w, ada_b, ml_w_in, ml_b_if, ml_norm_w, ml_w_out, gdn_w_in, gdn_conv_w, gdn_a_log,
           gdn_dt_bias, gdn_norm_w, gdn_w_out, ffn_w_gate_up, ffn_w_down, final_norm_w):
    batch, seq, d = x.shape
    depth = ada_w.shape[0]
    n_mixers = 2
    tokens = batch * seq
    x2 = x.reshape(tokens, d)

    c_rows = 8
    c_pad = jnp.pad(c, ((0, c_rows - batch), (0, 0)))
    mod = _adaln(c_pad, ada_w, ada_b)[:, :batch, :]

    w_gate_up = ffn_w_gate_up.astype(BF16)
    w_down = ffn_w_down.astype(BF16)
    for layer in range(depth):
        sh1, sc1, g1, sh2, sc2, g2 = [m.reshape(batch, 1, d) for m in jnp.split(mod[layer], 6, axis=-1)]
        jdx = layer // n_mixers
        if layer % n_mixers == 0:
            b_col, b_row = _gate_vec(ml_b_if[jdx], 0, ML_GATE_ROWS)
            proj, gcol, grow = _inproj(x2, sc1, sh1, _padded_in_weights(ml_w_in, ML_MAIN),
                                       b_col, b_row, mode="ml", layer=jdx, seq=seq)
            mixed = _mlstm(proj, gcol, grow, ml_norm_w[jdx].reshape(1, ML_V_WIDTH).astype(F32),
                           batch=batch, seq=seq)
            w_out = ml_w_out[jdx]
        else:
            b_col, b_row = _gate_vec(gdn_dt_bias[jdx], GDN_V_HEADS, GDN_GATE_ROWS)
            a_col, a_row = _gate_vec(gdn_a_log[jdx], GDN_V_HEADS, GDN_GATE_ROWS)
            proj, gcol, grow = _inproj(x2, sc1, sh1, _padded_in_weights(gdn_w_in, GDN_MAIN),
                                       b_col, b_row, (a_col, a_row, gdn_conv_w[jdx].astype(F32)),
                                       mode="gdn", layer=jdx, seq=seq)
            mixed = _gdn(proj, gcol, grow, gdn_norm_w[jdx].reshape(1, GDN_HEAD_DIM).astype(F32),
                         batch=batch, seq=seq)
            w_out = gdn_w_out[jdx]
        x2 = _block_tail(mixed, w_out.astype(BF16), x2, g1, sc2, sh2, g2,
                         w_gate_up, w_down, final_norm_w.reshape(1, d).astype(F32),
                         layer=layer, seq=seq, final_norm=(layer == depth - 1))
    return x2.reshape(batch, seq, d)
```

```python
import functools

import jax
import jax.numpy as jnp
from jax import lax
from jax.experimental import pallas as pl
from jax.experimental.pallas import tpu as pltpu

F32 = jnp.float32
BF16 = jnp.bfloat16
HIGHEST = lax.Precision.HIGHEST

LANES = 128
VMEM_LIMIT = 56 * 1024 * 1024

D_MODEL = 1024
CHUNK = 64
NORM_EPS = 1e-6

ML_HEADS = 4
ML_QK_DIM = 128
ML_V_DIM = 256
ML_QK_WIDTH = ML_HEADS * ML_QK_DIM
ML_V_WIDTH = ML_HEADS * ML_V_DIM
ML_MAIN = 2 * ML_QK_WIDTH + 2 * ML_V_WIDTH
ML_GATE_ROWS = 16
ML_GATE_CAP = 15.0

GDN_K_HEADS = 8
GDN_V_HEADS = 16
GDN_HEAD_DIM = 128
GDN_KEY_WIDTH = GDN_K_HEADS * GDN_HEAD_DIM
GDN_VAL_WIDTH = GDN_V_HEADS * GDN_HEAD_DIM
GDN_CONV_DIM = 2 * GDN_KEY_WIDTH + GDN_VAL_WIDTH
GDN_CONV_WIDTH = 4
GDN_MAIN = GDN_CONV_DIM + GDN_VAL_WIDTH
GDN_GATE_ROWS = 2 * GDN_V_HEADS

WEIGHT_PREP_MAX_TILES = 8
ADALN_TN = 2048
INPROJ_TM = 512
INPROJ_TN = 256
CONV_HALO = 8
PIPE_PIECES = 1
REC_ROWS = 512
GDN_ROWS = 256
FFN_TM = 512
FFN_TH = 256


def _params(*sem):
    return pltpu.CompilerParams(dimension_semantics=sem, vmem_limit_bytes=VMEM_LIMIT)


def _sigmoid(x):
    return 1.0 / (1.0 + jnp.exp(-x))


def _softplus(x):
    return jnp.maximum(x, 0.0) + jnp.log1p(jnp.exp(-jnp.abs(x)))


def _dot(a, b):
    return jnp.dot(a, b, preferred_element_type=F32)


def _dot_nt(a, b):
    return lax.dot_general(a, b, (((1,), (1,)), ((), ())), preferred_element_type=F32)


def _dot_tn(a, b):
    return lax.dot_general(a, b, (((0,), (0,)), ((), ())), preferred_element_type=F32)


def _iota(shape, dim):
    return lax.broadcasted_iota(jnp.int32, shape, dim)


def _adaln_kernel(c_ref, w_ref, b_ref, o_ref):
    c = c_ref[...]
    act = c * _sigmoid(c)
    o_ref[0] = jnp.dot(act, w_ref[0], precision=HIGHEST, preferred_element_type=F32) + b_ref[0]


def _adaln(c_pad, ada_w, ada_b):
    depth, d, n = ada_w.shape
    rows = c_pad.shape[0]
    tn = ADALN_TN
    return pl.pallas_call(
        _adaln_kernel,
        grid=(depth, n // tn),
        in_specs=[
            pl.BlockSpec((rows, d), lambda l, j: (0, 0)),
            pl.BlockSpec((1, d, tn), lambda l, j: (l, 0, j)),
            pl.BlockSpec((1, 1, tn), lambda l, j: (l, 0, j)),
        ],
        out_specs=pl.BlockSpec((1, rows, tn), lambda l, j: (l, 0, j)),
        out_shape=jax.ShapeDtypeStruct((depth, rows, n), F32),
        compiler_params=_params("parallel", "parallel"),
        name="adaln",
    )(c_pad, ada_w, ada_b.reshape(depth, 1, n))


def _modulated_norm(x, scale, shift):
    ms = jnp.mean(x * x, axis=-1, keepdims=True)
    return x * lax.rsqrt(ms + NORM_EPS) * (1.0 + scale) + shift


def _ml_gates(pre, idx):
    capped = ML_GATE_CAP * jnp.tanh(pre / ML_GATE_CAP)
    log_f = jnp.minimum(capped, 0.0) - jnp.log1p(jnp.exp(-jnp.abs(capped)))
    is_f = idx >= ML_HEADS
    return jnp.where(is_f, log_f, capped), is_f


def _gdn_gates(pre, a_log, idx):
    is_g = idx >= GDN_V_HEADS
    return jnp.where(is_g, -jnp.exp(a_log) * _softplus(pre), _sigmoid(pre)), is_g


def _inproj_kernel(*refs, mode, per_batch):
    if mode == "ml":
        (x_ref, sc_ref, sh_ref, w_ref, bcol_ref, brow_ref,
         proj_ref, gcol_ref, grow_ref, h_scr) = refs
    else:
        (x_ref, sc_ref, sh_ref, w_ref, bcol_ref, brow_ref, acol_ref, arow_ref, cw_ref,
         proj_ref, gcol_ref, grow_ref, h_scr, halo_scr, conv_buf) = refs
    tm = x_ref.shape[0]
    n = proj_ref.shape[1]
    n_rows = brow_ref.shape[0]
    tn = INPROJ_TN

    h_scr[...] = _modulated_norm(x_ref[...], sc_ref[0], sh_ref[0]).astype(BF16)

    def gate_stages():
        chunks = [slice(ch * CHUNK, (ch + 1) * CHUNK) for ch in range(tm // CHUNK)]
        lane = _iota((CHUNK, LANES), 1)
        row = _iota((n_rows, CHUNK), 0)
        rr = _iota((CHUNK, CHUNK), 0)
        cc = _iota((CHUNK, CHUNK), 1)
        lower = jnp.where(cc <= rr, 1.0, 0.0).astype(F32)
        upper = jnp.where(rr <= cc, 1.0, 0.0).astype(F32)
        wg = w_ref[:, n:n + LANES]
        wg_t = wg.astype(F32).T[:n_rows, :].astype(BF16)
        pre = _dot(h_scr[...], wg) + bcol_ref[...]
        pre_t = [_dot_nt(wg_t, h_scr[rs, :]) + brow_ref[...] for rs in chunks]
        yield
        if mode == "ml":
            col = [_ml_gates(pre[rs, :], lane) for rs in chunks]
        else:
            col = [_gdn_gates(pre[rs, :], acol_ref[...], lane) for rs in chunks]
        yield
        if mode == "ml":
            rowg = [_ml_gates(p_t, row) for p_t in pre_t]
        else:
            rowg = [_gdn_gates(p_t, arow_ref[...], row) for p_t in pre_t]
        yield
        csum = [jnp.dot(lower, vals, precision=HIGHEST, preferred_element_type=F32) for vals, _ in col]
        csum_t = [jnp.dot(vals_t, upper, precision=HIGHEST, preferred_element_type=F32) for vals_t, _ in rowg]
        yield
        for ch, rs in enumerate(chunks):
            gcol_ref[rs, :] = jnp.where(col[ch][1], csum[ch], col[ch][0])
            grow_ref[ch] = jnp.where(rowg[ch][1], csum_t[ch], rowg[ch][0])
        yield

    stages = gate_stages()
    order = list(range(n // tn))
    if mode == "gdn":
        first_of_seq = pl.program_id(0) % per_batch == 0
        conv_slabs = [j for j in order if j * tn < GDN_CONV_DIM]
        plain_slabs = [j for j in order if j * tn >= GDN_CONV_DIM]
        order = []
        while conv_slabs or plain_slabs:
            order += conv_slabs[:2] + plain_slabs[:1]
            conv_slabs, plain_slabs = conv_slabs[2:], plain_slabs[1:]
    d = x_ref.shape[1]
    kp, rp = d // PIPE_PIECES, tm // PIPE_PIECES

    def slab_cols(j):
        return slice(j * tn, (j + 1) * tn)

    def matmul_piece(j, kk):
        return _dot(h_scr[:, kk * kp:(kk + 1) * kp], w_ref[kk * kp:(kk + 1) * kp, slab_cols(j)])

    def start_epilogue(j, y):
        if mode == "gdn" and j * tn < GDN_CONV_DIM:
            halo_ref = halo_scr.at[:, slab_cols(j)]
            conv_buf[0:CONV_HALO, :] = jnp.where(first_of_seq, 0.0, halo_ref[...])
            conv_buf[CONV_HALO:CONV_HALO + tm, :] = y
            halo_ref[...] = y[tm - CONV_HALO:tm, :]

    def epilogue_piece(j, y, r):
        cs, rows = slab_cols(j), slice(r * rp, (r + 1) * rp)
        if not (mode == "gdn" and j * tn < GDN_CONV_DIM):
            proj_ref[rows, cs] = y[rows, :].astype(proj_ref.dtype)
            return
        conv_w = cw_ref[:, cs]
        acc = conv_w[GDN_CONV_WIDTH - 1:GDN_CONV_WIDTH, :] * y[rows, :]
        for tap in range(GDN_CONV_WIDTH - 1):
            first = CONV_HALO - (GDN_CONV_WIDTH - 1 - tap) + r * rp
            acc = acc + conv_w[tap:tap + 1, :] * conv_buf[first:first + rp, :]
        acc = acc * _sigmoid(acc)
        if j * tn >= 2 * GDN_KEY_WIDTH:
            proj_ref[rows, cs] = acc.astype(proj_ref.dtype)
            return
        q_scale = GDN_HEAD_DIM ** -0.5 if j * tn < GDN_KEY_WIDTH else 1.0
        for hh in range(tn // GDN_HEAD_DIM):
            hs = slice(hh * GDN_HEAD_DIM, (hh + 1) * GDN_HEAD_DIM)
            blk = acc[:, hs]
            ss = jnp.sum(blk * blk, axis=1, keepdims=True)
            proj_ref[rows, j * tn + hh * GDN_HEAD_DIM:j * tn + (hh + 1) * GDN_HEAD_DIM] = (
                blk * (lax.rsqrt(ss + NORM_EPS) * q_scale)).astype(proj_ref.dtype)

    y = _dot(h_scr[...], w_ref[:, slab_cols(order[0])])
    for idx, j in enumerate(order):
        next(stages, None)
        start_epilogue(j, y)
        following = order[idx + 1] if idx + 1 < len(order) else None
        y_next = None
        for piece in range(PIPE_PIECES):
            if following is not None:
                part = matmul_piece(following, piece)
                y_next = part if y_next is None else y_next + part
            epilogue_piece(j, y, piece)
        y = y_next
    for _ in stages:
        pass


def _inproj(x2, scale, shift, w_all, bias_col, bias_row, gdn_extra=None, *, mode, layer, seq):
    tokens, d = x2.shape
    n = ML_MAIN if mode == "ml" else GDN_MAIN
    n_rows = bias_row.shape[0]
    tm = INPROJ_TM
    per_batch = seq // tm
    const = lambda i: (0, 0)
    resident = functools.partial(pl.BlockSpec, index_map=const, pipeline_mode=pl.Buffered(1))
    in_specs = [
        pl.BlockSpec((tm, d), lambda i: (i, 0)),
        pl.BlockSpec((1, 1, d), lambda i: (i // per_batch, 0, 0)),
        pl.BlockSpec((1, 1, d), lambda i: (i // per_batch, 0, 0)),
        pl.BlockSpec((None,) + w_all.shape[1:], lambda i: (layer, 0, 0), pipeline_mode=pl.Buffered(1)),
        resident((1, LANES)),
        resident((n_rows, 1)),
    ]
    operands = [x2, scale, shift, w_all, bias_col, bias_row]
    scratch = [pltpu.VMEM((tm, d), BF16)]
    if mode == "gdn":
        alog_col, alog_row, conv_w = gdn_extra
        in_specs += [resident((1, LANES)), resident((n_rows, 1)), resident((GDN_CONV_WIDTH, GDN_CONV_DIM))]
        operands += [alog_col, alog_row, conv_w]
        scratch += [pltpu.VMEM((CONV_HALO, GDN_CONV_DIM), F32),
                    pltpu.VMEM((tm + CONV_HALO, INPROJ_TN), F32)]
    return pl.pallas_call(
        functools.partial(_inproj_kernel, mode=mode, per_batch=per_batch),
        grid=(tokens // tm,),
        in_specs=in_specs,
        out_specs=[
            pl.BlockSpec((tm, n), lambda i: (i, 0)),
            pl.BlockSpec((tm, LANES), lambda i: (i, 0)),
            pl.BlockSpec((tm // CHUNK, n_rows, CHUNK), lambda i: (i, 0, 0)),
        ],
        out_shape=[
            jax.ShapeDtypeStruct((tokens, n), BF16),
            jax.ShapeDtypeStruct((tokens, LANES), F32),
            jax.ShapeDtypeStruct((tokens // CHUNK, n_rows, CHUNK), F32),
        ],
        scratch_shapes=scratch,
        compiler_params=_params("arbitrary"),
        name="inproj_" + mode,
    )(*operands)


def _mlstm_kernel(q_ref, k_ref, v_ref, o_ref, gcol_ref, grow_ref, nw_ref, out_ref,
                  c_scr, n_scr, m_scr, cprev_scr):
    rows = q_ref.shape[0]

    @pl.when(pl.program_id(1) == 0)
    def _():
        c_scr[...] = jnp.zeros_like(c_scr)
        n_scr[...] = jnp.zeros_like(n_scr)
        m_scr[...] = jnp.zeros_like(m_scr)

    causal = _iota((CHUNK, CHUNK), 0) >= _iota((CHUNK, CHUNK), 1)
    k_scale = ML_QK_DIM ** -0.5
    n_ch = rows // CHUNK
    items = [(ch, h) for ch in range(n_ch) for h in range(ML_HEADS)]

    def row_slice(ch):
        return slice(ch * CHUNK, (ch + 1) * CHUNK)

    def qk_slice(h):
        return slice(h * ML_QK_DIM, (h + 1) * ML_QK_DIM)

    def v_slice(h):
        return slice(h * ML_V_DIM, (h + 1) * ML_V_DIM)

    n_items = len(items)

    lane = _iota((CHUNK, LANES), 1)

    def lane_rep(col):
        return jnp.broadcast_to(col, (CHUNK, LANES))

    def wide(a):
        return jnp.concatenate([a] * (ML_V_DIM // LANES), axis=1)

    b_col, b_last, d_mat, g_end, m_intra, m_loc, k_w, d_c, d_n, scores = ({} for _ in range(10))
    for it, (ch, h) in enumerate(items):
        rs = row_slice(ch)
        gc = gcol_ref[rs, :]
        gr = grow_ref[ch]
        li_c = lane_rep(jnp.sum(jnp.where(lane == h, gc, 0.0), axis=1, keepdims=True))
        b_c = lane_rep(jnp.sum(jnp.where(lane == ML_HEADS + h, gc, 0.0), axis=1, keepdims=True))
        li_r = gr[h:h + 1, :]
        b_r = gr[ML_HEADS + h:ML_HEADS + h + 1, :]
        b_col[it], b_last[it] = b_c, b_c[CHUNK - 1:CHUNK, :]
        d_mat[it] = jnp.where(causal, b_c[:, :CHUNK] - b_r + li_r, -jnp.inf)
        g_end[it] = b_last[it] - b_c + li_c
    for it in range(n_items):
        m_intra[it] = jnp.max(d_mat[it], axis=1, keepdims=True)
        m_loc[it] = jnp.max(g_end[it], axis=0, keepdims=True)
    for it, (ch, h) in enumerate(items):
        k_w[it] = (k_ref[row_slice(ch), qk_slice(h)].astype(F32)
                   * (k_scale * jnp.exp(g_end[it] - m_loc[it])))
    for it, (ch, h) in enumerate(items):
        rs = row_slice(ch)
        d_c[it] = _dot_tn(k_w[it].astype(BF16), v_ref[rs, v_slice(h)])
        d_n[it] = jnp.sum(k_w[it], axis=0, keepdims=True)
        scores[it] = _dot_nt(q_ref[rs, qk_slice(h)], k_ref[rs, qk_slice(h)]) * k_scale

    m_prev, n_prev = {}, {}
    m_run = [m_scr[h] for h in range(ML_HEADS)]
    n_run = [n_scr[h] for h in range(ML_HEADS)]
    for it, (ch, h) in enumerate(items):
        c_run = c_scr[h]
        cprev_scr[it] = c_run.astype(BF16)
        m_prev[it], n_prev[it] = m_run[h], n_run[h]
        m_new = jnp.maximum(b_last[it] + m_run[h], m_loc[it])
        a_old = jnp.exp(b_last[it] + m_run[h] - m_new)
        a_new = jnp.exp(m_loc[it] - m_new)
        c_scr[h] = wide(a_old) * c_run + wide(a_new) * d_c[it]
        n_run[h] = a_old * n_run[h] + a_new * d_n[it]
        m_run[h] = m_new
    for h in range(ML_HEADS):
        n_scr[h] = n_run[h]
        m_scr[h] = m_run[h]

    m_t, s_inter, p, num, den, hid, ms = ({} for _ in range(7))
    for it in range(n_items):
        m_inter = b_col[it] + m_prev[it]
        m_t[it] = jnp.maximum(m_inter, m_intra[it])
        s_inter[it] = jnp.exp(m_inter - m_t[it])
        p[it] = jnp.exp(d_mat[it] - m_t[it][:, :CHUNK]) * scores[it]
    for it, (ch, h) in enumerate(items):
        rs = row_slice(ch)
        q = q_ref[rs, qk_slice(h)]
        num[it] = (wide(s_inter[it]) * _dot(q, cprev_scr[it])
                   + _dot(p[it].astype(BF16), v_ref[rs, v_slice(h)]))
        den[it] = (s_inter[it] * jnp.sum(q.astype(F32) * n_prev[it], axis=1, keepdims=True)
                   + jnp.sum(p[it], axis=1, keepdims=True))
    for it in range(n_items):
        hid[it] = num[it] * wide(1.0 / jnp.maximum(jnp.abs(den[it]), jnp.exp(-m_t[it])))
        ms[it] = jnp.mean(hid[it] * hid[it], axis=1, keepdims=True)
    for it, (ch, h) in enumerate(items):
        rs, vs = row_slice(ch), v_slice(h)
        out = hid[it] * lax.rsqrt(ms[it] + NORM_EPS) * nw_ref[:, vs] * _sigmoid(o_ref[rs, vs].astype(F32))
        out_ref[rs, vs] = out.astype(out_ref.dtype)


def _mlstm(proj, gcol, grow, norm_w, *, batch, seq):
    tokens = proj.shape[0]
    rows = REC_ROWS
    steps = seq // rows
    row = lambda b, t: b * steps + t
    return pl.pallas_call(
        _mlstm_kernel,
        grid=(batch, steps),
        in_specs=[
            pl.BlockSpec((rows, ML_QK_WIDTH), lambda b, t: (row(b, t), 0)),
            pl.BlockSpec((rows, ML_QK_WIDTH), lambda b, t: (row(b, t), 1)),
            pl.BlockSpec((rows, ML_V_WIDTH), lambda b, t: (row(b, t), 1)),
            pl.BlockSpec((rows, ML_V_WIDTH), lambda b, t: (row(b, t), 2)),
            pl.BlockSpec((rows, LANES), lambda b, t: (row(b, t), 0)),
            pl.BlockSpec((rows // CHUNK, ML_GATE_ROWS, CHUNK), lambda b, t: (row(b, t), 0, 0)),
            pl.BlockSpec((1, ML_V_WIDTH), lambda b, t: (0, 0)),
        ],
        out_specs=pl.BlockSpec((rows, ML_V_WIDTH), lambda b, t: (row(b, t), 0)),
        out_shape=jax.ShapeDtypeStruct((tokens, ML_V_WIDTH), BF16),
        scratch_shapes=[
            pltpu.VMEM((ML_HEADS, ML_QK_DIM, ML_V_DIM), F32),
            pltpu.VMEM((ML_HEADS, 1, ML_QK_DIM), F32),
            pltpu.VMEM((ML_HEADS, 1, LANES), F32),
            pltpu.VMEM((ML_HEADS * rows // CHUNK, ML_QK_DIM, ML_V_DIM), BF16),
        ],
        compiler_params=_params("parallel", "arbitrary"),
        name="mlstm",
    )(proj, proj, proj, proj, gcol, grow, norm_w)


def _gdn_kernel(q_ref, k_ref, v_ref, z_ref, gcol_ref, grow_ref, nw_ref, out_ref,
                s_scr, wq_scr, ka_scr, u_scr, eg_scr):
    rows = q_ref.shape[0]
    rep = GDN_V_HEADS // GDN_K_HEADS
    dh = GDN_HEAD_DIM
    n_kh = q_ref.shape[1] // dh
    n_vh = n_kh * rep
    n_ch = rows // CHUNK

    @pl.when(pl.program_id(1) == 0)
    def _():
        s_scr[...] = jnp.zeros_like(s_scr)

    ri = _iota((CHUNK, CHUNK), 0)
    ci = _iota((CHUNK, CHUNK), 1)
    causal = ri >= ci
    strict = ri > ci
    eye = jnp.where(ri == ci, 1.0, 0.0).astype(F32)
    lane = _iota((CHUNK, LANES), 1)
    merge_mask = {}
    s = 1
    while s < CHUNK:
        merge_mask[s] = (ri // s - ci // s == 1) & (ri // (2 * s) == ci // (2 * s))
        s *= 2

    kk, qk, qs, ks = {}, {}, {}, {}
    for ch in range(n_ch):
        rs = slice(ch * CHUNK, (ch + 1) * CHUNK)
        for kh in range(n_kh):
            hs = slice(kh * dh, (kh + 1) * dh)
            qb = q_ref[rs, hs]
            kb = k_ref[rs, hs]
            qs[ch, kh], ks[ch, kh] = qb.astype(F32), kb.astype(F32)
            kq = _dot_nt(jnp.concatenate([kb, qb], axis=0), kb)
            kk[ch, kh], qk[ch, kh] = kq[:CHUNK], kq[CHUNK:]

    items = [(ch, vh) for ch in range(n_ch) for vh in range(n_vh)]
    x, a_mats, rhs = {}, {}, {}
    for it, (ch, vh) in enumerate(items):
        rs = slice(ch * CHUNK, (ch + 1) * CHUNK)
        kh = vh // rep
        gc = gcol_ref[rs, :]
        beta_c = jnp.sum(jnp.where(lane == vh, gc, 0.0), axis=1, keepdims=True)
        g_c = jnp.sum(jnp.where(lane == GDN_V_HEADS + vh, gc, 0.0), axis=1, keepdims=True)
        g_r = grow_ref[ch, GDN_V_HEADS + vh:GDN_V_HEADS + vh + 1, :]
        g_last = g_r[:, CHUNK - 1:CHUNK]
        decay = jnp.exp(jnp.where(causal, g_c - g_r, -jnp.inf))
        a_mat = jnp.where(strict, kk[ch, kh] * beta_c * decay, 0.0)
        x[it] = eye - jnp.where(merge_mask[1], a_mat, 0.0)
        a_mats[it] = a_mat

        q, k = qs[ch, kh], ks[ch, kh]
        v = v_ref[rs, vh * dh:(vh + 1) * dh].astype(F32)
        e_g = jnp.exp(g_c)
        rhs[it] = jnp.concatenate([v * beta_c, k * (beta_c * e_g)], axis=1).astype(BF16)
        ka_scr[it, dh:dh + CHUNK, :] = (qk[ch, kh] * decay).astype(BF16)
        wq_scr[it, CHUNK:2 * CHUNK, :] = (q * e_g).astype(BF16)
        ka_scr[it, 0:dh, :] = (k * jnp.exp(g_last - g_c)).T.astype(BF16)
        eg_scr[it] = jnp.broadcast_to(jnp.exp(g_last), (1, LANES))

    s = 2
    while s < CHUNK:
        for it in range(len(items)):
            nb = jnp.where(merge_mask[s], a_mats[it], 0.0).astype(BF16)
            y = _dot(nb, x[it].astype(BF16))
            x[it] = x[it] - _dot(x[it].astype(BF16), y.astype(BF16))
        s *= 2

    for it in range(len(items)):
        sol = _dot(x[it].astype(BF16), rhs[it])
        u_scr[it] = sol[:, :dh]
        wq_scr[it, 0:CHUNK, :] = sol[:, dh:].astype(BF16)

    state = [s_scr[vh] for vh in range(n_vh)]
    for ch in range(n_ch):
        rs = slice(ch * CHUNK, (ch + 1) * CHUNK)
        base = ch * n_vh
        sb = [state[vh].astype(BF16) for vh in range(n_vh)]
        r = [_dot(wq_scr[base + vh], sb[vh]) for vh in range(n_vh)]
        vnb = [(u_scr[base + vh] - r[vh][:CHUNK]).astype(BF16) for vh in range(n_vh)]
        kav = [_dot(ka_scr[base + vh], vnb[vh]) for vh in range(n_vh)]
        state = [state[vh] * eg_scr[base + vh] + kav[vh][:dh] for vh in range(n_vh)]
        for vh in range(n_vh):
            out = r[vh][CHUNK:] + kav[vh][dh:]
            ms = jnp.mean(out * out, axis=1, keepdims=True)
            z = z_ref[rs, vh * dh:(vh + 1) * dh].astype(F32)
            gated = out * lax.rsqrt(ms + NORM_EPS) * nw_ref[...] * (z * _sigmoid(z))
            out_ref[rs, vh * dh:(vh + 1) * dh] = gated.astype(out_ref.dtype)
    for vh in range(n_vh):
        s_scr[vh] = state[vh]


def _gdn(proj, gcol, grow, norm_w, *, batch, seq):
    tokens = proj.shape[0]
    rows = GDN_ROWS
    steps = seq // rows
    n_items = GDN_V_HEADS * rows // CHUNK
    row = lambda b, t: b * steps + t
    return pl.pallas_call(
        _gdn_kernel,
        grid=(batch, steps),
        in_specs=[
            pl.BlockSpec((rows, GDN_KEY_WIDTH), lambda b, t: (row(b, t), 0)),
            pl.BlockSpec((rows, GDN_KEY_WIDTH), lambda b, t: (row(b, t), 1)),
            pl.BlockSpec((rows, GDN_VAL_WIDTH), lambda b, t: (row(b, t), 2 * GDN_KEY_WIDTH // GDN_VAL_WIDTH)),
            pl.BlockSpec((rows, GDN_VAL_WIDTH), lambda b, t: (row(b, t), GDN_CONV_DIM // GDN_VAL_WIDTH)),
            pl.BlockSpec((rows, LANES), lambda b, t: (row(b, t), 0)),
            pl.BlockSpec((rows // CHUNK, GDN_GATE_ROWS, CHUNK), lambda b, t: (row(b, t), 0, 0)),
            pl.BlockSpec((1, GDN_HEAD_DIM), lambda b, t: (0, 0)),
        ],
        out_specs=pl.BlockSpec((rows, GDN_VAL_WIDTH), lambda b, t: (row(b, t), 0)),
        out_shape=jax.ShapeDtypeStruct((tokens, GDN_VAL_WIDTH), BF16),
        scratch_shapes=[
            pltpu.VMEM((GDN_V_HEADS, GDN_HEAD_DIM, GDN_HEAD_DIM), F32),
            pltpu.VMEM((n_items, 2 * CHUNK, GDN_HEAD_DIM), BF16),
            pltpu.VMEM((n_items, GDN_HEAD_DIM + CHUNK, CHUNK), BF16),
            pltpu.VMEM((n_items, CHUNK, GDN_HEAD_DIM), F32),
            pltpu.VMEM((n_items, 1, LANES), F32),
        ],
        compiler_params=_params("parallel", "arbitrary"),
        name="gdn",
    )(proj, proj, proj, proj, gcol, grow, norm_w)


def _block_tail_kernel(a_ref, wo_ref, x_ref, g1_ref, sc_ref, sh_ref, g2_ref, wgu_ref, wd_ref, fw_ref, o_ref,
                       x1_scr, h_scr, act_scr, *, final_norm):
    d_ff = wd_ref.shape[0]
    x1 = x_ref[...] + g1_ref[0] * _dot(a_ref[...], wo_ref[...])
    x1_scr[...] = x1
    h_scr[...] = _modulated_norm(x1, sc_ref[0], sh_ref[0]).astype(BF16)
    for j in range(d_ff // FFN_TH):
        gate = _dot(h_scr[...], wgu_ref[:, j * FFN_TH:(j + 1) * FFN_TH])
        up = _dot(h_scr[...], wgu_ref[:, d_ff + j * FFN_TH:d_ff + (j + 1) * FFN_TH])
        act_scr[:, j * FFN_TH:(j + 1) * FFN_TH] = (gate * _sigmoid(gate) * up).astype(BF16)
    y = x1_scr[...] + g2_ref[0] * _dot(act_scr[...], wd_ref[...])
    if final_norm:
        ms = jnp.mean(y * y, axis=-1, keepdims=True)
        y = y * lax.rsqrt(ms + NORM_EPS) * fw_ref[...]
    o_ref[...] = y


def _block_tail(mixed, w_out, x2, g1, scale, shift, g2, w_gate_up, w_down, final_w, *, layer, seq, final_norm):
    tokens, d = x2.shape
    kdim = mixed.shape[1]
    d_ff = w_down.shape[1]
    tm = FFN_TM
    per_batch = seq // tm
    mod_spec = pl.BlockSpec((1, 1, d), lambda i: (i // per_batch, 0, 0))
    resident = functools.partial(pl.BlockSpec, index_map=lambda i: (0, 0), pipeline_mode=pl.Buffered(1))
    stacked = functools.partial(pl.BlockSpec, index_map=lambda i: (layer, 0, 0), pipeline_mode=pl.Buffered(1))
    return pl.pallas_call(
        functools.partial(_block_tail_kernel, final_norm=final_norm),
        grid=(tokens // tm,),
        in_specs=[
            pl.BlockSpec((tm, kdim), lambda i: (i, 0)),
            resident((kdim, d)),
            pl.BlockSpec((tm, d), lambda i: (i, 0)),
            mod_spec, mod_spec, mod_spec, mod_spec,
            stacked((None, d, 2 * d_ff)),
            stacked((None, d_ff, d)),
            resident((1, d)),
        ],
        out_specs=pl.BlockSpec((tm, d), lambda i: (i, 0)),
        out_shape=jax.ShapeDtypeStruct((tokens, d), F32),
        scratch_shapes=[pltpu.VMEM((tm, d), F32), pltpu.VMEM((tm, d), BF16), pltpu.VMEM((tm, d_ff), BF16)],
        compiler_params=_params("parallel"),
        name="block_tail",
    )(mixed, w_out, x2, g1, scale, shift, g2, w_gate_up, w_down, final_w)


def _pad_cast_kernel(wt_ref, o_ref, *, width):
    cols = wt_ref.shape[0]
    col = pl.program_id(1) * cols + _iota(wt_ref.shape, 0)
    o_ref[...] = jnp.where(col < width, wt_ref[...], 0.0).T.astype(o_ref.dtype)


def _padded_in_weights(w_in, n_main):
    layers, d, width = w_in.shape
    n_pad = n_main + LANES
    tiles = n_pad // LANES
    per_slab = max(t for t in range(1, WEIGHT_PREP_MAX_TILES + 1) if tiles % t == 0)
    cols = per_slab * LANES
    return pl.pallas_call(
        functools.partial(_pad_cast_kernel, width=width),
        grid=(layers, tiles // per_slab),
        in_specs=[pl.BlockSpec((None, cols, d), lambda l, j: (l, j, 0))],
        out_specs=pl.BlockSpec((None, d, cols), lambda l, j: (l, 0, j)),
        out_shape=jax.ShapeDtypeStruct((layers, d, n_pad), BF16),
        compiler_params=_params("parallel", "parallel"),
        name="pad_cast",
    )(jnp.swapaxes(w_in, 1, 2))


def _gate_vec(vec, offset, n_rows):
    count = vec.shape[0]
    col = jnp.zeros((1, LANES), F32).at[0, offset:offset + count].set(vec.astype(F32))
    row = jnp.zeros((n_rows, 1), F32).at[offset:offset + count, 0].set(vec.astype(F32))
    return col, row


def kernel(x, c, ada_w, ada_b, ml_w_in, ml_b_if, ml_norm_w, ml_w_out, gdn_w_in, gdn_conv_w, gdn_a_log,
           gdn_dt_bias, gdn_norm_w, gdn_w_out, ffn_w_gate_up, ffn_w_down, final_norm_w):
    batch, seq, d = x.shape
    depth = ada_w.shape[0]
    n_mixers = 2
    tokens = batch * seq
    x2 = x.reshape(tokens, d)

    c_rows = 8
    c_pad = jnp.pad(c, ((0, c_rows - batch), (0, 0)))
    mod = _adaln(c_pad, ada_w, ada_b)[:, :batch, :]

    w_gate_up = ffn_w_gate_up.astype(BF16)
    w_down = ffn_w_down.astype(BF16)
    for layer in range(depth):
        sh1, sc1, g1, sh2, sc2, g2 = [m.reshape(batch, 1, d) for m in jnp.split(mod[layer], 6, axis=-1)]
        jdx = layer // n_mixers
        if layer % n_mixers == 0:
            b_col, b_row = _gate_vec(ml_b_if[jdx], 0, ML_GATE_ROWS)
            proj, gcol, grow = _inproj(x2, sc1, sh1, _padded_in_weights(ml_w_in, ML_MAIN),
                                       b_col, b_row, mode="ml", layer=jdx, seq=seq)
            mixed = _mlstm(proj, gcol, grow, ml_norm_w[jdx].reshape(1, ML_V_WIDTH).astype(F32),
                           batch=batch, seq=seq)
            w_out = ml_w_out[jdx]
        else:
            b_col, b_row = _gate_vec(gdn_dt_bias[jdx], GDN_V_HEADS, GDN_GATE_ROWS)
            a_col, a_row = _gate_vec(gdn_a_log[jdx], GDN_V_HEADS, GDN_GATE_ROWS)
            proj, gcol, grow = _inproj(x2, sc1, sh1, _padded_in_weights(gdn_w_in, GDN_MAIN),
                                       b_col, b_row, (a_col, a_row, gdn_conv_w[jdx].astype(F32)),
                                       mode="gdn", layer=jdx, seq=seq)
            mixed = _gdn(proj, gcol, grow, gdn_norm_w[jdx].reshape(1, GDN_HEAD_DIM).astype(F32),
                         batch=batch, seq=seq)
            w_out = gdn_w_out[jdx]
        x2 = _block_tail(mixed, w_out.astype(BF16), x2, g1, sc2, sh2, g2,
                         w_gate_up, w_down, final_norm_w.reshape(1, d).astype(F32),
                         layer=layer, seq=seq, final_norm=(layer == depth - 1))
    return x2.reshape(batch, seq, d)
```

```python
import functools

import jax
import jax.numpy as jnp
from jax import lax
from jax.experimental import pallas as pl
from jax.experimental.pallas import tpu as pltpu

F32 = jnp.float32
BF16 = jnp.bfloat16
HIGHEST = lax.Precision.HIGHEST

LANES = 128
VMEM_LIMIT = 56 * 1024 * 1024

D_MODEL = 1024
CHUNK = 64
NORM_EPS = 1e-6

ML_HEADS = 4
ML_QK_DIM = 128
ML_V_DIM = 256
ML_QK_WIDTH = ML_HEADS * ML_QK_DIM
ML_V_WIDTH = ML_HEADS * ML_V_DIM
ML_MAIN = 2 * ML_QK_WIDTH + 2 * ML_V_WIDTH
ML_GATE_ROWS = 16
ML_GATE_CAP = 15.0

GDN_K_HEADS = 8
GDN_V_HEADS = 16
GDN_HEAD_DIM = 128
GDN_KEY_WIDTH = GDN_K_HEADS * GDN_HEAD_DIM
GDN_VAL_WIDTH = GDN_V_HEADS * GDN_HEAD_DIM
GDN_CONV_DIM = 2 * GDN_KEY_WIDTH + GDN_VAL_WIDTH
GDN_CONV_WIDTH = 4
GDN_MAIN = GDN_CONV_DIM + GDN_VAL_WIDTH
GDN_GATE_ROWS = 2 * GDN_V_HEADS

WEIGHT_PREP_MAX_TILES = 8
ADALN_TN = 2048
INPROJ_TM = 512
INPROJ_TN = 256
CONV_HALO = 8
PIPE_PIECES = 1
REC_ROWS = 512
GDN_ROWS = 256
FFN_TM = 512
FFN_TH = 256


def _params(*sem):
    return pltpu.CompilerParams(dimension_semantics=sem, vmem_limit_bytes=VMEM_LIMIT)


def _sigmoid(x):
    return 1.0 / (1.0 + jnp.exp(-x))


def _softplus(x):
    return jnp.maximum(x, 0.0) + jnp.log1p(jnp.exp(-jnp.abs(x)))


def _dot(a, b):
    return jnp.dot(a, b, preferred_element_type=F32)


def _dot_nt(a, b):
    return lax.dot_general(a, b, (((1,), (1,)), ((), ())), preferred_element_type=F32)


def _dot_tn(a, b):
    return lax.dot_general(a, b, (((0,), (0,)), ((), ())), preferred_element_type=F32)


def _iota(shape, dim):
    return lax.broadcasted_iota(jnp.int32, shape, dim)


def _adaln_kernel(c_ref, w_ref, b_ref, o_ref):
    c = c_ref[...]
    act = c * _sigmoid(c)
    w = w_ref[0]
    a1 = act.astype(BF16)
    a2 = (act - a1.astype(F32)).astype(BF16)
    w1 = w.astype(BF16)
    w2 = (w - w1.astype(F32)).astype(BF16)
    o_ref[0] = _dot(a1, w1) + (_dot(a1, w2) + _dot(a2, w1)) + b_ref[0]


def _adaln(c_pad, ada_w, ada_b):
    depth, d, n = ada_w.shape
    rows = c_pad.shape[0]
    tn = ADALN_TN
    return pl.pallas_call(
        _adaln_kernel,
        grid=(depth, n // tn),
        in_specs=[
            pl.BlockSpec((rows, d), lambda l, j: (0, 0)),
            pl.BlockSpec((1, d, tn), lambda l, j: (l, 0, j)),
            pl.BlockSpec((1, 1, tn), lambda l, j: (l, 0, j)),
        ],
        out_specs=pl.BlockSpec((1, rows, tn), lambda l, j: (l, 0, j)),
        out_shape=jax.ShapeDtypeStruct((depth, rows, n), F32),
        compiler_params=_params("parallel", "parallel"),
        name="adaln",
    )(c_pad, ada_w, ada_b.reshape(depth, 1, n))


def _modulated_norm(x, scale, shift):
    ms = jnp.mean(x * x, axis=-1, keepdims=True)
    return x * lax.rsqrt(ms + NORM_EPS) * (1.0 + scale) + shift


def _ml_gates(pre, idx):
    capped = ML_GATE_CAP * jnp.tanh(pre / ML_GATE_CAP)
    log_f = jnp.minimum(capped, 0.0) - jnp.log1p(jnp.exp(-jnp.abs(capped)))
    is_f = idx >= ML_HEADS
    return jnp.where(is_f, log_f, capped), is_f


def _gdn_gates(pre, a_log, idx):
    is_g = idx >= GDN_V_HEADS
    return jnp.where(is_g, -jnp.exp(a_log) * _softplus(pre), _sigmoid(pre)), is_g


def _inproj_kernel(*refs, mode, per_batch):
    if mode == "ml":
        (x_ref, sc_ref, sh_ref, w_ref, bcol_ref, brow_ref,
         proj_ref, gcol_ref, grow_ref, h_scr) = refs
    else:
        (x_ref, sc_ref, sh_ref, w_ref, bcol_ref, brow_ref, acol_ref, arow_ref, cw_ref,
         proj_ref, gcol_ref, grow_ref, h_scr, halo_scr, conv_buf) = refs
    tm = x_ref.shape[0]
    n = proj_ref.shape[1]
    n_rows = brow_ref.shape[0]
    tn = INPROJ_TN

    h_scr[...] = _modulated_norm(x_ref[...], sc_ref[0], sh_ref[0]).astype(BF16)

    def gate_stages():
        chunks = [slice(ch * CHUNK, (ch + 1) * CHUNK) for ch in range(tm // CHUNK)]
        lane = _iota((CHUNK, LANES), 1)
        row = _iota((n_rows, CHUNK), 0)
        rr = _iota((CHUNK, CHUNK), 0)
        cc = _iota((CHUNK, CHUNK), 1)
        lower = jnp.where(cc <= rr, 1.0, 0.0).astype(F32)
        upper = jnp.where(rr <= cc, 1.0, 0.0).astype(F32)
        wg = w_ref[:, n:n + LANES]
        wg_t = wg.astype(F32).T[:n_rows, :].astype(BF16)
        pre = _dot(h_scr[...], wg) + bcol_ref[...]
        pre_t = [_dot_nt(wg_t, h_scr[rs, :]) + brow_ref[...] for rs in chunks]
        yield
        if mode == "ml":
            col = [_ml_gates(pre[rs, :], lane) for rs in chunks]
        else:
            col = [_gdn_gates(pre[rs, :], acol_ref[...], lane) for rs in chunks]
        yield
        if mode == "ml":
            rowg = [_ml_gates(p_t, row) for p_t in pre_t]
        else:
            rowg = [_gdn_gates(p_t, arow_ref[...], row) for p_t in pre_t]
        yield
        csum = [jnp.dot(lower, vals, precision=HIGHEST, preferred_element_type=F32) for vals, _ in col]
        csum_t = [jnp.dot(vals_t, upper, precision=HIGHEST, preferred_element_type=F32) for vals_t, _ in rowg]
        yield
        for ch, rs in enumerate(chunks):
            gcol_ref[rs, :] = jnp.where(col[ch][1], csum[ch], col[ch][0])
            grow_ref[ch] = jnp.where(rowg[ch][1], csum_t[ch], rowg[ch][0])
        yield

    stages = gate_stages()
    order = list(range(n // tn))
    if mode == "gdn":
        first_of_seq = pl.program_id(0) % per_batch == 0
        conv_slabs = [j for j in order if j * tn < GDN_CONV_DIM]
        plain_slabs = [j for j in order if j * tn >= GDN_CONV_DIM]
        order = []
        while conv_slabs or plain_slabs:
            order += conv_slabs[:2] + plain_slabs[:1]
            conv_slabs, plain_slabs = conv_slabs[2:], plain_slabs[1:]
    d = x_ref.shape[1]
    kp, rp = d // PIPE_PIECES, tm // PIPE_PIECES

    def slab_cols(j):
        return slice(j * tn, (j + 1) * tn)

    def matmul_piece(j, kk):
        return _dot(h_scr[:, kk * kp:(kk + 1) * kp], w_ref[kk * kp:(kk + 1) * kp, slab_cols(j)])

    def start_epilogue(j, y):
        if mode == "gdn" and j * tn < GDN_CONV_DIM:
            halo_ref = halo_scr.at[:, slab_cols(j)]
            conv_buf[0:CONV_HALO, :] = jnp.where(first_of_seq, 0.0, halo_ref[...])
            conv_buf[CONV_HALO:CONV_HALO + tm, :] = y
            halo_ref[...] = y[tm - CONV_HALO:tm, :]

    def epilogue_piece(j, y, r):
        cs, rows = slab_cols(j), slice(r * rp, (r + 1) * rp)
        if not (mode == "gdn" and j * tn < GDN_CONV_DIM):
            proj_ref[rows, cs] = y[rows, :].astype(proj_ref.dtype)
            return
        conv_w = cw_ref[:, cs]
        acc = conv_w[GDN_CONV_WIDTH - 1:GDN_CONV_WIDTH, :] * y[rows, :]
        for tap in range(GDN_CONV_WIDTH - 1):
            first = CONV_HALO - (GDN_CONV_WIDTH - 1 - tap) + r * rp
            acc = acc + conv_w[tap:tap + 1, :] * conv_buf[first:first + rp, :]
        acc = acc * _sigmoid(acc)
        if j * tn >= 2 * GDN_KEY_WIDTH:
            proj_ref[rows, cs] = acc.astype(proj_ref.dtype)
            return
        q_scale = GDN_HEAD_DIM ** -0.5 if j * tn < GDN_KEY_WIDTH else 1.0
        for hh in range(tn // GDN_HEAD_DIM):
            hs = slice(hh * GDN_HEAD_DIM, (hh + 1) * GDN_HEAD_DIM)
            blk = acc[:, hs]
            ss = jnp.sum(blk * blk, axis=1, keepdims=True)
            proj_ref[rows, j * tn + hh * GDN_HEAD_DIM:j * tn + (hh + 1) * GDN_HEAD_DIM] = (
                blk * (lax.rsqrt(ss + NORM_EPS) * q_scale)).astype(proj_ref.dtype)

    y = _dot(h_scr[...], w_ref[:, slab_cols(order[0])])
    for idx, j in enumerate(order):
        next(stages, None)
        start_epilogue(j, y)
        following = order[idx + 1] if idx + 1 < len(order) else None
        y_next = None
        for piece in range(PIPE_PIECES):
            if following is not None:
                part = matmul_piece(following, piece)
                y_next = part if y_next is None else y_next + part
            epilogue_piece(j, y, piece)
        y = y_next
    for _ in stages:
        pass


def _inproj(x2, scale, shift, w_all, bias_col, bias_row, gdn_extra=None, *, mode, layer, seq):
    tokens, d = x2.shape
    n = ML_MAIN if mode == "ml" else GDN_MAIN
    n_rows = bias_row.shape[0]
    tm = INPROJ_TM
    per_batch = seq // tm
    const = lambda i: (0, 0)
    resident = functools.partial(pl.BlockSpec, index_map=const, pipeline_mode=pl.Buffered(1))
    in_specs = [
        pl.BlockSpec((tm, d), lambda i: (i, 0)),
        pl.BlockSpec((1, 1, d), lambda i: (i // per_batch, 0, 0)),
        pl.BlockSpec((1, 1, d), lambda i: (i // per_batch, 0, 0)),
        pl.BlockSpec((None,) + w_all.shape[1:], lambda i: (layer, 0, 0), pipeline_mode=pl.Buffered(1)),
        resident((1, LANES)),
        resident((n_rows, 1)),
    ]
    operands = [x2, scale, shift, w_all, bias_col, bias_row]
    scratch = [pltpu.VMEM((tm, d), BF16)]
    if mode == "gdn":
        alog_col, alog_row, conv_w = gdn_extra
        in_specs += [resident((1, LANES)), resident((n_rows, 1)), resident((GDN_CONV_WIDTH, GDN_CONV_DIM))]
        operands += [alog_col, alog_row, conv_w]
        scratch += [pltpu.VMEM((CONV_HALO, GDN_CONV_DIM), F32),
                    pltpu.VMEM((tm + CONV_HALO, INPROJ_TN), F32)]
    return pl.pallas_call(
        functools.partial(_inproj_kernel, mode=mode, per_batch=per_batch),
        grid=(tokens // tm,),
        in_specs=in_specs,
        out_specs=[
            pl.BlockSpec((tm, n), lambda i: (i, 0)),
            pl.BlockSpec((tm, LANES), lambda i: (i, 0)),
            pl.BlockSpec((tm // CHUNK, n_rows, CHUNK), lambda i: (i, 0, 0)),
        ],
        out_shape=[
            jax.ShapeDtypeStruct((tokens, n), BF16),
            jax.ShapeDtypeStruct((tokens, LANES), F32),
            jax.ShapeDtypeStruct((tokens // CHUNK, n_rows, CHUNK), F32),
        ],
        scratch_shapes=scratch,
        compiler_params=_params("arbitrary"),
        name="inproj_" + mode,
    )(*operands)


def _mlstm_kernel(q_ref, k_ref, v_ref, o_ref, gcol_ref, grow_ref, nw_ref, out_ref,
                  c_scr, n_scr, m_scr, cprev_scr):
    rows = q_ref.shape[0]

    @pl.when(pl.program_id(1) == 0)
    def _():
        c_scr[...] = jnp.zeros_like(c_scr)
        n_scr[...] = jnp.zeros_like(n_scr)
        m_scr[...] = jnp.zeros_like(m_scr)

    causal = _iota((CHUNK, CHUNK), 0) >= _iota((CHUNK, CHUNK), 1)
    k_scale = ML_QK_DIM ** -0.5
    n_ch = rows // CHUNK
    items = [(ch, h) for ch in range(n_ch) for h in range(ML_HEADS)]

    def row_slice(ch):
        return slice(ch * CHUNK, (ch + 1) * CHUNK)

    def qk_slice(h):
        return slice(h * ML_QK_DIM, (h + 1) * ML_QK_DIM)

    def v_slice(h):
        return slice(h * ML_V_DIM, (h + 1) * ML_V_DIM)

    n_items = len(items)

    lane = _iota((CHUNK, LANES), 1)

    def lane_rep(col):
        return jnp.broadcast_to(col, (CHUNK, LANES))

    def wide(a):
        return jnp.concatenate([a] * (ML_V_DIM // LANES), axis=1)

    b_col, b_last, d_mat, g_end, m_intra, m_loc, k_w, d_c, d_n, scores = ({} for _ in range(10))
    for it, (ch, h) in enumerate(items):
        rs = row_slice(ch)
        gc = gcol_ref[rs, :]
        gr = grow_ref[ch]
        li_c = lane_rep(jnp.sum(jnp.where(lane == h, gc, 0.0), axis=1, keepdims=True))
        b_c = lane_rep(jnp.sum(jnp.where(lane == ML_HEADS + h, gc, 0.0), axis=1, keepdims=True))
        li_r = gr[h:h + 1, :]
        b_r = gr[ML_HEADS + h:ML_HEADS + h + 1, :]
        b_col[it], b_last[it] = b_c, b_c[CHUNK - 1:CHUNK, :]
        d_mat[it] = jnp.where(causal, b_c[:, :CHUNK] - b_r + li_r, -jnp.inf)
        g_end[it] = b_last[it] - b_c + li_c
    for it in range(n_items):
        m_intra[it] = jnp.max(d_mat[it], axis=1, keepdims=True)
        m_loc[it] = jnp.max(g_end[it], axis=0, keepdims=True)
    for it, (ch, h) in enumerate(items):
        k_w[it] = (k_ref[row_slice(ch), qk_slice(h)].astype(F32)
                   * (k_scale * jnp.exp(g_end[it] - m_loc[it])))
    for it, (ch, h) in enumerate(items):
        rs = row_slice(ch)
        d_c[it] = _dot_tn(k_w[it].astype(BF16), v_ref[rs, v_slice(h)])
        d_n[it] = jnp.sum(k_w[it], axis=0, keepdims=True)
        scores[it] = _dot_nt(q_ref[rs, qk_slice(h)], k_ref[rs, qk_slice(h)]) * k_scale

    m_prev, n_prev = {}, {}
    m_run = [m_scr[h] for h in range(ML_HEADS)]
    n_run = [n_scr[h] for h in range(ML_HEADS)]
    for it, (ch, h) in enumerate(items):
        c_run = c_scr[h]
        cprev_scr[it] = c_run.astype(BF16)
        m_prev[it], n_prev[it] = m_run[h], n_run[h]
        m_new = jnp.maximum(b_last[it] + m_run[h], m_loc[it])
        a_old = jnp.exp(b_last[it] + m_run[h] - m_new)
        a_new = jnp.exp(m_loc[it] - m_new)
        c_scr[h] = wide(a_old) * c_run + wide(a_new) * d_c[it]
        n_run[h] = a_old * n_run[h] + a_new * d_n[it]
        m_run[h] = m_new
    for h in range(ML_HEADS):
        n_scr[h] = n_run[h]
        m_scr[h] = m_run[h]

    m_t, s_inter, p, num, den, hid, ms = ({} for _ in range(7))
    for it in range(n_items):
        m_inter = b_col[it] + m_prev[it]
        m_t[it] = jnp.maximum(m_inter, m_intra[it])
        s_inter[it] = jnp.exp(m_inter - m_t[it])
        p[it] = jnp.exp(d_mat[it] - m_t[it][:, :CHUNK]) * scores[it]
    for it, (ch, h) in enumerate(items):
        rs = row_slice(ch)
        q = q_ref[rs, qk_slice(h)]
        num[it] = (wide(s_inter[it]) * _dot(q, cprev_scr[it])
                   + _dot(p[it].astype(BF16), v_ref[rs, v_slice(h)]))
        den[it] = (s_inter[it] * jnp.sum(q.astype(F32) * n_prev[it], axis=1, keepdims=True)
                   + jnp.sum(p[it], axis=1, keepdims=True))
    for it in range(n_items):
        hid[it] = num[it] * wide(1.0 / jnp.maximum(jnp.abs(den[it]), jnp.exp(-m_t[it])))
        ms[it] = jnp.mean(hid[it] * hid[it], axis=1, keepdims=True)
    for it, (ch, h) in enumerate(items):
        rs, vs = row_slice(ch), v_slice(h)
        out = hid[it] * lax.rsqrt(ms[it] + NORM_EPS) * nw_ref[:, vs] * _sigmoid(o_ref[rs, vs].astype(F32))
        out_ref[rs, vs] = out.astype(out_ref.dtype)


def _mlstm(proj, gcol, grow, norm_w, *, batch, seq):
    tokens = proj.shape[0]
    rows = REC_ROWS
    steps = seq // rows
    row = lambda b, t: b * steps + t
    return pl.pallas_call(
        _mlstm_kernel,
        grid=(batch, steps),
        in_specs=[
            pl.BlockSpec((rows, ML_QK_WIDTH), lambda b, t: (row(b, t), 0)),
            pl.BlockSpec((rows, ML_QK_WIDTH), lambda b, t: (row(b, t), 1)),
            pl.BlockSpec((rows, ML_V_WIDTH), lambda b, t: (row(b, t), 1)),
            pl.BlockSpec((rows, ML_V_WIDTH), lambda b, t: (row(b, t), 2)),
            pl.BlockSpec((rows, LANES), lambda b, t: (row(b, t), 0)),
            pl.BlockSpec((rows // CHUNK, ML_GATE_ROWS, CHUNK), lambda b, t: (row(b, t), 0, 0)),
            pl.BlockSpec((1, ML_V_WIDTH), lambda b, t: (0, 0)),
        ],
        out_specs=pl.BlockSpec((rows, ML_V_WIDTH), lambda b, t: (row(b, t), 0)),
        out_shape=jax.ShapeDtypeStruct((tokens, ML_V_WIDTH), BF16),
        scratch_shapes=[
            pltpu.VMEM((ML_HEADS, ML_QK_DIM, ML_V_DIM), F32),
            pltpu.VMEM((ML_HEADS, 1, ML_QK_DIM), F32),
            pltpu.VMEM((ML_HEADS, 1, LANES), F32),
            pltpu.VMEM((ML_HEADS * rows // CHUNK, ML_QK_DIM, ML_V_DIM), BF16),
        ],
        compiler_params=_params("parallel", "arbitrary"),
        name="mlstm",
    )(proj, proj, proj, proj, gcol, grow, norm_w)


def _gdn_kernel(q_ref, k_ref, v_ref, z_ref, gcol_ref, grow_ref, nw_ref, out_ref,
                s_scr, wq_scr, attn_scr, kdt_scr, u_scr, eg_scr):
    rows = q_ref.shape[0]
    rep = GDN_V_HEADS // GDN_K_HEADS
    dh = GDN_HEAD_DIM
    n_kh = q_ref.shape[1] // dh
    n_vh = n_kh * rep
    n_ch = rows // CHUNK

    @pl.when(pl.program_id(1) == 0)
    def _():
        s_scr[...] = jnp.zeros_like(s_scr)

    ri = _iota((CHUNK, CHUNK), 0)
    ci = _iota((CHUNK, CHUNK), 1)
    causal = ri >= ci
    strict = ri > ci
    eye = jnp.where(ri == ci, 1.0, 0.0).astype(F32)
    lane = _iota((CHUNK, LANES), 1)
    merge_mask = {}
    s = 1
    while s < CHUNK:
        merge_mask[s] = (ri // s - ci // s == 1) & (ri // (2 * s) == ci // (2 * s))
        s *= 2

    kk, qk, qs, ks = {}, {}, {}, {}
    for ch in range(n_ch):
        rs = slice(ch * CHUNK, (ch + 1) * CHUNK)
        for kh in range(n_kh):
            hs = slice(kh * dh, (kh + 1) * dh)
            qb = q_ref[rs, hs]
            kb = k_ref[rs, hs]
            qs[ch, kh], ks[ch, kh] = qb.astype(F32), kb.astype(F32)
            kk[ch, kh] = _dot_nt(kb, kb)
            qk[ch, kh] = _dot_nt(qb, kb)

    items = [(ch, vh) for ch in range(n_ch) for vh in range(n_vh)]
    x, a_mats, rhs = {}, {}, {}
    for it, (ch, vh) in enumerate(items):
        rs = slice(ch * CHUNK, (ch + 1) * CHUNK)
        kh = vh // rep
        gc = gcol_ref[rs, :]
        beta_c = jnp.sum(jnp.where(lane == vh, gc, 0.0), axis=1, keepdims=True)
        g_c = jnp.sum(jnp.where(lane == GDN_V_HEADS + vh, gc, 0.0), axis=1, keepdims=True)
        g_r = grow_ref[ch, GDN_V_HEADS + vh:GDN_V_HEADS + vh + 1, :]
        g_last = g_r[:, CHUNK - 1:CHUNK]
        decay = jnp.exp(jnp.where(causal, g_c - g_r, -jnp.inf))
        a_mat = jnp.where(strict, kk[ch, kh] * beta_c * decay, 0.0)
        x[it] = eye - jnp.where(merge_mask[1], a_mat, 0.0)
        a_mats[it] = a_mat

        q, k = qs[ch, kh], ks[ch, kh]
        v = v_ref[rs, vh * dh:(vh + 1) * dh].astype(F32)
        e_g = jnp.exp(g_c)
        rhs[it] = jnp.concatenate([v * beta_c, k * (beta_c * e_g)], axis=1).astype(BF16)
        attn_scr[it] = (qk[ch, kh] * decay).astype(BF16)
        wq_scr[it, CHUNK:2 * CHUNK, :] = (q * e_g).astype(BF16)
        kdt_scr[it] = (k * jnp.exp(g_last - g_c)).T.astype(BF16)
        eg_scr[it] = jnp.broadcast_to(jnp.exp(g_last), (1, LANES))

    s = 2
    while s < CHUNK:
        for it in range(len(items)):
            nb = jnp.where(merge_mask[s], a_mats[it], 0.0).astype(BF16)
            y = _dot(nb, x[it].astype(BF16))
            x[it] = x[it] - _dot(x[it].astype(BF16), y.astype(BF16))
        s *= 2

    for it in range(len(items)):
        sol = _dot(x[it].astype(BF16), rhs[it])
        u_scr[it] = sol[:, :dh]
        wq_scr[it, 0:CHUNK, :] = sol[:, dh:].astype(BF16)

    state = [s_scr[vh] for vh in range(n_vh)]
    for ch in range(n_ch):
        rs = slice(ch * CHUNK, (ch + 1) * CHUNK)
        base = ch * n_vh
        sb = [state[vh].astype(BF16) for vh in range(n_vh)]
        r = [_dot(wq_scr[base + vh], sb[vh]) for vh in range(n_vh)]
        vnb = [(u_scr[base + vh] - r[vh][:CHUNK]).astype(BF16) for vh in range(n_vh)]
        state = [state[vh] * eg_scr[base + vh] + _dot(kdt_scr[base + vh], vnb[vh]) for vh in range(n_vh)]
        for vh in range(n_vh):
            out = r[vh][CHUNK:] + _dot(attn_scr[base + vh], vnb[vh])
            ms = jnp.mean(out * out, axis=1, keepdims=True)
            z = z_ref[rs, vh * dh:(vh + 1) * dh].astype(F32)
            gated = out * lax.rsqrt(ms + NORM_EPS) * nw_ref[...] * (z * _sigmoid(z))
            out_ref[rs, vh * dh:(vh + 1) * dh] = gated.astype(out_ref.dtype)
    for vh in range(n_vh):
        s_scr[vh] = state[vh]


def _gdn(proj, gcol, grow, norm_w, *, batch, seq):
    tokens = proj.shape[0]
    rows = GDN_ROWS
    steps = seq // rows
    n_items = GDN_V_HEADS * rows // CHUNK
    row = lambda b, t: b * steps + t
    return pl.pallas_call(
        _gdn_kernel,
        grid=(batch, steps),
        in_specs=[
            pl.BlockSpec((rows, GDN_KEY_WIDTH), lambda b, t: (row(b, t), 0)),
            pl.BlockSpec((rows, GDN_KEY_WIDTH), lambda b, t: (row(b, t), 1)),
            pl.BlockSpec((rows, GDN_VAL_WIDTH), lambda b, t: (row(b, t), 2 * GDN_KEY_WIDTH // GDN_VAL_WIDTH)),
            pl.BlockSpec((rows, GDN_VAL_WIDTH), lambda b, t: (row(b, t), GDN_CONV_DIM // GDN_VAL_WIDTH)),
            pl.BlockSpec((rows, LANES), lambda b, t: (row(b, t), 0)),
            pl.BlockSpec((rows // CHUNK, GDN_GATE_ROWS, CHUNK), lambda b, t: (row(b, t), 0, 0)),
            pl.BlockSpec((1, GDN_HEAD_DIM), lambda b, t: (0, 0)),
        ],
        out_specs=pl.BlockSpec((rows, GDN_VAL_WIDTH), lambda b, t: (row(b, t), 0)),
        out_shape=jax.ShapeDtypeStruct((tokens, GDN_VAL_WIDTH), BF16),
        scratch_shapes=[
            pltpu.VMEM((GDN_V_HEADS, GDN_HEAD_DIM, GDN_HEAD_DIM), F32),
            pltpu.VMEM((n_items, 2 * CHUNK, GDN_HEAD_DIM), BF16),
            pltpu.VMEM((n_items, CHUNK, CHUNK), BF16),
            pltpu.VMEM((n_items, GDN_HEAD_DIM, CHUNK), BF16),
            pltpu.VMEM((n_items, CHUNK, GDN_HEAD_DIM), F32),
            pltpu.VMEM((n_items, 1, LANES), F32),
        ],
        compiler_params=_params("parallel", "arbitrary"),
        name="gdn",
    )(proj, proj, proj, proj, gcol, grow, norm_w)


def _block_tail_kernel(a_ref, wo_ref, x_ref, g1_ref, sc_ref, sh_ref, g2_ref, wgu_ref, wd_ref, fw_ref, o_ref,
                       x1_scr, h_scr, act_scr, *, final_norm):
    d_ff = wd_ref.shape[0]
    x1 = x_ref[...] + g1_ref[0] * _dot(a_ref[...], wo_ref[...])
    x1_scr[...] = x1
    h_scr[...] = _modulated_norm(x1, sc_ref[0], sh_ref[0]).astype(BF16)
    for j in range(d_ff // FFN_TH):
        gate = _dot(h_scr[...], wgu_ref[:, j * FFN_TH:(j + 1) * FFN_TH])
        up = _dot(h_scr[...], wgu_ref[:, d_ff + j * FFN_TH:d_ff + (j + 1) * FFN_TH])
        act_scr[:, j * FFN_TH:(j + 1) * FFN_TH] = (gate * _sigmoid(gate) * up).astype(BF16)
    y = x1_scr[...] + g2_ref[0] * _dot(act_scr[...], wd_ref[...])
    if final_norm:
        ms = jnp.mean(y * y, axis=-1, keepdims=True)
        y = y * lax.rsqrt(ms + NORM_EPS) * fw_ref[...]
    o_ref[...] = y


def _block_tail(mixed, w_out, x2, g1, scale, shift, g2, w_gate_up, w_down, final_w, *, layer, seq, final_norm):
    tokens, d = x2.shape
    kdim = mixed.shape[1]
    d_ff = w_down.shape[1]
    tm = FFN_TM
    per_batch = seq // tm
    mod_spec = pl.BlockSpec((1, 1, d), lambda i: (i // per_batch, 0, 0))
    resident = functools.partial(pl.BlockSpec, index_map=lambda i: (0, 0), pipeline_mode=pl.Buffered(1))
    stacked = functools.partial(pl.BlockSpec, index_map=lambda i: (layer, 0, 0), pipeline_mode=pl.Buffered(1))
    return pl.pallas_call(
        functools.partial(_block_tail_kernel, final_norm=final_norm),
        grid=(tokens // tm,),
        in_specs=[
            pl.BlockSpec((tm, kdim), lambda i: (i, 0)),
            resident((kdim, d)),
            pl.BlockSpec((tm, d), lambda i: (i, 0)),
            mod_spec, mod_spec, mod_spec, mod_spec,
            stacked((None, d, 2 * d_ff)),
            stacked((None, d_ff, d)),
            resident((1, d)),
        ],
        out_specs=pl.BlockSpec((tm, d), lambda i: (i, 0)),
        out_shape=jax.ShapeDtypeStruct((tokens, d), F32),
        scratch_shapes=[pltpu.VMEM((tm, d), F32), pltpu.VMEM((tm, d), BF16), pltpu.VMEM((tm, d_ff), BF16)],
        compiler_params=_params("parallel"),
        name="block_tail",
    )(mixed, w_out, x2, g1, scale, shift, g2, w_gate_up, w_down, final_w)


def _pad_cast_kernel(wt_ref, o_ref, *, width):
    cols = wt_ref.shape[0]
    col = pl.program_id(1) * cols + _iota(wt_ref.shape, 0)
    o_ref[...] = jnp.where(col < width, wt_ref[...], 0.0).T.astype(o_ref.dtype)


def _padded_in_weights(w_in, n_main):
    layers, d, width = w_in.shape
    n_pad = n_main + LANES
    tiles = n_pad // LANES
    per_slab = max(t for t in range(1, WEIGHT_PREP_MAX_TILES + 1) if tiles % t == 0)
    cols = per_slab * LANES
    return pl.pallas_call(
        functools.partial(_pad_cast_kernel, width=width),
        grid=(layers, tiles // per_slab),
        in_specs=[pl.BlockSpec((None, cols, d), lambda l, j: (l, j, 0))],
        out_specs=pl.BlockSpec((None, d, cols), lambda l, j: (l, 0, j)),
        out_shape=jax.ShapeDtypeStruct((layers, d, n_pad), BF16),
        compiler_params=_params("parallel", "parallel"),
        name="pad_cast",
    )(jnp.swapaxes(w_in, 1, 2))


def _gate_vec(vec, offset, n_rows):
    count = vec.shape[0]
    col = jnp.zeros((1, LANES), F32).at[0, offset:offset + count].set(vec.astype(F32))
    row = jnp.zeros((n_rows, 1), F32).at[offset:offset + count, 0].set(vec.astype(F32))
    return col, row


def kernel(x, c, ada_w, ada_b, ml_w_in, ml_b_if, ml_norm_w, ml_w_out, gdn_w_in, gdn_conv_w, gdn_a_log,
           gdn_dt_bias, gdn_norm_w, gdn_w_out, ffn_w_gate_up, ffn_w_down, final_norm_w):
    batch, seq, d = x.shape
    depth = ada_w.shape[0]
    n_mixers = 2
    tokens = batch * seq
    x2 = x.reshape(tokens, d)

    c_rows = 8
    c_pad = jnp.pad(c, ((0, c_rows - batch), (0, 0)))
    mod = _adaln(c_pad, ada_w, ada_b)[:, :batch, :]

    w_gate_up = ffn_w_gate_up.astype(BF16)
    w_down = ffn_w_down.astype(BF16)
    for layer in range(depth):
        sh1, sc1, g1, sh2, sc2, g2 = [m.reshape(batch, 1, d) for m in jnp.split(mod[layer], 6, axis=-1)]
        jdx = layer // n_mixers
        if layer % n_mixers == 0:
            b_col, b_row = _gate_vec(ml_b_if[jdx], 0, ML_GATE_ROWS)
            proj, gcol, grow = _inproj(x2, sc1, sh1, _padded_in_weights(ml_w_in, ML_MAIN),
                                       b_col, b_row, mode="ml", layer=jdx, seq=seq)
            mixed = _mlstm(proj, gcol, grow, ml_norm_w[jdx].reshape(1, ML_V_WIDTH).astype(F32),
                           batch=batch, seq=seq)
            w_out = ml_w_out[jdx]
        else:
            b_col, b_row = _gate_vec(gdn_dt_bias[jdx], GDN_V_HEADS, GDN_GATE_ROWS)
            a_col, a_row = _gate_vec(gdn_a_log[jdx], GDN_V_HEADS, GDN_GATE_ROWS)
            proj, gcol, grow = _inproj(x2, sc1, sh1, _padded_in_weights(gdn_w_in, GDN_MAIN),
                                       b_col, b_row, (a_col, a_row, gdn_conv_w[jdx].astype(F32)),
                                       mode="gdn", layer=jdx, seq=seq)
            mixed = _gdn(proj, gcol, grow, gdn_norm_w[jdx].reshape(1, GDN_HEAD_DIM).astype(F32),
                         batch=batch, seq=seq)
            w_out = gdn_w_out[jdx]
        x2 = _block_tail(mixed, w_out.astype(BF16), x2, g1, sc2, sh2, g2,
                         w_gate_up, w_down, final_norm_w.reshape(1, d).astype(F32),
                         layer=layer, seq=seq, final_norm=(layer == depth - 1))
    return x2.reshape(batch, seq, d)
```

```python
import functools

import jax
import jax.numpy as jnp
from jax import lax
from jax.experimental import pallas as pl
from jax.experimental.pallas import tpu as pltpu

F32 = jnp.float32
BF16 = jnp.bfloat16
HIGHEST = lax.Precision.HIGHEST

LANES = 128
VMEM_LIMIT = 56 * 1024 * 1024

D_MODEL = 1024
CHUNK = 64
NORM_EPS = 1e-6

ML_HEADS = 4
ML_QK_DIM = 128
ML_V_DIM = 256
ML_QK_WIDTH = ML_HEADS * ML_QK_DIM
ML_V_WIDTH = ML_HEADS * ML_V_DIM
ML_MAIN = 2 * ML_QK_WIDTH + 2 * ML_V_WIDTH
ML_GATE_ROWS = 16
ML_GATE_CAP = 15.0

GDN_K_HEADS = 8
GDN_V_HEADS = 16
GDN_HEAD_DIM = 128
GDN_KEY_WIDTH = GDN_K_HEADS * GDN_HEAD_DIM
GDN_VAL_WIDTH = GDN_V_HEADS * GDN_HEAD_DIM
GDN_CONV_DIM = 2 * GDN_KEY_WIDTH + GDN_VAL_WIDTH
GDN_CONV_WIDTH = 4
GDN_MAIN = GDN_CONV_DIM + GDN_VAL_WIDTH
GDN_GATE_ROWS = 2 * GDN_V_HEADS

WEIGHT_PREP_MAX_TILES = 8
ADALN_TN = 2048
INPROJ_TM = 512
INPROJ_TN = 256
CONV_HALO = 8
PIPE_PIECES = 1
REC_ROWS = 512
GDN_ROWS = 256
FFN_TM = 512
FFN_TH = 256


def _params(*sem):
    return pltpu.CompilerParams(dimension_semantics=sem, vmem_limit_bytes=VMEM_LIMIT)


def _sigmoid(x):
    return 1.0 / (1.0 + jnp.exp(-x))


def _softplus(x):
    return jnp.maximum(x, 0.0) + jnp.log1p(jnp.exp(-jnp.abs(x)))


def _dot(a, b):
    return jnp.dot(a, b, preferred_element_type=F32)


def _dot_nt(a, b):
    return lax.dot_general(a, b, (((1,), (1,)), ((), ())), preferred_element_type=F32)


def _dot_tn(a, b):
    return lax.dot_general(a, b, (((0,), (0,)), ((), ())), preferred_element_type=F32)


def _iota(shape, dim):
    return lax.broadcasted_iota(jnp.int32, shape, dim)


def _adaln_kernel(c_ref, w_ref, b_ref, o_ref):
    c = c_ref[...]
    act = c * _sigmoid(c)
    w = w_ref[0]
    a1 = act.astype(BF16)
    a2 = (act - a1.astype(F32)).astype(BF16)
    w1 = w.astype(BF16)
    w2 = (w - w1.astype(F32)).astype(BF16)
    o_ref[0] = _dot(a1, w1) + (_dot(a1, w2) + _dot(a2, w1)) + b_ref[0]


def _adaln(c_pad, ada_w, ada_b):
    depth, d, n = ada_w.shape
    rows = c_pad.shape[0]
    tn = ADALN_TN
    return pl.pallas_call(
        _adaln_kernel,
        grid=(depth, n // tn),
        in_specs=[
            pl.BlockSpec((rows, d), lambda l, j: (0, 0)),
            pl.BlockSpec((1, d, tn), lambda l, j: (l, 0, j)),
            pl.BlockSpec((1, 1, tn), lambda l, j: (l, 0, j)),
        ],
        out_specs=pl.BlockSpec((1, rows, tn), lambda l, j: (l, 0, j)),
        out_shape=jax.ShapeDtypeStruct((depth, rows, n), F32),
        compiler_params=_params("parallel", "parallel"),
        name="adaln",
    )(c_pad, ada_w, ada_b.reshape(depth, 1, n))


def _modulated_norm(x, scale, shift):
    ms = jnp.mean(x * x, axis=-1, keepdims=True)
    return x * lax.rsqrt(ms + NORM_EPS) * (1.0 + scale) + shift


def _ml_gates(pre, idx):
    capped = ML_GATE_CAP * jnp.tanh(pre / ML_GATE_CAP)
    log_f = jnp.minimum(capped, 0.0) - jnp.log1p(jnp.exp(-jnp.abs(capped)))
    is_f = idx >= ML_HEADS
    return jnp.where(is_f, log_f, capped), is_f


def _gdn_gates(pre, a_log, idx):
    is_g = idx >= GDN_V_HEADS
    return jnp.where(is_g, -jnp.exp(a_log) * _softplus(pre), _sigmoid(pre)), is_g


def _inproj_kernel(*refs, mode, per_batch):
    if mode == "ml":
        (x_ref, sc_ref, sh_ref, w_ref, bcol_ref, brow_ref,
         proj_ref, gcol_ref, grow_ref, h_scr) = refs
    else:
        (x_ref, sc_ref, sh_ref, w_ref, bcol_ref, brow_ref, acol_ref, arow_ref, cw_ref,
         proj_ref, gcol_ref, grow_ref, h_scr, halo_scr, conv_buf) = refs
    tm = x_ref.shape[0]
    n = proj_ref.shape[1]
    n_rows = brow_ref.shape[0]
    tn = INPROJ_TN

    h_scr[...] = _modulated_norm(x_ref[...], sc_ref[0], sh_ref[0]).astype(BF16)

    def gate_stages():
        chunks = [slice(ch * CHUNK, (ch + 1) * CHUNK) for ch in range(tm // CHUNK)]
        lane = _iota((CHUNK, LANES), 1)
        row = _iota((n_rows, CHUNK), 0)
        rr = _iota((CHUNK, CHUNK), 0)
        cc = _iota((CHUNK, CHUNK), 1)
        lower = jnp.where(cc <= rr, 1.0, 0.0).astype(F32)
        upper = jnp.where(rr <= cc, 1.0, 0.0).astype(F32)
        wg = w_ref[:, n:n + LANES]
        wg_t = wg.astype(F32).T[:n_rows, :].astype(BF16)
        pre = _dot(h_scr[...], wg) + bcol_ref[...]
        pre_t = [_dot_nt(wg_t, h_scr[rs, :]) + brow_ref[...] for rs in chunks]
        yield
        if mode == "ml":
            col = [_ml_gates(pre[rs, :], lane) for rs in chunks]
        else:
            col = [_gdn_gates(pre[rs, :], acol_ref[...], lane) for rs in chunks]
        yield
        if mode == "ml":
            rowg = [_ml_gates(p_t, row) for p_t in pre_t]
        else:
            rowg = [_gdn_gates(p_t, arow_ref[...], row) for p_t in pre_t]
        yield
        def pieces(v):
            p1 = v.astype(BF16)
            r1 = v - p1.astype(F32)
            p2 = r1.astype(BF16)
            return p1, p2, (r1 - p2.astype(F32)).astype(BF16)

        lower_b, upper_b = lower.astype(BF16), upper.astype(BF16)
        csum = [sum(_dot(lower_b, p) for p in pieces(vals)) for vals, _ in col]
        csum_t = [sum(_dot(p, upper_b) for p in pieces(vals_t)) for vals_t, _ in rowg]
        yield
        for ch, rs in enumerate(chunks):
            gcol_ref[rs, :] = jnp.where(col[ch][1], csum[ch], col[ch][0])
            grow_ref[ch] = jnp.where(rowg[ch][1], csum_t[ch], rowg[ch][0])
        yield

    stages = gate_stages()
    order = list(range(n // tn))
    if mode == "gdn":
        first_of_seq = pl.program_id(0) % per_batch == 0
        conv_slabs = [j for j in order if j * tn < GDN_CONV_DIM]
        plain_slabs = [j for j in order if j * tn >= GDN_CONV_DIM]
        order = []
        while conv_slabs or plain_slabs:
            order += conv_slabs[:2] + plain_slabs[:1]
            conv_slabs, plain_slabs = conv_slabs[2:], plain_slabs[1:]
    d = x_ref.shape[1]
    kp, rp = d // PIPE_PIECES, tm // PIPE_PIECES

    def slab_cols(j):
        return slice(j * tn, (j + 1) * tn)

    def matmul_piece(j, kk):
        return _dot(h_scr[:, kk * kp:(kk + 1) * kp], w_ref[kk * kp:(kk + 1) * kp, slab_cols(j)])

    def start_epilogue(j, y):
        if mode == "gdn" and j * tn < GDN_CONV_DIM:
            halo_ref = halo_scr.at[:, slab_cols(j)]
            conv_buf[0:CONV_HALO, :] = jnp.where(first_of_seq, 0.0, halo_ref[...])
            conv_buf[CONV_HALO:CONV_HALO + tm, :] = y
            halo_ref[...] = y[tm - CONV_HALO:tm, :]

    def epilogue_piece(j, y, r):
        cs, rows = slab_cols(j), slice(r * rp, (r + 1) * rp)
        if not (mode == "gdn" and j * tn < GDN_CONV_DIM):
            proj_ref[rows, cs] = y[rows, :].astype(proj_ref.dtype)
            return
        conv_w = cw_ref[:, cs]
        acc = conv_w[GDN_CONV_WIDTH - 1:GDN_CONV_WIDTH, :] * y[rows, :]
        for tap in range(GDN_CONV_WIDTH - 1):
            first = CONV_HALO - (GDN_CONV_WIDTH - 1 - tap) + r * rp
            acc = acc + conv_w[tap:tap + 1, :] * conv_buf[first:first + rp, :]
        acc = acc * _sigmoid(acc)
        if j * tn >= 2 * GDN_KEY_WIDTH:
            proj_ref[rows, cs] = acc.astype(proj_ref.dtype)
            return
        q_scale = GDN_HEAD_DIM ** -0.5 if j * tn < GDN_KEY_WIDTH else 1.0
        for hh in range(tn // GDN_HEAD_DIM):
            hs = slice(hh * GDN_HEAD_DIM, (hh + 1) * GDN_HEAD_DIM)
            blk = acc[:, hs]
            ss = jnp.sum(blk * blk, axis=1, keepdims=True)
            proj_ref[rows, j * tn + hh * GDN_HEAD_DIM:j * tn + (hh + 1) * GDN_HEAD_DIM] = (
                blk * (lax.rsqrt(ss + NORM_EPS) * q_scale)).astype(proj_ref.dtype)

    y = _dot(h_scr[...], w_ref[:, slab_cols(order[0])])
    for idx, j in enumerate(order):
        next(stages, None)
        start_epilogue(j, y)
        following = order[idx + 1] if idx + 1 < len(order) else None
        y_next = None
        for piece in range(PIPE_PIECES):
            if following is not None:
                part = matmul_piece(following, piece)
                y_next = part if y_next is None else y_next + part
            epilogue_piece(j, y, piece)
        y = y_next
    for _ in stages:
        pass


def _inproj(x2, scale, shift, w_all, bias_col, bias_row, gdn_extra=None, *, mode, layer, seq):
    tokens, d = x2.shape
    n = ML_MAIN if mode == "ml" else GDN_MAIN
    n_rows = bias_row.shape[0]
    tm = INPROJ_TM
    per_batch = seq // tm
    const = lambda i: (0, 0)
    resident = functools.partial(pl.BlockSpec, index_map=const, pipeline_mode=pl.Buffered(1))
    in_specs = [
        pl.BlockSpec((tm, d), lambda i: (i, 0)),
        pl.BlockSpec((1, 1, d), lambda i: (i // per_batch, 0, 0)),
        pl.BlockSpec((1, 1, d), lambda i: (i // per_batch, 0, 0)),
        pl.BlockSpec((None,) + w_all.shape[1:], lambda i: (layer, 0, 0), pipeline_mode=pl.Buffered(1)),
        resident((1, LANES)),
        resident((n_rows, 1)),
    ]
    operands = [x2, scale, shift, w_all, bias_col, bias_row]
    scratch = [pltpu.VMEM((tm, d), BF16)]
    if mode == "gdn":
        alog_col, alog_row, conv_w = gdn_extra
        in_specs += [resident((1, LANES)), resident((n_rows, 1)), resident((GDN_CONV_WIDTH, GDN_CONV_DIM))]
        operands += [alog_col, alog_row, conv_w]
        scratch += [pltpu.VMEM((CONV_HALO, GDN_CONV_DIM), F32),
                    pltpu.VMEM((tm + CONV_HALO, INPROJ_TN), F32)]
    return pl.pallas_call(
        functools.partial(_inproj_kernel, mode=mode, per_batch=per_batch),
        grid=(tokens // tm,),
        in_specs=in_specs,
        out_specs=[
            pl.BlockSpec((tm, n), lambda i: (i, 0)),
            pl.BlockSpec((tm, LANES), lambda i: (i, 0)),
            pl.BlockSpec((tm // CHUNK, n_rows, CHUNK), lambda i: (i, 0, 0)),
        ],
        out_shape=[
            jax.ShapeDtypeStruct((tokens, n), BF16),
            jax.ShapeDtypeStruct((tokens, LANES), F32),
            jax.ShapeDtypeStruct((tokens // CHUNK, n_rows, CHUNK), F32),
        ],
        scratch_shapes=scratch,
        compiler_params=_params("arbitrary"),
        name="inproj_" + mode,
    )(*operands)


def _mlstm_kernel(q_ref, k_ref, v_ref, o_ref, gcol_ref, grow_ref, nw_ref, out_ref,
                  c_scr, n_scr, m_scr, cprev_scr):
    rows = q_ref.shape[0]

    @pl.when(pl.program_id(1) == 0)
    def _():
        c_scr[...] = jnp.zeros_like(c_scr)
        n_scr[...] = jnp.zeros_like(n_scr)
        m_scr[...] = jnp.zeros_like(m_scr)

    causal = _iota((CHUNK, CHUNK), 0) >= _iota((CHUNK, CHUNK), 1)
    k_scale = ML_QK_DIM ** -0.5
    n_ch = rows // CHUNK
    items = [(ch, h) for ch in range(n_ch) for h in range(ML_HEADS)]

    def row_slice(ch):
        return slice(ch * CHUNK, (ch + 1) * CHUNK)

    def qk_slice(h):
        return slice(h * ML_QK_DIM, (h + 1) * ML_QK_DIM)

    def v_slice(h):
        return slice(h * ML_V_DIM, (h + 1) * ML_V_DIM)

    n_items = len(items)

    lane = _iota((CHUNK, LANES), 1)

    def lane_rep(col):
        return jnp.broadcast_to(col, (CHUNK, LANES))

    def wide(a):
        return jnp.concatenate([a] * (ML_V_DIM // LANES), axis=1)

    b_col, b_last, d_mat, g_end, m_intra, m_loc, k_w, d_c, d_n, scores = ({} for _ in range(10))
    for it, (ch, h) in enumerate(items):
        rs = row_slice(ch)
        gc = gcol_ref[rs, :]
        gr = grow_ref[ch]
        li_c = lane_rep(jnp.sum(jnp.where(lane == h, gc, 0.0), axis=1, keepdims=True))
        b_c = lane_rep(jnp.sum(jnp.where(lane == ML_HEADS + h, gc, 0.0), axis=1, keepdims=True))
        li_r = gr[h:h + 1, :]
        b_r = gr[ML_HEADS + h:ML_HEADS + h + 1, :]
        b_col[it], b_last[it] = b_c, b_c[CHUNK - 1:CHUNK, :]
        d_mat[it] = jnp.where(causal, b_c[:, :CHUNK] - b_r + li_r, -jnp.inf)
        g_end[it] = b_last[it] - b_c + li_c
    for it in range(n_items):
        m_intra[it] = jnp.max(d_mat[it], axis=1, keepdims=True)
        m_loc[it] = jnp.max(g_end[it], axis=0, keepdims=True)
    for it, (ch, h) in enumerate(items):
        k_w[it] = (k_ref[row_slice(ch), qk_slice(h)].astype(F32)
                   * (k_scale * jnp.exp(g_end[it] - m_loc[it])))
    for it, (ch, h) in enumerate(items):
        rs = row_slice(ch)
        d_c[it] = _dot_tn(k_w[it].astype(BF16), v_ref[rs, v_slice(h)])
        d_n[it] = jnp.sum(k_w[it], axis=0, keepdims=True)
        scores[it] = _dot_nt(q_ref[rs, qk_slice(h)], k_ref[rs, qk_slice(h)]) * k_scale

    m_prev, n_prev = {}, {}
    m_run = [m_scr[h] for h in range(ML_HEADS)]
    n_run = [n_scr[h] for h in range(ML_HEADS)]
    for it, (ch, h) in enumerate(items):
        c_run = c_scr[h]
        cprev_scr[it] = c_run.astype(BF16)
        m_prev[it], n_prev[it] = m_run[h], n_run[h]
        m_new = jnp.maximum(b_last[it] + m_run[h], m_loc[it])
        a_old = jnp.exp(b_last[it] + m_run[h] - m_new)
        a_new = jnp.exp(m_loc[it] - m_new)
        c_scr[h] = wide(a_old) * c_run + wide(a_new) * d_c[it]
        n_run[h] = a_old * n_run[h] + a_new * d_n[it]
        m_run[h] = m_new
    for h in range(ML_HEADS):
        n_scr[h] = n_run[h]
        m_scr[h] = m_run[h]

    m_t, s_inter, p, num, den, hid, ms = ({} for _ in range(7))
    for it in range(n_items):
        m_inter = b_col[it] + m_prev[it]
        m_t[it] = jnp.maximum(m_inter, m_intra[it])
        s_inter[it] = jnp.exp(m_inter - m_t[it])
        p[it] = jnp.exp(d_mat[it] - m_t[it][:, :CHUNK]) * scores[it]
    for it, (ch, h) in enumerate(items):
        rs = row_slice(ch)
        q = q_ref[rs, qk_slice(h)]
        num[it] = (wide(s_inter[it]) * _dot(q, cprev_scr[it])
                   + _dot(p[it].astype(BF16), v_ref[rs, v_slice(h)]))
        den[it] = (s_inter[it] * jnp.sum(q.astype(F32) * n_prev[it], axis=1, keepdims=True)
                   + jnp.sum(p[it], axis=1, keepdims=True))
    for it in range(n_items):
        hid[it] = num[it] * wide(1.0 / jnp.maximum(jnp.abs(den[it]), jnp.exp(-m_t[it])))
        ms[it] = jnp.mean(hid[it] * hid[it], axis=1, keepdims=True)
    for it, (ch, h) in enumerate(items):
        rs, vs = row_slice(ch), v_slice(h)
        out = hid[it] * lax.rsqrt(ms[it] + NORM_EPS) * nw_ref[:, vs] * _sigmoid(o_ref[rs, vs].astype(F32))
        out_ref[rs, vs] = out.astype(out_ref.dtype)


def _mlstm(proj, gcol, grow, norm_w, *, batch, seq):
    tokens = proj.shape[0]
    rows = REC_ROWS
    steps = seq // rows
    row = lambda b, t: b * steps + t
    return pl.pallas_call(
        _mlstm_kernel,
        grid=(batch, steps),
        in_specs=[
            pl.BlockSpec((rows, ML_QK_WIDTH), lambda b, t: (row(b, t), 0)),
            pl.BlockSpec((rows, ML_QK_WIDTH), lambda b, t: (row(b, t), 1)),
            pl.BlockSpec((rows, ML_V_WIDTH), lambda b, t: (row(b, t), 1)),
            pl.BlockSpec((rows, ML_V_WIDTH), lambda b, t: (row(b, t), 2)),
            pl.BlockSpec((rows, LANES), lambda b, t: (row(b, t), 0)),
            pl.BlockSpec((rows // CHUNK, ML_GATE_ROWS, CHUNK), lambda b, t: (row(b, t), 0, 0)),
            pl.BlockSpec((1, ML_V_WIDTH), lambda b, t: (0, 0)),
        ],
        out_specs=pl.BlockSpec((rows, ML_V_WIDTH), lambda b, t: (row(b, t), 0)),
        out_shape=jax.ShapeDtypeStruct((tokens, ML_V_WIDTH), BF16),
        scratch_shapes=[
            pltpu.VMEM((ML_HEADS, ML_QK_DIM, ML_V_DIM), F32),
            pltpu.VMEM((ML_HEADS, 1, ML_QK_DIM), F32),
            pltpu.VMEM((ML_HEADS, 1, LANES), F32),
            pltpu.VMEM((ML_HEADS * rows // CHUNK, ML_QK_DIM, ML_V_DIM), BF16),
        ],
        compiler_params=_params("parallel", "arbitrary"),
        name="mlstm",
    )(proj, proj, proj, proj, gcol, grow, norm_w)


def _gdn_kernel(q_ref, k_ref, v_ref, z_ref, gcol_ref, grow_ref, nw_ref, out_ref,
                s_scr, wq_scr, attn_scr, kdt_scr, u_scr, eg_scr):
    rows = q_ref.shape[0]
    rep = GDN_V_HEADS // GDN_K_HEADS
    dh = GDN_HEAD_DIM
    n_kh = q_ref.shape[1] // dh
    n_vh = n_kh * rep
    n_ch = rows // CHUNK

    @pl.when(pl.program_id(1) == 0)
    def _():
        s_scr[...] = jnp.zeros_like(s_scr)

    ri = _iota((CHUNK, CHUNK), 0)
    ci = _iota((CHUNK, CHUNK), 1)
    causal = ri >= ci
    strict = ri > ci
    eye = jnp.where(ri == ci, 1.0, 0.0).astype(F32)
    lane = _iota((CHUNK, LANES), 1)
    merge_mask = {}
    s = 1
    while s < CHUNK:
        merge_mask[s] = (ri // s - ci // s == 1) & (ri // (2 * s) == ci // (2 * s))
        s *= 2

    kk, qk, qs, ks = {}, {}, {}, {}
    for ch in range(n_ch):
        rs = slice(ch * CHUNK, (ch + 1) * CHUNK)
        for kh in range(n_kh):
            hs = slice(kh * dh, (kh + 1) * dh)
            qb = q_ref[rs, hs]
            kb = k_ref[rs, hs]
            qs[ch, kh], ks[ch, kh] = qb.astype(F32), kb.astype(F32)
            kk[ch, kh] = _dot_nt(kb, kb)
            qk[ch, kh] = _dot_nt(qb, kb)

    items = [(ch, vh) for ch in range(n_ch) for vh in range(n_vh)]
    x, a_mats, rhs = {}, {}, {}
    for it, (ch, vh) in enumerate(items):
        rs = slice(ch * CHUNK, (ch + 1) * CHUNK)
        kh = vh // rep
        gc = gcol_ref[rs, :]
        beta_c = jnp.sum(jnp.where(lane == vh, gc, 0.0), axis=1, keepdims=True)
        g_c = jnp.sum(jnp.where(lane == GDN_V_HEADS + vh, gc, 0.0), axis=1, keepdims=True)
        g_r = grow_ref[ch, GDN_V_HEADS + vh:GDN_V_HEADS + vh + 1, :]
        g_last = g_r[:, CHUNK - 1:CHUNK]
        decay = jnp.exp(jnp.where(causal, g_c - g_r, -jnp.inf))
        a_mat = jnp.where(strict, kk[ch, kh] * beta_c * decay, 0.0)
        x[it] = eye - jnp.where(merge_mask[1], a_mat, 0.0)
        a_mats[it] = a_mat

        q, k = qs[ch, kh], ks[ch, kh]
        v = v_ref[rs, vh * dh:(vh + 1) * dh].astype(F32)
        e_g = jnp.exp(g_c)
        rhs[it] = jnp.concatenate([v * beta_c, k * (beta_c * e_g)], axis=1).astype(BF16)
        attn_scr[it] = (qk[ch, kh] * decay).astype(BF16)
        wq_scr[it, CHUNK:2 * CHUNK, :] = (q * e_g).astype(BF16)
        kdt_scr[it] = (k * jnp.exp(g_last - g_c)).T.astype(BF16)
        eg_scr[it] = jnp.broadcast_to(jnp.exp(g_last), (1, LANES))

    s = 2
    while s < CHUNK:
        for it in range(len(items)):
            nb = jnp.where(merge_mask[s], a_mats[it], 0.0).astype(BF16)
            y = _dot(nb, x[it].astype(BF16))
            x[it] = x[it] - _dot(x[it].astype(BF16), y.astype(BF16))
        s *= 2

    for it in range(len(items)):
        sol = _dot(x[it].astype(BF16), rhs[it])
        u_scr[it] = sol[:, :dh]
        wq_scr[it, 0:CHUNK, :] = sol[:, dh:].astype(BF16)

    state = [s_scr[vh] for vh in range(n_vh)]
    for ch in range(n_ch):
        rs = slice(ch * CHUNK, (ch + 1) * CHUNK)
        base = ch * n_vh
        sb = [state[vh].astype(BF16) for vh in range(n_vh)]
        r = [_dot(wq_scr[base + vh], sb[vh]) for vh in range(n_vh)]
        vnb = [(u_scr[base + vh] - r[vh][:CHUNK]).astype(BF16) for vh in range(n_vh)]
        state = [state[vh] * eg_scr[base + vh] + _dot(kdt_scr[base + vh], vnb[vh]) for vh in range(n_vh)]
        for vh in range(n_vh):
            out = r[vh][CHUNK:] + _dot(attn_scr[base + vh], vnb[vh])
            ms = jnp.mean(out * out, axis=1, keepdims=True)
            z = z_ref[rs, vh * dh:(vh + 1) * dh].astype(F32)
            gated = out * lax.rsqrt(ms + NORM_EPS) * nw_ref[...] * (z * _sigmoid(z))
            out_ref[rs, vh * dh:(vh + 1) * dh] = gated.astype(out_ref.dtype)
    for vh in range(n_vh):
        s_scr[vh] = state[vh]


def _gdn(proj, gcol, grow, norm_w, *, batch, seq):
    tokens = proj.shape[0]
    rows = GDN_ROWS
    steps = seq // rows
    n_items = GDN_V_HEADS * rows // CHUNK
    row = lambda b, t: b * steps + t
    return pl.pallas_call(
        _gdn_kernel,
        grid=(batch, steps),
        in_specs=[
            pl.BlockSpec((rows, GDN_KEY_WIDTH), lambda b, t: (row(b, t), 0)),
            pl.BlockSpec((rows, GDN_KEY_WIDTH), lambda b, t: (row(b, t), 1)),
            pl.BlockSpec((rows, GDN_VAL_WIDTH), lambda b, t: (row(b, t), 2 * GDN_KEY_WIDTH // GDN_VAL_WIDTH)),
            pl.BlockSpec((rows, GDN_VAL_WIDTH), lambda b, t: (row(b, t), GDN_CONV_DIM // GDN_VAL_WIDTH)),
            pl.BlockSpec((rows, LANES), lambda b, t: (row(b, t), 0)),
            pl.BlockSpec((rows // CHUNK, GDN_GATE_ROWS, CHUNK), lambda b, t: (row(b, t), 0, 0)),
            pl.BlockSpec((1, GDN_HEAD_DIM), lambda b, t: (0, 0)),
        ],
        out_specs=pl.BlockSpec((rows, GDN_VAL_WIDTH), lambda b, t: (row(b, t), 0)),
        out_shape=jax.ShapeDtypeStruct((tokens, GDN_VAL_WIDTH), BF16),
        scratch_shapes=[
            pltpu.VMEM((GDN_V_HEADS, GDN_HEAD_DIM, GDN_HEAD_DIM), F32),
            pltpu.VMEM((n_items, 2 * CHUNK, GDN_HEAD_DIM), BF16),
            pltpu.VMEM((n_items, CHUNK, CHUNK), BF16),
            pltpu.VMEM((n_items, GDN_HEAD_DIM, CHUNK), BF16),
            pltpu.VMEM((n_items, CHUNK, GDN_HEAD_DIM), F32),
            pltpu.VMEM((n_items, 1, LANES), F32),
        ],
        compiler_params=_params("parallel", "arbitrary"),
        name="gdn",
    )(proj, proj, proj, proj, gcol, grow, norm_w)


def _block_tail_kernel(a_ref, wo_ref, x_ref, g1_ref, sc_ref, sh_ref, g2_ref, wgu_ref, wd_ref, fw_ref, o_ref,
                       x1_scr, h_scr, act_scr, *, final_norm):
    d_ff = wd_ref.shape[0]
    x1 = x_ref[...] + g1_ref[0] * _dot(a_ref[...], wo_ref[...])
    x1_scr[...] = x1
    h_scr[...] = _modulated_norm(x1, sc_ref[0], sh_ref[0]).astype(BF16)
    for j in range(d_ff // FFN_TH):
        gate = _dot(h_scr[...], wgu_ref[:, j * FFN_TH:(j + 1) * FFN_TH])
        up = _dot(h_scr[...], wgu_ref[:, d_ff + j * FFN_TH:d_ff + (j + 1) * FFN_TH])
        act_scr[:, j * FFN_TH:(j + 1) * FFN_TH] = (gate * _sigmoid(gate) * up).astype(BF16)
    y = x1_scr[...] + g2_ref[0] * _dot(act_scr[...], wd_ref[...])
    if final_norm:
        ms = jnp.mean(y * y, axis=-1, keepdims=True)
        y = y * lax.rsqrt(ms + NORM_EPS) * fw_ref[...]
    o_ref[...] = y


def _block_tail(mixed, w_out, x2, g1, scale, shift, g2, w_gate_up, w_down, final_w, *, layer, seq, final_norm):
    tokens, d = x2.shape
    kdim = mixed.shape[1]
    d_ff = w_down.shape[1]
    tm = FFN_TM
    per_batch = seq // tm
    mod_spec = pl.BlockSpec((1, 1, d), lambda i: (i // per_batch, 0, 0))
    resident = functools.partial(pl.BlockSpec, index_map=lambda i: (0, 0), pipeline_mode=pl.Buffered(1))
    stacked = functools.partial(pl.BlockSpec, index_map=lambda i: (layer, 0, 0), pipeline_mode=pl.Buffered(1))
    return pl.pallas_call(
        functools.partial(_block_tail_kernel, final_norm=final_norm),
        grid=(tokens // tm,),
        in_specs=[
            pl.BlockSpec((tm, kdim), lambda i: (i, 0)),
            resident((kdim, d)),
            pl.BlockSpec((tm, d), lambda i: (i, 0)),
            mod_spec, mod_spec, mod_spec, mod_spec,
            stacked((None, d, 2 * d_ff)),
            stacked((None, d_ff, d)),
            resident((1, d)),
        ],
        out_specs=pl.BlockSpec((tm, d), lambda i: (i, 0)),
        out_shape=jax.ShapeDtypeStruct((tokens, d), F32),
        scratch_shapes=[pltpu.VMEM((tm, d), F32), pltpu.VMEM((tm, d), BF16), pltpu.VMEM((tm, d_ff), BF16)],
        compiler_params=_params("parallel"),
        name="block_tail",
    )(mixed, w_out, x2, g1, scale, shift, g2, w_gate_up, w_down, final_w)


def _pad_cast_kernel(wt_ref, o_ref, *, width):
    cols = wt_ref.shape[0]
    col = pl.program_id(1) * cols + _iota(wt_ref.shape, 0)
    o_ref[...] = jnp.where(col < width, wt_ref[...], 0.0).T.astype(o_ref.dtype)


def _padded_in_weights(w_in, n_main):
    layers, d, width = w_in.shape
    n_pad = n_main + LANES
    tiles = n_pad // LANES
    per_slab = max(t for t in range(1, WEIGHT_PREP_MAX_TILES + 1) if tiles % t == 0)
    cols = per_slab * LANES
    return pl.pallas_call(
        functools.partial(_pad_cast_kernel, width=width),
        grid=(layers, tiles // per_slab),
        in_specs=[pl.BlockSpec((None, cols, d), lambda l, j: (l, j, 0))],
        out_specs=pl.BlockSpec((None, d, cols), lambda l, j: (l, 0, j)),
        out_shape=jax.ShapeDtypeStruct((layers, d, n_pad), BF16),
        compiler_params=_params("parallel", "parallel"),
        name="pad_cast",
    )(jnp.swapaxes(w_in, 1, 2))


def _gate_vec(vec, offset, n_rows):
    count = vec.shape[0]
    col = jnp.zeros((1, LANES), F32).at[0, offset:offset + count].set(vec.astype(F32))
    row = jnp.zeros((n_rows, 1), F32).at[offset:offset + count, 0].set(vec.astype(F32))
    return col, row


def kernel(x, c, ada_w, ada_b, ml_w_in, ml_b_if, ml_norm_w, ml_w_out, gdn_w_in, gdn_conv_w, gdn_a_log,
           gdn_dt_bias, gdn_norm_w, gdn_w_out, ffn_w_gate_up, ffn_w_down, final_norm_w):
    batch, seq, d = x.shape
    depth = ada_w.shape[0]
    n_mixers = 2
    tokens = batch * seq
    x2 = x.reshape(tokens, d)

    c_rows = 8
    c_pad = jnp.pad(c, ((0, c_rows - batch), (0, 0)))
    mod = _adaln(c_pad, ada_w, ada_b)[:, :batch, :]

    w_gate_up = ffn_w_gate_up.astype(BF16)
    w_down = ffn_w_down.astype(BF16)
    for layer in range(depth):
        sh1, sc1, g1, sh2, sc2, g2 = [m.reshape(batch, 1, d) for m in jnp.split(mod[layer], 6, axis=-1)]
        jdx = layer // n_mixers
        if layer % n_mixers == 0:
            b_col, b_row = _gate_vec(ml_b_if[jdx], 0, ML_GATE_ROWS)
            proj, gcol, grow = _inproj(x2, sc1, sh1, _padded_in_weights(ml_w_in, ML_MAIN),
                                       b_col, b_row, mode="ml", layer=jdx, seq=seq)
            mixed = _mlstm(proj, gcol, grow, ml_norm_w[jdx].reshape(1, ML_V_WIDTH).astype(F32),
                           batch=batch, seq=seq)
            w_out = ml_w_out[jdx]
        else:
            b_col, b_row = _gate_vec(gdn_dt_bias[jdx], GDN_V_HEADS, GDN_GATE_ROWS)
            a_col, a_row = _gate_vec(gdn_a_log[jdx], GDN_V_HEADS, GDN_GATE_ROWS)
            proj, gcol, grow = _inproj(x2, sc1, sh1, _padded_in_weights(gdn_w_in, GDN_MAIN),
                                       b_col, b_row, (a_col, a_row, gdn_conv_w[jdx].astype(F32)),
                                       mode="gdn", layer=jdx, seq=seq)
            mixed = _gdn(proj, gcol, grow, gdn_norm_w[jdx].reshape(1, GDN_HEAD_DIM).astype(F32),
                         batch=batch, seq=seq)
            w_out = gdn_w_out[jdx]
        x2 = _block_tail(mixed, w_out.astype(BF16), x2, g1, sc2, sh2, g2,
                         w_gate_up, w_down, final_norm_w.reshape(1, d).astype(F32),
                         layer=layer, seq=seq, final_norm=(layer == depth - 1))
    return x2.reshape(batch, seq, d)
```
